```python
import math
import jax, jax.numpy as jnp
from jax import lax
import numpy as np

D_MODEL = 2048
BATCH = 2
SEQ = 4096
DEPTH = 4
DEC_BATCH = 128
DEC_SEQ = 4
PAST_LEN = 8192
PAGE_SIZE = 128

N_MIXERS = 3
N_POOL_LAYERS = (DEPTH + 2) // N_MIXERS
N_MLA_LAYERS = (DEPTH + 1) // N_MIXERS
N_SWA_LAYERS = DEPTH // N_MIXERS

POOL_WINDOWS = (2, 4, 8, 16)
POOL_GROUPS = len(POOL_WINDOWS)
POOL_GC = D_MODEL // POOL_GROUPS
POOL_BUF = max(POOL_WINDOWS) - 1

MLA_HEADS = 16
MLA_Q_RANK = 512
MLA_KV_RANK = 512
MLA_NOPE = 128
MLA_ROPE = 64
MLA_V = 128
MLA_SCALE = 1.0 / math.sqrt(MLA_NOPE + MLA_ROPE)
ROPE_THETA = 10000.0
Q_BLOCK = 128

SWA_HEADS = 32
SWA_KV_HEADS = 8
SWA_GROUP = SWA_HEADS // SWA_KV_HEADS
SWA_HEAD_DIM = 64
SWA_SCALE = 1.0 / math.sqrt(SWA_HEAD_DIM)
WINDOW = 128

D_FF = 5632
CONV_W = 3

ALPHA = (2.0 * DEPTH) ** 0.25
BETA = (8.0 * DEPTH) ** -0.25
LN_EPS = 1e-5
RMS_EPS = 1e-6
NEG_INF = -1e30

kernel_name = 'hybrid_pool_mla_swa_convffn_step'


def _layer_norm(x, g, b):
    xf = x.astype(jnp.float32)
    mu = xf.mean(-1, keepdims=True)
    var = jnp.square(xf - mu).mean(-1, keepdims=True)
    return ((xf - mu) * lax.rsqrt(var + LN_EPS) * g + b).astype(x.dtype)


def _rms_norm(x, g):
    xf = x.astype(jnp.float32)
    return (xf * lax.rsqrt(jnp.square(xf).mean(-1, keepdims=True) + RMS_EPS) * g).astype(x.dtype)


def _rope(x, pos):
    half = x.shape[-1] // 2
    inv = ROPE_THETA ** (-jnp.arange(half, dtype=jnp.float32) / half)
    ang = pos.astype(jnp.float32)[:, None] * inv[None, :]
    shape = (1, pos.shape[0]) + (1,) * (x.ndim - 3) + (half,)
    cos = jnp.cos(ang).reshape(shape)
    sin = jnp.sin(ang).reshape(shape)
    xf = x.astype(jnp.float32)
    x1, x2 = xf[..., :half], xf[..., half:]
    return jnp.concatenate([x1 * cos - x2 * sin, x2 * cos + x1 * sin], axis=-1).astype(x.dtype)


def _alibi_slopes():
    h = jnp.arange(1, SWA_HEADS + 1, dtype=jnp.float32)
    return (2.0 ** (-8.0 * h / SWA_HEADS)).reshape(SWA_KV_HEADS, SWA_GROUP)


def _pool_mix(x, prev, start_pos, w_pool, scale):
    B, T, D = x.shape
    nb = prev.shape[1]
    ext = jnp.concatenate([prev.astype(x.dtype), x], axis=1)
    csum = jnp.concatenate([jnp.zeros((B, 1, D), jnp.float32),
                            jnp.cumsum(ext.astype(jnp.float32), axis=1)], axis=1)
    pos = start_pos + jnp.arange(T)
    means = []
    for g, wnd in enumerate(POOL_WINDOWS):
        cs = csum[..., g * POOL_GC:(g + 1) * POOL_GC]
        win_sum = cs[:, nb + 1:nb + 1 + T] - cs[:, nb + 1 - wnd:nb + 1 - wnd + T]
        cnt = jnp.minimum(pos + 1, wnd).astype(jnp.float32)
        means.append(win_sum / cnt[None, :, None])
    pooled = (jnp.concatenate(means, axis=-1) - x.astype(jnp.float32)).astype(x.dtype)
    y = jnp.einsum('btgc,gcd->btgd', pooled.reshape(B, T, POOL_GROUPS, POOL_GC), w_pool)
    return y.reshape(B, T, D) * scale, ext[:, -POOL_BUF:]


def _mla_project(x, pos, w_a, g_q, g_kv, w_uq, w_uk):
    a = jnp.einsum('btd,de->bte', x, w_a)
    cq = _rms_norm(a[..., :MLA_Q_RANK], g_q)
    ckv = _rms_norm(a[..., MLA_Q_RANK:MLA_Q_RANK + MLA_KV_RANK], g_kv)
    kpe = _rope(a[..., MLA_Q_RANK + MLA_KV_RANK:], pos)
    q = jnp.einsum('btr,rhe->bthe', cq, w_uq)
    q_lat = jnp.einsum('bthn,rhn->bthr', q[..., :MLA_NOPE], w_uk)
    q_pe = _rope(q[..., MLA_NOPE:], pos)
    return q_lat, q_pe, ckv, kpe


def _mla_scores(q_lat, q_pe, ckv, kpe):
    s = jnp.einsum('bqhr,bkr->bhqk', q_lat, ckv, preferred_element_type=jnp.float32)
    s = s + jnp.einsum('bqhe,bke->bhqk', q_pe, kpe, preferred_element_type=jnp.float32)
    return s * MLA_SCALE


def _mla_output(o_lat, w_uv, w_o):
    o = jnp.einsum('bthr,rhv->bthv', o_lat, w_uv)
    return jnp.einsum('bthv,hvd->btd', o, w_o)


def _mla_prompt_attend(q_lat, q_pe, ckv, kpe):
    B, S, H, R = q_lat.shape
    nb = S // Q_BLOCK
    qb = q_lat.reshape(B, nb, Q_BLOCK, H, R).swapaxes(0, 1)
    pb = q_pe.reshape(B, nb, Q_BLOCK, H, MLA_ROPE).swapaxes(0, 1)
    kpos = jnp.arange(S)
    ckv_f = ckv.astype(jnp.float32)

    def one_block(args):
        i, ql, qp = args
        s = _mla_scores(ql, qp, ckv, kpe)
        qpos = i * Q_BLOCK + jnp.arange(Q_BLOCK)
        s = jnp.where(kpos[None, :] <= qpos[:, None], s, NEG_INF)
        p = jax.nn.softmax(s, axis=-1)
        return jnp.einsum('bhqk,bkr->bqhr', p, ckv_f)

    o = lax.map(one_block, (jnp.arange(nb), qb, pb))
    return o.swapaxes(0, 1).reshape(B, S, H, R)


def _online_update(carry, s, vals):
    m, l, acc = carry
    m_new = jnp.maximum(m, s.max(-1))
    corr = jnp.exp(m - m_new)
    p = jnp.exp(s - m_new[..., None])
    l_new = l * corr + p.sum(-1)
    acc_new = acc * corr[..., None] + jnp.einsum('bhtk,bkr->bhtr', p, vals.astype(jnp.float32))
    return (m_new, l_new, acc_new)


def _mla_sample_attend(q_lat, q_pe, ckv_new, kpe_new, cache_ckv, cache_kpe, layer, page_table):
    Bd, T, H, R = q_lat.shape
    carry0 = (jnp.full((Bd, H, T), NEG_INF, jnp.float32),
              jnp.zeros((Bd, H, T), jnp.float32),
              jnp.zeros((Bd, H, T, R), jnp.float32))

    def page_step(carry, phys):
        ck = cache_ckv[layer, phys]
        kp = cache_kpe[layer, phys]
        return _online_update(carry, _mla_scores(q_lat, q_pe, ck, kp), ck), None

    carry, _ = lax.scan(page_step, carry0, page_table.T)
    s = _mla_scores(q_lat, q_pe, ckv_new, kpe_new)
    s = jnp.where(jnp.tril(jnp.ones((T, T), bool)), s, NEG_INF)
    m, l, acc = _online_update(carry, s, ckv_new)
    return (acc / l[..., None]).transpose(0, 2, 1, 3)


def _swa_project(x, w_qkv, b_qkv):
    B, T, _ = x.shape
    qkv = jnp.einsum('btd,de->bte', x, w_qkv) + b_qkv
    nq = SWA_HEADS * SWA_HEAD_DIM
    nkv = SWA_KV_HEADS * SWA_HEAD_DIM
    q = qkv[..., :nq].reshape(B, T, SWA_KV_HEADS, SWA_GROUP, SWA_HEAD_DIM)
    k = qkv[..., nq:nq + nkv].reshape(B, T, SWA_KV_HEADS, SWA_HEAD_DIM)
    v = qkv[..., nq + nkv:].reshape(B, T, SWA_KV_HEADS, SWA_HEAD_DIM)
    return q, k, v


def _swa_attend(q, k, v, qpos, kpos, sinks, slopes):
    s = jnp.einsum('bnqhgd,bnkhd->bnhgqk', q, k, preferred_element_type=jnp.float32) * SWA_SCALE
    dist = qpos[:, :, None] - kpos[:, None, :]
    valid = (dist >= 0) & (dist < WINDOW) & (kpos[:, None, :] >= 0)
    s = s - slopes[None, None, :, :, None, None] * dist.astype(jnp.float32)[None, :, None, None]
    s = jnp.where(valid[None, :, None, None], s, NEG_INF)
    sink = jnp.broadcast_to(sinks.astype(jnp.float32).reshape(SWA_KV_HEADS, SWA_GROUP)[None, None, :, :, None, None],
                            s.shape[:-1] + (1,))
    p = jax.nn.softmax(jnp.concatenate([s, sink], axis=-1), axis=-1)[..., :-1]
    return jnp.einsum('bnhgqk,bnkhd->bnqhgd', p, v.astype(jnp.float32))


def _swa_prompt(x, w_qkv, b_qkv, sinks, w_o, b_o, slopes):
    B, S, _ = x.shape
    q, k, v = _swa_project(x, w_qkv, b_qkv)
    nb = S // WINDOW
    qb = q.reshape(B, nb, WINDOW, SWA_KV_HEADS, SWA_GROUP, SWA_HEAD_DIM)

    def band(t):
        prev = jnp.concatenate([jnp.zeros_like(t[:, :WINDOW]), t[:, :S - WINDOW]], axis=1)
        shp = (B, nb, WINDOW, SWA_KV_HEADS, SWA_HEAD_DIM)
        return jnp.concatenate([prev.reshape(shp), t.reshape(shp)], axis=2)

    starts = jnp.arange(nb)[:, None] * WINDOW
    qpos = starts + jnp.arange(WINDOW)[None, :]
    kpos = starts - WINDOW + jnp.arange(2 * WINDOW)[None, :]
    o = _swa_attend(qb, band(k), band(v), qpos, kpos, sinks, slopes)
    y = jnp.einsum('bse,ed->bsd', o.reshape(B, S, -1).astype(x.dtype), w_o) + b_o
    return y, k[:, S - WINDOW:], v[:, S - WINDOW:]


def _swa_sample(x, k_buf, v_buf, w_qkv, b_qkv, sinks, w_o, b_o, slopes):
    B, T, _ = x.shape
    buf = k_buf.shape[1]
    q, k, v = _swa_project(x, w_qkv, b_qkv)
    k_ext = jnp.concatenate([k_buf.astype(k.dtype), k], axis=1)
    v_ext = jnp.concatenate([v_buf.astype(v.dtype), v], axis=1)
    qpos = (PAST_LEN + jnp.arange(T))[None, :]
    kpos = (PAST_LEN - buf + jnp.arange(buf + T))[None, :]
    o = _swa_attend(q[:, None], k_ext[:, None], v_ext[:, None], qpos, kpos, sinks, slopes)
    y = jnp.einsum('bse,ed->bsd', o.reshape(B, T, -1).astype(x.dtype), w_o) + b_o
    return y, k_ext[:, T:], v_ext[:, T:]


def _conv_ffn(x, prev, w_in, w_conv, b_conv, w_out):
    T = x.shape[1]
    u = jnp.einsum('btd,df->btf', x, w_in)
    gate, val = u[..., :D_FF], u[..., D_FF:]
    ext = jnp.concatenate([prev.astype(gate.dtype), gate], axis=1)
    conv = b_conv
    for j in range(CONV_W):
        conv = conv + ext[:, j:j + T] * w_conv[j]
    h = jax.nn.gelu(conv, approximate=False) * val
    return jnp.einsum('btf,fd->btd', h, w_out), ext[:, -(CONV_W - 1):]


def setup_inputs(seed: int = 0) -> dict:
    key = jax.random.key(seed)
    ks = jax.random.split(key, 32)

    def nrm(k, shape, scale):
        return scale * jax.random.normal(k, shape, jnp.float32)

    n_pages = PAST_LEN // PAGE_SIZE
    n_phys = (5 * DEC_BATCH * n_pages + 3) // 4
    buf = min(WINDOW, PAST_LEN)
    qk_dim = MLA_NOPE + MLA_ROPE
    qkv_dim = (SWA_HEADS + 2 * SWA_KV_HEADS) * SWA_HEAD_DIM
    page_table = jax.random.permutation(ks[5], n_phys)[:DEC_BATCH * n_pages]
    page_table = page_table.reshape(DEC_BATCH, n_pages).astype(jnp.int32)
    return {
        'x_prompt': nrm(ks[0], (BATCH, SEQ, D_MODEL), 1.0),
        'x_sample': nrm(ks[1], (DEC_BATCH, DEC_SEQ, D_MODEL), 1.0),
        'state_pool': nrm(ks[2], (N_POOL_LAYERS, DEC_BATCH, POOL_BUF, D_MODEL), 1.0),
        'cache_mla_ckv': nrm(ks[3], (N_MLA_LAYERS, n_phys, PAGE_SIZE, MLA_KV_RANK), 1.0),
        'cache_mla_kpe': nrm(ks[4], (N_MLA_LAYERS, n_phys, PAGE_SIZE, MLA_ROPE), 1.0),
        'page_table': page_table,
        'cache_swa_k': nrm(ks[6], (N_SWA_LAYERS, DEC_BATCH, buf, SWA_KV_HEADS, SWA_HEAD_DIM), 1.0),
        'cache_swa_v': nrm(ks[7], (N_SWA_LAYERS, DEC_BATCH, buf, SWA_KV_HEADS, SWA_HEAD_DIM), 1.0),
        'state_ffn_conv': nrm(ks[8], (DEPTH, DEC_BATCH, CONV_W - 1, D_FF), 1.0),
        'ln_g': 1.0 + nrm(ks[9], (DEPTH, 2, D_MODEL), 0.02),
        'ln_b': nrm(ks[10], (DEPTH, 2, D_MODEL), 0.02),
        'pool_w': nrm(ks[11], (N_POOL_LAYERS, POOL_GROUPS, POOL_GC, POOL_GC), BETA * POOL_GC ** -0.5),
        'pool_scale': 1.0 + nrm(ks[12], (N_POOL_LAYERS, D_MODEL), 0.02),
        'mla_w_a': nrm(ks[13], (N_MLA_LAYERS, D_MODEL, MLA_Q_RANK + MLA_KV_RANK + MLA_ROPE), D_MODEL ** -0.5),
        'mla_g_q': 1.0 + nrm(ks[14], (N_MLA_LAYERS, MLA_Q_RANK), 0.02),
        'mla_g_kv': 1.0 + nrm(ks[15], (N_MLA_LAYERS, MLA_KV_RANK), 0.02),
        'mla_w_uq': nrm(ks[16], (N_MLA_LAYERS, MLA_Q_RANK, MLA_HEADS, qk_dim), MLA_Q_RANK ** -0.5),
        'mla_w_uk': nrm(ks[17], (N_MLA_LAYERS, MLA_KV_RANK, MLA_HEADS, MLA_NOPE), MLA_KV_RANK ** -0.5),
        'mla_w_uv': nrm(ks[18], (N_MLA_LAYERS, MLA_KV_RANK, MLA_HEADS, MLA_V), MLA_KV_RANK ** -0.5),
        'mla_w_o': nrm(ks[19], (N_MLA_LAYERS, MLA_HEADS, MLA_V, D_MODEL), BETA * (MLA_HEADS * MLA_V) ** -0.5),
        'swa_w_qkv': nrm(ks[20], (N_SWA_LAYERS, D_MODEL, qkv_dim), D_MODEL ** -0.5),
        'swa_b_qkv': nrm(ks[21], (N_SWA_LAYERS, qkv_dim), 0.02),
        'swa_sinks': nrm(ks[22], (N_SWA_LAYERS, SWA_HEADS), 0.5),
        'swa_w_o': nrm(ks[23], (N_SWA_LAYERS, SWA_HEADS * SWA_HEAD_DIM, D_MODEL), BETA * (SWA_HEADS * SWA_HEAD_DIM) ** -0.5),
        'swa_b_o': nrm(ks[24], (N_SWA_LAYERS, D_MODEL), 0.02),
        'ffn_w_in': nrm(ks[25], (DEPTH, D_MODEL, 2 * D_FF), D_MODEL ** -0.5),
        'ffn_conv_w': nrm(ks[26], (DEPTH, CONV_W, D_FF), CONV_W ** -0.5),
        'ffn_conv_b': nrm(ks[27], (DEPTH, D_FF), 0.02),
        'ffn_w_out': nrm(ks[28], (DEPTH, D_FF, D_MODEL), BETA * D_FF ** -0.5),
    }


def reference(x_prompt, x_sample, state_pool, cache_mla_ckv, cache_mla_kpe, page_table,
              cache_swa_k, cache_swa_v, state_ffn_conv, ln_g, ln_b, pool_w, pool_scale,
              mla_w_a, mla_g_q, mla_g_kv, mla_w_uq, mla_w_uk, mla_w_uv, mla_w_o,
              swa_w_qkv, swa_b_qkv, swa_sinks, swa_w_o, swa_b_o,
              ffn_w_in, ffn_conv_w, ffn_conv_b, ffn_w_out):
    B, S, _ = x_prompt.shape
    T = x_sample.shape[1]
    pos_p = jnp.arange(S)
    pos_s = PAST_LEN + jnp.arange(T)
    slopes = _alibi_slopes()
    xp, xs = x_prompt, x_sample
    pool_p, pool_s, ckv_p, ckv_s, kpe_p, kpe_s = [], [], [], [], [], []
    swk_p, swk_s, swv_p, swv_s, conv_p, conv_s = [], [], [], [], [], []
    for i in range(DEPTH):
        kind, j = i % N_MIXERS, i // N_MIXERS
        if kind == 0:
            hp, st = _pool_mix(xp, jnp.zeros((B, POOL_BUF, D_MODEL), xp.dtype), 0, pool_w[j], pool_scale[j])
            pool_p.append(st)
            hs, st = _pool_mix(xs, state_pool[j], PAST_LEN, pool_w[j], pool_scale[j])
            pool_s.append(st)
        elif kind == 1:
            ql, qp, ckv, kpe = _mla_project(xp, pos_p, mla_w_a[j], mla_g_q[j], mla_g_kv[j], mla_w_uq[j], mla_w_uk[j])
            hp = _mla_output(_mla_prompt_attend(ql, qp, ckv, kpe).astype(xp.dtype), mla_w_uv[j], mla_w_o[j])
            ckv_p.append(ckv)
            kpe_p.append(kpe)
            ql, qp, ckv, kpe = _mla_project(xs, pos_s, mla_w_a[j], mla_g_q[j], mla_g_kv[j], mla_w_uq[j], mla_w_uk[j])
            o_lat = _mla_sample_attend(ql, qp, ckv, kpe, cache_mla_ckv, cache_mla_kpe, j, page_table)
            hs = _mla_output(o_lat.astype(xs.dtype), mla_w_uv[j], mla_w_o[j])
            ckv_s.append(ckv)
            kpe_s.append(kpe)
        else:
            hp, kb, vb = _swa_prompt(xp, swa_w_qkv[j], swa_b_qkv[j], swa_sinks[j], swa_w_o[j], swa_b_o[j], slopes)
            swk_p.append(kb)
            swv_p.append(vb)
            hs, kb, vb = _swa_sample(xs, cache_swa_k[j], cache_swa_v[j], swa_w_qkv[j], swa_b_qkv[j],
                                     swa_sinks[j], swa_w_o[j], swa_b_o[j], slopes)
            swk_s.append(kb)
            swv_s.append(vb)
        xp = _layer_norm(ALPHA * xp + hp, ln_g[i, 0], ln_b[i, 0])
        xs = _layer_norm(ALPHA * xs + hs, ln_g[i, 0], ln_b[i, 0])
        fp, cp = _conv_ffn(xp, jnp.zeros((B, CONV_W - 1, D_FF), xp.dtype), ffn_w_in[i], ffn_conv_w[i], ffn_conv_b[i], ffn_w_out[i])
        fs, cs = _conv_ffn(xs, state_ffn_conv[i], ffn_w_in[i], ffn_conv_w[i], ffn_conv_b[i], ffn_w_out[i])
        conv_p.append(cp)
        conv_s.append(cs)
        xp = _layer_norm(ALPHA * xp + fp, ln_g[i, 1], ln_b[i, 1])
        xs = _layer_norm(ALPHA * xs + fs, ln_g[i, 1], ln_b[i, 1])
    return (xp, xs, jnp.stack(pool_p), jnp.stack(pool_s), jnp.stack(ckv_p), jnp.stack(ckv_s),
            jnp.stack(kpe_p), jnp.stack(kpe_s), jnp.stack(swk_p), jnp.stack(swk_s),
            jnp.stack(swv_p), jnp.stack(swv_s), jnp.stack(conv_p), jnp.stack(conv_s))
```

```python
import functools
import math

import jax
import jax.numpy as jnp
import numpy as np
from jax import lax
from jax.experimental import pallas as pl
from jax.experimental.pallas import tpu as pltpu

D_MODEL = 2048
DEPTH = 4
N_MIXERS = 3
PAST_LEN = 8192
PAGE_SIZE = 128

POOL_WINDOWS = (2, 4, 8, 16)
POOL_GROUPS = len(POOL_WINDOWS)
POOL_GC = D_MODEL // POOL_GROUPS
POOL_BUF = max(POOL_WINDOWS) - 1

MLA_HEADS = 16
MLA_Q_RANK = 512
MLA_KV_RANK = 512
MLA_NOPE = 128
MLA_ROPE = 64
MLA_V = 128
MLA_SCALE = 1.0 / math.sqrt(MLA_NOPE + MLA_ROPE)
ROPE_THETA = 10000.0

SWA_HEADS = 32
SWA_KV_HEADS = 8
SWA_GROUP = SWA_HEADS // SWA_KV_HEADS
SWA_HEAD_DIM = 64
SWA_SCALE = 1.0 / math.sqrt(SWA_HEAD_DIM)
WINDOW = 128

D_FF = 5632
CONV_W = 3

ALPHA = (2.0 * DEPTH) ** 0.25
LN_EPS = 1e-5
RMS_EPS = 1e-6
NEG_INF = -1e30

LANES = 128
SUBLANES = 8
BF16_ROWS = 16
VMEM_LIMIT = 56 * 1024 * 1024

ROW_TILE = 512
FFN_TF = 512
FLASH_T = 512
DECODE_PAGES = 16
SWA_SEQ_TILE = 8

F32 = jnp.float32
BF16 = jnp.bfloat16


def _dot(a, b):
    return jnp.dot(a, b, preferred_element_type=F32)


def _dot_nt(a, b):
    return lax.dot_general(a, b, (((1,), (1,)), ((), ())), preferred_element_type=F32)


def _layer_norm(y, g, b):
    mu = jnp.mean(y, axis=-1, keepdims=True)
    d = y - mu
    var = jnp.mean(d * d, axis=-1, keepdims=True)
    return d * lax.rsqrt(var + LN_EPS) * g + b


def _rms_norm(a, g):
    return a * lax.rsqrt(jnp.mean(a * a, axis=-1, keepdims=True) + RMS_EPS) * g


def _params(*sem):
    return pltpu.CompilerParams(dimension_semantics=sem, vmem_limit_bytes=VMEM_LIMIT)


def _row_spec(tm, cols):
    return pl.BlockSpec((tm, cols), lambda i: (i, 0))


def _const_spec(shape):
    nd = len(shape)
    return pl.BlockSpec(shape, lambda i: (0,) * nd)


def _ffn_body(x_ref, halo_ref, wg_ref, wv_ref, wo_ref, cw_ref, cb_ref, lng_ref, lnb_ref,
              out_ref, tail_ref, xb_scr, g_scr, acc_scr, *, tm, halo, shift, halo_is_gate,
              tail_rows):
    j = pl.program_id(1)
    xoff = 0 if halo_is_gate else halo

    @pl.when(j == 0)
    def _():
        if not halo_is_gate:
            xb_scr[0:halo, :] = halo_ref[0].astype(BF16)
        xb_scr[xoff:xoff + tm, :] = x_ref[...].astype(BF16)
        acc_scr[...] = jnp.zeros_like(acc_scr)

    if halo_is_gate:
        g_scr[0:halo, :] = halo_ref[...]
        g_scr[halo:halo + tm, :] = _dot(xb_scr[...], wg_ref[...])
    else:
        g_scr[...] = _dot(xb_scr[...], wg_ref[...])
    val = _dot(xb_scr[xoff:xoff + tm, :], wv_ref[...])
    conv = (cb_ref[...]
            + g_scr[halo - 2 * shift:halo - 2 * shift + tm, :] * cw_ref[0:1, :]
            + g_scr[halo - shift:halo - shift + tm, :] * cw_ref[1:2, :]
            + g_scr[halo:halo + tm, :] * cw_ref[2:3, :])
    h = 0.5 * conv * (1.0 + lax.erf(conv * math.sqrt(0.5))) * val
    acc_scr[...] += _dot(h.astype(BF16), wo_ref[...])
    tail = g_scr[halo + tm - tail_rows:halo + tm, :]
    if halo_is_gate:
        tail_ref[...] = tail
    else:
        tail_ref[0] = tail

    @pl.when(j == pl.num_programs(1) - 1)
    def _():
        y = ALPHA * x_ref[...] + acc_scr[...]
        out_ref[...] = _layer_norm(y, lng_ref[...], lnb_ref[...])


def _ffn_call(x, halo, w_in, w_out, cw, cb, lng, lnb, *, tm, halo_rows, shift, halo_is_gate,
              tail_rows):
    n = x.shape[0]
    nt = n // tm
    nf = D_FF // FFN_TF
    if halo_is_gate:
        halo_spec = pl.BlockSpec((halo_rows, FFN_TF), lambda i, j: (0, j))
        x_rows = tm
        tail_shape = (tail_rows, D_FF)
        tail_spec = pl.BlockSpec((tail_rows, FFN_TF), lambda i, j: (0, j))
    else:
        halo_spec = pl.BlockSpec((1, halo_rows, D_MODEL), lambda i, j: (i, 0, 0))
        x_rows = tm + halo_rows
        tail_shape = (nt, tail_rows, D_FF)
        tail_spec = pl.BlockSpec((1, tail_rows, FFN_TF), lambda i, j: (i, 0, j))
    body = functools.partial(_ffn_body, tm=tm, halo=halo_rows, shift=shift,
                             halo_is_gate=halo_is_gate, tail_rows=tail_rows)
    return pl.pallas_call(
        body,
        grid=(nt, nf),
        in_specs=[
            pl.BlockSpec((tm, D_MODEL), lambda i, j: (i, 0)),
            halo_spec,
            pl.BlockSpec((D_MODEL, FFN_TF), lambda i, j: (0, j)),
            pl.BlockSpec((D_MODEL, FFN_TF), lambda i, j: (0, j + nf)),
            pl.BlockSpec((FFN_TF, D_MODEL), lambda i, j: (j, 0)),
            pl.BlockSpec((CONV_W, FFN_TF), lambda i, j: (0, j)),
            pl.BlockSpec((1, FFN_TF), lambda i, j: (0, j)),
            pl.BlockSpec((1, D_MODEL), lambda i, j: (0, 0)),
            pl.BlockSpec((1, D_MODEL), lambda i, j: (0, 0)),
        ],
        out_specs=[pl.BlockSpec((tm, D_MODEL), lambda i, j: (i, 0)), tail_spec],
        out_shape=[jax.ShapeDtypeStruct((n, D_MODEL), F32),
                   jax.ShapeDtypeStruct(tail_shape, F32)],
        scratch_shapes=[pltpu.VMEM((x_rows, D_MODEL), BF16),
                        pltpu.VMEM((tm + halo_rows, FFN_TF), F32),
                        pltpu.VMEM((tm, D_MODEL), F32)],
        compiler_params=_params("parallel", "arbitrary"),
        name="conv_ffn_ln",
    )(x, halo, w_in, w_in, w_out, cw, cb, lng, lnb)


def _prompt_halo(x, tm, rows, tiles_per_seq):
    nt = x.shape[0] // tm
    last = x.reshape(nt, tm, x.shape[1])[:, tm - rows:, :]
    prev = jnp.concatenate([jnp.zeros_like(last[:1]), last[:-1]], axis=0)
    starts = (jnp.arange(nt) % tiles_per_seq == 0)[:, None, None]
    return jnp.where(starts, 0.0, prev)


def _proj_ln_body(a_ref, x_ref, w_ref, bias_ref, lng_ref, lnb_ref, out_ref):
    h = _dot(a_ref[...], w_ref[...]) + bias_ref[...]
    out_ref[...] = _layer_norm(ALPHA * x_ref[...] + h, lng_ref[...], lnb_ref[...])


def _proj_ln_call(a, x, w, bias, lng, lnb, tm):
    n, k = a.shape
    return pl.pallas_call(
        _proj_ln_body,
        grid=(n // tm,),
        in_specs=[_row_spec(tm, k), _row_spec(tm, D_MODEL), _const_spec(w.shape),
                  _const_spec((1, D_MODEL)), _const_spec((1, D_MODEL)), _const_spec((1, D_MODEL))],
        out_specs=_row_spec(tm, D_MODEL),
        out_shape=jax.ShapeDtypeStruct((n, D_MODEL), F32),
        compiler_params=_params("parallel"),
        name="proj_res_ln",
    )(a, x, w, bias, lng, lnb)


def _pool_prompt_body(x_ref, halo_ref, w_ref, scale_ref, lng_ref, lnb_ref, out_ref, ext_scr,
                      *, tm, tiles_per_seq):
    i = pl.program_id(0)
    hb = POOL_BUF + 1
    ext_scr[0:hb, :] = halo_ref[0]
    ext_scr[hb:hb + tm, :] = x_ref[...]
    pos = (i % tiles_per_seq) * tm + lax.broadcasted_iota(jnp.int32, (tm, 1), 0)
    for g, wnd in enumerate(POOL_WINDOWS):
        c0 = g * POOL_GC
        xg = x_ref[:, c0:c0 + POOL_GC]
        win = xg
        for k in range(1, wnd):
            win = win + ext_scr[hb - k:hb - k + tm, c0:c0 + POOL_GC]
        cnt = jnp.minimum(pos + 1, wnd).astype(F32)
        pooled = win / cnt - xg
        y = _dot(pooled.astype(BF16), w_ref[g]) * scale_ref[:, c0:c0 + POOL_GC]
        ext_scr[hb:hb + tm, c0:c0 + POOL_GC] = ALPHA * xg + y
    out_ref[...] = _layer_norm(ext_scr[hb:hb + tm, :], lng_ref[...], lnb_ref[...])


def _pool_prompt_call(x, w, scale, lng, lnb, tm, tiles_per_seq):
    n = x.shape[0]
    hb = POOL_BUF + 1
    halo = _prompt_halo(x, tm, hb, tiles_per_seq)
    body = functools.partial(_pool_prompt_body, tm=tm, tiles_per_seq=tiles_per_seq)
    return pl.pallas_call(
        body,
        grid=(n // tm,),
        in_specs=[_row_spec(tm, D_MODEL),
                  pl.BlockSpec((1, hb, D_MODEL), lambda i: (i, 0, 0)),
                  _const_spec(w.shape), _const_spec((1, D_MODEL)),
                  _const_spec((1, D_MODEL)), _const_spec((1, D_MODEL))],
        out_specs=_row_spec(tm, D_MODEL),
        out_shape=jax.ShapeDtypeStruct((n, D_MODEL), F32),
        scratch_shapes=[pltpu.VMEM((tm + hb, D_MODEL), F32)],
        compiler_params=_params("arbitrary"),
        name="pool_mix_ln_prompt",
    )(x, halo, w, scale, lng, lnb)


def _pool_sample_body(ext_ref, w_ref, scale_ref, lng_ref, lnb_ref, out_ref, y_scr, *, t_new, bt):
    for g, wnd in enumerate(POOL_WINDOWS):
        c0 = g * POOL_GC
        pooled = []
        for t in range(t_new):
            xg = ext_ref[POOL_BUF + t, :, c0:c0 + POOL_GC]
            win = xg
            for k in range(1, wnd):
                win = win + ext_ref[POOL_BUF + t - k, :, c0:c0 + POOL_GC]
            pooled.append(win / float(wnd) - xg)
        pooled = jnp.concatenate(pooled, axis=0)
        y = _dot(pooled.astype(BF16), w_ref[g]) * scale_ref[:, c0:c0 + POOL_GC]
        for t in range(t_new):
            y_scr[t, :, c0:c0 + POOL_GC] = (ALPHA * ext_ref[POOL_BUF + t, :, c0:c0 + POOL_GC]
                                            + y[t * bt:(t + 1) * bt])
    for t in range(t_new):
        out_ref[t] = _layer_norm(y_scr[t], lng_ref[...], lnb_ref[...])


def _pool_sample_call(ext, w, scale, lng, lnb, bt):
    rows, nb, _ = ext.shape
    t_new = rows - POOL_BUF
    body = functools.partial(_pool_sample_body, t_new=t_new, bt=bt)
    return pl.pallas_call(
        body,
        grid=(nb // bt,),
        in_specs=[pl.BlockSpec((rows, bt, D_MODEL), lambda i: (0, i, 0)),
                  _const_spec(w.shape), _const_spec((1, D_MODEL)),
                  _const_spec((1, D_MODEL)), _const_spec((1, D_MODEL))],
        out_specs=pl.BlockSpec((t_new, bt, D_MODEL), lambda i: (0, i, 0)),
        out_shape=jax.ShapeDtypeStruct((t_new, nb, D_MODEL), F32),
        scratch_shapes=[pltpu.VMEM((t_new, bt, D_MODEL), F32)],
        compiler_params=_params("parallel"),
        name="pool_mix_ln_sample",
    )(ext, w, scale, lng, lnb)


def _rope_pair(pair, cs):
    t = pair * cs
    return t + pltpu.roll(t, MLA_ROPE, 1)


def _mla_a_body(x_ref, cs_ref, w_ref, gq_ref, gkv_ref, cq_ref, ckv_ref, kpe_ref, kpad_ref):
    a = _dot(x_ref[...].astype(BF16), w_ref[...])
    cq_ref[...] = _rms_norm(a[:, :MLA_Q_RANK], gq_ref[...]).astype(BF16)
    ckv_ref[...] = _rms_norm(a[:, MLA_Q_RANK:MLA_Q_RANK + MLA_KV_RANK], gkv_ref[...])
    r = _rope_pair(a[:, MLA_Q_RANK + MLA_KV_RANK:], cs_ref[...])
    kpe_ref[...] = r[:, :MLA_ROPE]
    lane = lax.broadcasted_iota(jnp.int32, r.shape, 1)
    kpad_ref[...] = jnp.where(lane < MLA_ROPE, r, 0.0).astype(BF16)


def _mla_a_call(x, cs, w, gq, gkv, tm, tiles_per_seq):
    n = x.shape[0]
    return pl.pallas_call(
        _mla_a_body,
        grid=(n // tm,),
        in_specs=[_row_spec(tm, D_MODEL),
                  pl.BlockSpec((tm, LANES), lambda i: (i % tiles_per_seq, 0)),
                  _const_spec(w.shape), _const_spec((1, MLA_Q_RANK)),
                  _const_spec((1, MLA_KV_RANK))],
        out_specs=[_row_spec(tm, MLA_Q_RANK), _row_spec(tm, MLA_KV_RANK),
                   _row_spec(tm, MLA_ROPE), _row_spec(tm, LANES)],
        out_shape=[jax.ShapeDtypeStruct((n, MLA_Q_RANK), BF16),
                   jax.ShapeDtypeStruct((n, MLA_KV_RANK), F32),
                   jax.ShapeDtypeStruct((n, MLA_ROPE), F32),
                   jax.ShapeDtypeStruct((n, LANES), BF16)],
        compiler_params=_params("parallel"),
        name="mla_down_proj",
    )(x, cs, w, gq, gkv)


MLA_QW = MLA_NOPE + 2 * MLA_ROPE


def _mla_q_body(cq_ref, cs_ref, w_ref, q_ref):
    cs = cs_ref[...]
    for h in range(MLA_HEADS):
        c0 = h * MLA_QW
        qh = _dot(cq_ref[...], w_ref[:, c0:c0 + MLA_QW])
        q_ref[:, c0:c0 + MLA_NOPE] = qh[:, :MLA_NOPE].astype(BF16)
        q_ref[:, c0 + MLA_NOPE:c0 + MLA_QW] = _rope_pair(qh[:, MLA_NOPE:], cs).astype(BF16)


def _mla_q_call(cq, cs, w, tm, tiles_per_seq):
    n = cq.shape[0]
    return pl.pallas_call(
        _mla_q_body,
        grid=(n // tm,),
        in_specs=[_row_spec(tm, MLA_Q_RANK),
                  pl.BlockSpec((tm, LANES), lambda i: (i % tiles_per_seq, 0)),
                  _const_spec(w.shape)],
        out_specs=_row_spec(tm, MLA_HEADS * MLA_QW),
        out_shape=jax.ShapeDtypeStruct((n, MLA_HEADS * MLA_QW), BF16),
        compiler_params=_params("parallel"),
        name="mla_q_proj",
    )(cq, cs, w)


def _mla_kv_up_body(ckv_ref, kpad_ref, wuk_ref, wuv_ref, k_ref, v_ref):
    ckv = ckv_ref[...].astype(BF16)
    v_ref[...] = _dot(ckv, wuv_ref[...]).astype(BF16)
    for h in range(MLA_HEADS):
        c0 = h * MLA_QW
        k_ref[:, c0:c0 + MLA_NOPE] = _dot(ckv, wuk_ref[:, h * MLA_NOPE:(h + 1) * MLA_NOPE]).astype(BF16)
        k_ref[:, c0 + MLA_NOPE:c0 + MLA_QW] = kpad_ref[...]


def _mla_kv_up_call(ckv, kpad, wuk, wuv, tm):
    n = ckv.shape[0]
    return pl.pallas_call(
        _mla_kv_up_body,
        grid=(n // tm,),
        in_specs=[_row_spec(tm, MLA_KV_RANK), _row_spec(tm, LANES),
                  _const_spec(wuk.shape), _const_spec(wuv.shape)],
        out_specs=[_row_spec(tm, MLA_HEADS * MLA_QW), _row_spec(tm, MLA_HEADS * MLA_V)],
        out_shape=[jax.ShapeDtypeStruct((n, MLA_HEADS * MLA_QW), BF16),
                   jax.ShapeDtypeStruct((n, MLA_HEADS * MLA_V), BF16)],
        compiler_params=_params("parallel"),
        name="mla_kv_up_proj",
    )(ckv, kpad, wuk, wuv)


def _softmax_step(s, v, m_scr, l_scr, acc_scr):
    m_prev = m_scr[...]
    m_new = jnp.maximum(m_prev, jnp.max(s, axis=-1, keepdims=True))
    corr = jnp.exp(m_prev - m_new)
    p = jnp.exp(s - m_new)
    l_scr[...] = l_scr[...] * corr + jnp.sum(p, axis=-1, keepdims=True)
    acc_scr[...] = acc_scr[...] * corr + _dot(p.astype(BF16), v)
    m_scr[...] = m_new


def _softmax_init(m_scr, l_scr, acc_scr):
    m_scr[...] = jnp.full_like(m_scr, NEG_INF)
    l_scr[...] = jnp.zeros_like(l_scr)
    acc_scr[...] = jnp.zeros_like(acc_scr)


def _mla_flash_body(q_ref, k_ref, v_ref, o_ref, m_scr, l_scr, acc_scr, *, t):
    qi = pl.program_id(2)
    _softmax_init(m_scr, l_scr, acc_scr)
    q = q_ref[...]

    def scores(ki):
        k = k_ref[pl.ds(pl.multiple_of(ki * t, t), t), :]
        return _dot_nt(q, k) * MLA_SCALE

    def full_step(ki, carry):
        v = v_ref[pl.ds(pl.multiple_of(ki * t, t), t), :]
        _softmax_step(scores(ki), v, m_scr, l_scr, acc_scr)
        return carry

    lax.fori_loop(0, qi, full_step, 0)
    row = lax.broadcasted_iota(jnp.int32, (t, t), 0)
    col = lax.broadcasted_iota(jnp.int32, (t, t), 1)
    s = jnp.where(col <= row, scores(qi), NEG_INF)
    _softmax_step(s, v_ref[pl.ds(pl.multiple_of(qi * t, t), t), :], m_scr, l_scr, acc_scr)
    o_ref[...] = (acc_scr[...] / l_scr[...]).astype(BF16)


def _mla_flash_call(q, k, v, batch, seq):
    t = FLASH_T
    nq = seq // t
    body = functools.partial(_mla_flash_body, t=t)
    return pl.pallas_call(
        body,
        grid=(batch, MLA_HEADS, nq),
        in_specs=[pl.BlockSpec((t, MLA_QW), lambda b, h, i: (b * nq + i, h)),
                  pl.BlockSpec((seq, MLA_QW), lambda b, h, i: (b, h)),
                  pl.BlockSpec((seq, MLA_V), lambda b, h, i: (b, h))],
        out_specs=pl.BlockSpec((t, MLA_V), lambda b, h, i: (b * nq + i, h)),
        out_shape=jax.ShapeDtypeStruct((batch * seq, MLA_HEADS * MLA_V), BF16),
        scratch_shapes=[pltpu.VMEM((t, 1), F32), pltpu.VMEM((t, 1), F32),
                        pltpu.VMEM((t, MLA_V), F32)],
        compiler_params=_params("parallel", "parallel", "arbitrary"),
        name="mla_prompt_attention",
    )(q, k, v)


MLA_LATQ = MLA_KV_RANK + 2 * MLA_ROPE


def _mla_absorb_q_body(q_ref, wuk_ref, out_ref):
    q = q_ref[...]
    out_ref[:, :MLA_KV_RANK] = _dot_nt(q[:, :MLA_NOPE], wuk_ref[...]).astype(BF16)
    out_ref[:, MLA_KV_RANK:] = q[:, MLA_NOPE:]


def _mla_absorb_q_call(q, wuk):
    n = q.shape[0]
    return pl.pallas_call(
        _mla_absorb_q_body,
        grid=(MLA_HEADS,),
        in_specs=[pl.BlockSpec((n, MLA_QW), lambda h: (0, h)),
                  pl.BlockSpec((MLA_KV_RANK, MLA_NOPE), lambda h: (0, h))],
        out_specs=pl.BlockSpec((n, MLA_LATQ), lambda h: (0, h)),
        out_shape=jax.ShapeDtypeStruct((n, MLA_HEADS * MLA_LATQ), BF16),
        compiler_params=_params("parallel"),
        name="mla_absorb_q",
    )(q, wuk)


def _mla_decode_body(pt_ref, q_ref, nck_ref, nkp_ref, *rest, pages, t_new):
    ck_refs = rest[:pages]
    kp_refs = rest[pages:2 * pages]
    o_ref = rest[2 * pages]
    ck_scr, kp_scr, m_scr, l_scr, acc_scr = rest[2 * pages + 1:]
    step = pl.program_id(1)

    @pl.when(step == 0)
    def _():
        _softmax_init(m_scr, l_scr, acc_scr)

    for g in range(pages):
        ck_scr[g * PAGE_SIZE:(g + 1) * PAGE_SIZE, :] = ck_refs[g][0, 0].astype(BF16)
        kp_scr[g * PAGE_SIZE:(g + 1) * PAGE_SIZE, :] = kp_refs[g][0, 0].astype(BF16)
    q = q_ref[0]
    q_lat = q[:, :MLA_KV_RANK]
    q_pe = q[:, MLA_KV_RANK:MLA_KV_RANK + MLA_ROPE]
    s = (_dot_nt(q_lat, ck_scr[...]) + _dot_nt(q_pe, kp_scr[...])) * MLA_SCALE
    _softmax_step(s, ck_scr[...], m_scr, l_scr, acc_scr)

    @pl.when(step == pl.num_programs(1) - 1)
    def _():
        pad = PAGE_SIZE - nck_ref.shape[1]
        nck = jnp.concatenate([nck_ref[0], jnp.zeros((pad, MLA_KV_RANK), F32)],
                              axis=0).astype(BF16)
        nkp = jnp.concatenate([nkp_ref[0], jnp.zeros((pad, MLA_ROPE), F32)],
                              axis=0).astype(BF16)
        s2 = (_dot_nt(q_lat, nck) + _dot_nt(q_pe, nkp)) * MLA_SCALE
        tq = lax.broadcasted_iota(jnp.int32, s2.shape, 0) // MLA_HEADS
        tk = lax.broadcasted_iota(jnp.int32, s2.shape, 1)
        s2 = jnp.where((tk <= tq) & (tk < t_new), s2, NEG_INF)
        _softmax_step(s2, nck, m_scr, l_scr, acc_scr)
        o_ref[0] = (acc_scr[...] / l_scr[...]).astype(BF16)


def _mla_decode_call(page_table, q, new_ckv, new_kpe, cache_ckv, cache_kpe, layer, t_new):
    nb, rows, _ = q.shape
    n_pages = page_table.shape[1]
    pages = DECODE_PAGES
    pad_rows = new_ckv.shape[1]

    def page_spec(width, g):
        return pl.BlockSpec((1, 1, PAGE_SIZE, width),
                            lambda b, s, pt: (layer, pt[b, s * pages + g], 0, 0))

    body = functools.partial(_mla_decode_body, pages=pages, t_new=t_new)
    grid_spec = pltpu.PrefetchScalarGridSpec(
        num_scalar_prefetch=1,
        grid=(nb, n_pages // pages),
        in_specs=([pl.BlockSpec((1, rows, MLA_LATQ), lambda b, s, pt: (b, 0, 0)),
                   pl.BlockSpec((1, pad_rows, MLA_KV_RANK), lambda b, s, pt: (b, 0, 0)),
                   pl.BlockSpec((1, pad_rows, MLA_ROPE), lambda b, s, pt: (b, 0, 0))]
                  + [page_spec(MLA_KV_RANK, g) for g in range(pages)]
                  + [page_spec(MLA_ROPE, g) for g in range(pages)]),
        out_specs=pl.BlockSpec((1, rows, MLA_KV_RANK), lambda b, s, pt: (b, 0, 0)),
        scratch_shapes=[pltpu.VMEM((pages * PAGE_SIZE, MLA_KV_RANK), BF16),
                        pltpu.VMEM((pages * PAGE_SIZE, MLA_ROPE), BF16),
                        pltpu.VMEM((rows, 1), F32), pltpu.VMEM((rows, 1), F32),
                        pltpu.VMEM((rows, MLA_KV_RANK), F32)],
    )
    return pl.pallas_call(
        body,
        grid_spec=grid_spec,
        out_shape=jax.ShapeDtypeStruct((nb, rows, MLA_KV_RANK), BF16),
        compiler_params=_params("parallel", "arbitrary"),
        name="mla_paged_decode",
    )(page_table, q, new_ckv, new_kpe, *([cache_ckv] * pages), *([cache_kpe] * pages))


def _mla_absorb_o_body(o_ref, wuv_ref, out_ref):
    out_ref[...] = _dot(o_ref[0], wuv_ref[...]).astype(BF16)


def _mla_absorb_o_call(o_lat, wuv):
    _, n, _ = o_lat.shape
    return pl.pallas_call(
        _mla_absorb_o_body,
        grid=(MLA_HEADS,),
        in_specs=[pl.BlockSpec((1, n, MLA_KV_RANK), lambda h: (h, 0, 0)),
                  pl.BlockSpec((MLA_KV_RANK, MLA_V), lambda h: (0, h))],
        out_specs=pl.BlockSpec((n, MLA_V), lambda h: (0, h)),
        out_shape=jax.ShapeDtypeStruct((n, MLA_HEADS * MLA_V), BF16),
        compiler_params=_params("parallel"),
        name="mla_absorb_o",
    )(o_lat, wuv)


SWA_NQ = SWA_HEADS * SWA_HEAD_DIM
SWA_NKV = SWA_KV_HEADS * SWA_HEAD_DIM
SWA_PAIRS = SWA_KV_HEADS // 2


def _swa_q_perm():
    cols = []
    for p in range(SWA_PAIRS):
        for g in range(SWA_GROUP):
            for half in range(2):
                head = (2 * p + half) * SWA_GROUP + g
                cols.extend(range(head * SWA_HEAD_DIM, (head + 1) * SWA_HEAD_DIM))
    return np.asarray(cols, dtype=np.int32)


def _swa_slope(head):
    return 2.0 ** (-8.0 * (head + 1) / SWA_HEADS)


def _swa_qkv_body(x_ref, w_ref, b_ref, q_ref, k_ref, v_ref):
    qkv = _dot(x_ref[...].astype(BF16), w_ref[...]) + b_ref[...]
    q_ref[...] = qkv[:, :SWA_NQ].astype(BF16)
    k_ref[...] = qkv[:, SWA_NQ:SWA_NQ + SWA_NKV]
    v_ref[...] = qkv[:, SWA_NQ + SWA_NKV:]


def _swa_qkv_call(x, w, b, tm):
    n = x.shape[0]
    return pl.pallas_call(
        _swa_qkv_body,
        grid=(n // tm,),
        in_specs=[_row_spec(tm, D_MODEL), _const_spec(w.shape), _const_spec(b.shape)],
        out_specs=[_row_spec(tm, SWA_NQ), _row_spec(tm, SWA_NKV), _row_spec(tm, SWA_NKV)],
        out_shape=[jax.ShapeDtypeStruct((n, SWA_NQ), BF16),
                   jax.ShapeDtypeStruct((n, SWA_NKV), F32),
                   jax.ShapeDtypeStruct((n, SWA_NKV), F32)],
        compiler_params=_params("parallel"),
        name="swa_qkv_proj",
    )(x, w, b)


def _swa_heads(q_blk, k_pair, v_pair, sink_ref, p, g, dist, valid):
    lane_q = lax.broadcasted_iota(jnp.int32, q_blk.shape, 1)
    outs = []
    for half in range(2):
        head = (2 * p + half) * SWA_GROUP + g
        in_half = (lane_q >= SWA_HEAD_DIM) if half else (lane_q < SWA_HEAD_DIM)
        qm = jnp.where(in_half, q_blk, jnp.zeros_like(q_blk))
        s = _dot_nt(qm, k_pair) * SWA_SCALE - _swa_slope(head) * dist
        s = jnp.where(valid, s, NEG_INF)
        sink = sink_ref[head]
        m = jnp.maximum(jnp.max(s, axis=-1, keepdims=True), sink)
        e = jnp.exp(s - m)
        denom = jnp.sum(e, axis=-1, keepdims=True) + jnp.exp(sink - m)
        outs.append(_dot((e / denom).astype(BF16), v_pair))
    lane_o = lax.broadcasted_iota(jnp.int32, outs[0].shape, 1)
    return jnp.where(lane_o < SWA_HEAD_DIM, outs[0], outs[1])


def _swa_prompt_body(sink_ref, q_ref, kc_ref, kp_ref, vc_ref, vp_ref, o_ref):
    i = pl.program_id(1)
    k = jnp.concatenate([kp_ref[...].astype(BF16), kc_ref[...].astype(BF16)], axis=0)
    v = jnp.concatenate([vp_ref[...].astype(BF16), vc_ref[...].astype(BF16)], axis=0)
    row = lax.broadcasted_iota(jnp.int32, (WINDOW, 2 * WINDOW), 0)
    col = lax.broadcasted_iota(jnp.int32, (WINDOW, 2 * WINDOW), 1)
    dist = row + WINDOW - col
    valid = (dist >= 0) & (dist < WINDOW) & ((col >= WINDOW) | (i > 0))
    distf = dist.astype(F32)
    for p in range(SWA_PAIRS):
        kpair = k[:, p * LANES:(p + 1) * LANES]
        vpair = v[:, p * LANES:(p + 1) * LANES]
        for g in range(SWA_GROUP):
            c0 = (p * SWA_GROUP + g) * LANES
            o = _swa_heads(q_ref[:, c0:c0 + LANES], kpair, vpair, sink_ref, p, g, distf, valid)
            o_ref[:, c0:c0 + LANES] = o.astype(BF16)


def _swa_prompt_call(sinks, q, k, v, batch, seq):
    nb = seq // WINDOW
    cur = lambda b, i: (b * nb + i, 0)
    prev = lambda b, i: (b * nb + jnp.maximum(i - 1, 0), 0)
    return pl.pallas_call(
        _swa_prompt_body,
        grid=(batch, nb),
        in_specs=[pl.BlockSpec(memory_space=pltpu.SMEM),
                  pl.BlockSpec((WINDOW, SWA_NQ), cur),
                  pl.BlockSpec((WINDOW, SWA_NKV), cur), pl.BlockSpec((WINDOW, SWA_NKV), prev),
                  pl.BlockSpec((WINDOW, SWA_NKV), cur), pl.BlockSpec((WINDOW, SWA_NKV), prev)],
        out_specs=pl.BlockSpec((WINDOW, SWA_NQ), cur),
        out_shape=jax.ShapeDtypeStruct((batch * seq, SWA_NQ), BF16),
        compiler_params=_params("parallel", "arbitrary"),
        name="swa_prompt_attention",
    )(sinks, q, k, k, v, v)


def _swa_sample_body(sink_ref, q_ref, kc_ref, vc_ref, kn_ref, vn_ref, o_ref, *, bt, t_pad):
    n_keys = 2 * WINDOW
    fill = jnp.zeros((n_keys - WINDOW - t_pad, SWA_NKV), F32)
    rows = 2 * SWA_GROUP * t_pad
    ridx = lax.broadcasted_iota(jnp.int32, (rows, 1), 0) // t_pad
    tq = lax.broadcasted_iota(jnp.int32, (rows, n_keys), 0) % t_pad
    col = lax.broadcasted_iota(jnp.int32, (rows, n_keys), 1)
    dist = tq + WINDOW - col
    valid = (dist >= 0) & (dist < WINDOW)
    distf = dist.astype(F32)
    lane = lax.broadcasted_iota(jnp.int32, (t_pad, LANES), 1)
    bias, sink_cols = [], []
    for p in range(SWA_PAIRS):
        slope = jnp.zeros((rows, 1), F32)
        sink = jnp.zeros((rows, 1), F32)
        for half in range(2):
            for g in range(SWA_GROUP):
                head = (2 * p + half) * SWA_GROUP + g
                here = ridx == half * SWA_GROUP + g
                slope = jnp.where(here, _swa_slope(head), slope)
                sink = jnp.where(here, sink_ref[head], sink)
        bias.append(slope * distf)
        sink_cols.append(sink)

    def one_seq(bb, carry):
        k = jnp.concatenate([kc_ref[bb], kn_ref[bb], fill], axis=0).astype(BF16)
        v = jnp.concatenate([vc_ref[bb], vn_ref[bb], fill], axis=0).astype(BF16)
        for p in range(SWA_PAIRS):
            blocks = [q_ref[bb, :, (p * SWA_GROUP + g) * LANES:(p * SWA_GROUP + g + 1) * LANES]
                      .astype(F32) for g in range(SWA_GROUP)]
            stack = ([jnp.where(lane < SWA_HEAD_DIM, blk, 0.0) for blk in blocks]
                     + [jnp.where(lane >= SWA_HEAD_DIM, blk, 0.0) for blk in blocks])
            qs = jnp.concatenate(stack, axis=0).astype(BF16)
            s = _dot_nt(qs, k[:, p * LANES:(p + 1) * LANES]) * SWA_SCALE - bias[p]
            s = jnp.where(valid, s, NEG_INF)
            m = jnp.maximum(jnp.max(s, axis=-1, keepdims=True), sink_cols[p])
            e = jnp.exp(s - m)
            denom = jnp.sum(e, axis=-1, keepdims=True) + jnp.exp(sink_cols[p] - m)
            o = _dot((e / denom).astype(BF16), v[:, p * LANES:(p + 1) * LANES])
            for g in range(SWA_GROUP):
                lo = o[g * t_pad:(g + 1) * t_pad]
                hi = o[(SWA_GROUP + g) * t_pad:(SWA_GROUP + g + 1) * t_pad]
                c0 = (p * SWA_GROUP + g) * LANES
                o_ref[bb, :, c0:c0 + LANES] = jnp.where(lane < SWA_HEAD_DIM, lo, hi).astype(BF16)
        return carry

    lax.fori_loop(0, bt, one_seq, 0)


def _swa_sample_call(sinks, q, kc, vc, kn, vn):
    nb, t_pad, _ = q.shape
    bt = SWA_SEQ_TILE
    body = functools.partial(_swa_sample_body, bt=bt, t_pad=t_pad)
    seq3 = lambda rows, cols: pl.BlockSpec((bt, rows, cols), lambda i: (i, 0, 0))
    return pl.pallas_call(
        body,
        grid=(nb // bt,),
        in_specs=[pl.BlockSpec(memory_space=pltpu.SMEM),
                  seq3(t_pad, SWA_NQ), seq3(WINDOW, SWA_NKV), seq3(WINDOW, SWA_NKV),
                  seq3(t_pad, SWA_NKV), seq3(t_pad, SWA_NKV)],
        out_specs=seq3(t_pad, SWA_NQ),
        out_shape=jax.ShapeDtypeStruct((nb, t_pad, SWA_NQ), BF16),
        compiler_params=_params("parallel"),
        name="swa_sample_attention",
    )(sinks, q, kc, vc, kn, vn)


def _rot_cols(w):
    half = w.shape[-1] // 2
    return jnp.concatenate([-w[..., half:], w[..., :half]], axis=-1)


def _rope_table(pos):
    half = MLA_ROPE // 2
    inv = ROPE_THETA ** (-jnp.arange(half, dtype=F32) / half)
    ang = pos.astype(F32)[:, None] * inv[None, :]
    cos, sin = jnp.cos(ang), jnp.sin(ang)
    return jnp.concatenate([cos, cos, sin, sin], axis=-1)


def _time_major(a, nb, t):
    return jnp.swapaxes(a, 0, 1).reshape((t * nb,) + a.shape[2:])


def _batch_major(a, nb, t):
    return jnp.swapaxes(a.reshape((t, nb) + a.shape[1:]), 0, 1)


def kernel(x_prompt, x_sample, state_pool, cache_mla_ckv, cache_mla_kpe, page_table, cache_swa_k, cache_swa_v, state_ffn_conv, ln_g, ln_b, pool_w, pool_scale, mla_w_a, mla_g_q, mla_g_kv, mla_w_uq, mla_w_uk, mla_w_uv, mla_w_o, swa_w_qkv, swa_b_qkv, swa_sinks, swa_w_o, swa_b_o, ffn_w_in, ffn_conv_w, ffn_conv_b, ffn_w_out):
    B, S, D = x_prompt.shape
    NB, T, _ = x_sample.shape
    NS = NB * T
    tm = ROW_TILE
    tps = S // tm
    assert S % tm == 0 and NS == tm and S % FLASH_T == 0 and T <= SUBLANES

    xp = x_prompt.reshape(B * S, D)
    xs = _time_major(x_sample, NB, T)
    cs_p = _rope_table(jnp.arange(S))
    cs_s = jnp.repeat(_rope_table(PAST_LEN + jnp.arange(T)), NB, axis=0)
    zero_bias = jnp.zeros((1, D), F32)
    q_perm = _swa_q_perm()

    pool_p, pool_s, ckv_p, ckv_s, kpe_p, kpe_s = [], [], [], [], [], []
    swk_p, swk_s, swv_p, swv_s, conv_p, conv_s = [], [], [], [], [], []

    for i in range(DEPTH):
        kind, j = i % N_MIXERS, i // N_MIXERS
        g1, b1 = ln_g[i, 0][None, :], ln_b[i, 0][None, :]
        g2, b2 = ln_g[i, 1][None, :], ln_b[i, 1][None, :]
        if kind == 0:
            w = pool_w[j].astype(BF16)
            sc = pool_scale[j][None, :]
            pool_p.append(xp.reshape(B, S, D)[:, S - POOL_BUF:])
            xs_bm = _batch_major(xs, NB, T)
            ext = jnp.concatenate([state_pool[j], xs_bm], axis=1)
            pool_s.append(ext[:, -POOL_BUF:])
            xp = _pool_prompt_call(xp, w, sc, g1, b1, tm, tps)
            xs = _pool_sample_call(jnp.swapaxes(ext, 0, 1), w, sc, g1, b1, 32).reshape(NS, D)
        elif kind == 1:
            w_a = mla_w_a[j]
            n_lat = MLA_Q_RANK + MLA_KV_RANK
            w_a_ext = jnp.concatenate([w_a, _rot_cols(w_a[:, n_lat:])], axis=1).astype(BF16)
            w_uq = mla_w_uq[j]
            w_uq_ext = jnp.concatenate([w_uq, _rot_cols(w_uq[..., MLA_NOPE:])], axis=-1)
            w_uq_ext = w_uq_ext.reshape(MLA_Q_RANK, MLA_HEADS * MLA_QW).astype(BF16)
            w_uk = mla_w_uk[j].reshape(MLA_KV_RANK, MLA_HEADS * MLA_NOPE).astype(BF16)
            w_uv = mla_w_uv[j].reshape(MLA_KV_RANK, MLA_HEADS * MLA_V).astype(BF16)
            w_o = mla_w_o[j].reshape(MLA_HEADS * MLA_V, D).astype(BF16)
            gq, gkv = mla_g_q[j][None, :], mla_g_kv[j][None, :]

            cq, ckv, kpe, kpad = _mla_a_call(xp, cs_p, w_a_ext, gq, gkv, tm, tps)
            q = _mla_q_call(cq, cs_p, w_uq_ext, tm, tps)
            k_full, v_full = _mla_kv_up_call(ckv, kpad, w_uk, w_uv, tm)
            o = _mla_flash_call(q, k_full, v_full, B, S)
            ckv_p.append(ckv.reshape(B, S, MLA_KV_RANK))
            kpe_p.append(kpe.reshape(B, S, MLA_ROPE))
            xp = _proj_ln_call(o, xp, w_o, zero_bias, g1, b1, tm)

            cq, ckv, kpe, _ = _mla_a_call(xs, cs_s, w_a_ext, gq, gkv, tm, 1)
            q = _mla_q_call(cq, cs_s, w_uq_ext, tm, 1)
            q_abs = _mla_absorb_q_call(q, w_uk)
            q_abs = _batch_major(q_abs.reshape(NS, MLA_HEADS, MLA_LATQ), NB, T)
            q_abs = q_abs.reshape(NB, T * MLA_HEADS, MLA_LATQ)
            ckv_bm = _batch_major(ckv, NB, T)
            kpe_bm = _batch_major(kpe, NB, T)
            pad = ((0, 0), (0, SUBLANES - T), (0, 0))
            o_lat = _mla_decode_call(page_table, q_abs, jnp.pad(ckv_bm, pad), jnp.pad(kpe_bm, pad),
                                     cache_mla_ckv, cache_mla_kpe, j, T)
            o_lat = o_lat.reshape(NB, T, MLA_HEADS, MLA_KV_RANK).transpose(2, 1, 0, 3)
            o = _mla_absorb_o_call(o_lat.reshape(MLA_HEADS, NS, MLA_KV_RANK), w_uv)
            ckv_s.append(ckv_bm)
            kpe_s.append(kpe_bm)
            xs = _proj_ln_call(o, xs, w_o, zero_bias, g1, b1, tm)
        else:
            w_qkv, b_qkv = swa_w_qkv[j], swa_b_qkv[j]
            w_qkv = jnp.concatenate([w_qkv[:, :SWA_NQ][:, q_perm], w_qkv[:, SWA_NQ:]], axis=1)
            b_qkv = jnp.concatenate([b_qkv[:SWA_NQ][q_perm], b_qkv[SWA_NQ:]])[None, :]
            w_qkv = w_qkv.astype(BF16)
            w_o = swa_w_o[j][q_perm, :].astype(BF16)
            b_o = swa_b_o[j][None, :]
            sinks = swa_sinks[j]

            q, k, v = _swa_qkv_call(xp, w_qkv, b_qkv, tm // 2)
            o = _swa_prompt_call(sinks, q, k, v, B, S)
            kv_shape = (B, WINDOW, SWA_KV_HEADS, SWA_HEAD_DIM)
            swk_p.append(k.reshape(B, S, SWA_NKV)[:, S - WINDOW:].reshape(kv_shape))
            swv_p.append(v.reshape(B, S, SWA_NKV)[:, S - WINDOW:].reshape(kv_shape))
            xp = _proj_ln_call(o, xp, w_o, b_o, g1, b1, tm)

            q, k, v = _swa_qkv_call(xs, w_qkv, b_qkv, tm // 2)
            pad = ((0, 0), (0, SUBLANES - T), (0, 0))
            q_bm = jnp.pad(_batch_major(q, NB, T), pad)
            k_bm, v_bm = _batch_major(k, NB, T), _batch_major(v, NB, T)
            kc = cache_swa_k[j].reshape(NB, WINDOW, SWA_NKV)
            vc = cache_swa_v[j].reshape(NB, WINDOW, SWA_NKV)
            o = _swa_sample_call(sinks, q_bm, kc, vc, jnp.pad(k_bm, pad), jnp.pad(v_bm, pad))
            o = _time_major(o[:, :T], NB, T)
            kv_shape = (NB, WINDOW, SWA_KV_HEADS, SWA_HEAD_DIM)
            swk_s.append(jnp.concatenate([kc, k_bm], axis=1)[:, T:].reshape(kv_shape))
            swv_s.append(jnp.concatenate([vc, v_bm], axis=1)[:, T:].reshape(kv_shape))
            xs = _proj_ln_call(o, xs, w_o, b_o, g1, b1, tm)

        w_in = ffn_w_in[i].astype(BF16)
        w_out = ffn_w_out[i].astype(BF16)
        cw, cb = ffn_conv_w[i], ffn_conv_b[i][None, :]
        halo = _prompt_halo(xp, tm, BF16_ROWS, tps)
        xp, tails = _ffn_call(xp, halo, w_in, w_out, cw, cb, g2, b2, tm=tm, halo_rows=BF16_ROWS,
                              shift=1, halo_is_gate=False, tail_rows=SUBLANES)
        conv_p.append(tails.reshape(B, tps, SUBLANES, D_FF)[:, -1, SUBLANES - (CONV_W - 1):])
        prev = _time_major(state_ffn_conv[i], NB, CONV_W - 1)
        xs, tails = _ffn_call(xs, prev, w_in, w_out, cw, cb, g2, b2, tm=tm,
                              halo_rows=(CONV_W - 1) * NB, shift=NB, halo_is_gate=True,
                              tail_rows=(CONV_W - 1) * NB)
        conv_s.append(_batch_major(tails, NB, CONV_W - 1))

    y_p = xp.reshape(B, S, D)
    y_s = _batch_major(xs, NB, T)
    return (y_p, y_s, jnp.stack(pool_p), jnp.stack(pool_s), jnp.stack(ckv_p), jnp.stack(ckv_s),
            jnp.stack(kpe_p), jnp.stack(kpe_s), jnp.stack(swk_p), jnp.stack(swk_s),
            jnp.stack(swv_p), jnp.stack(swv_s), jnp.stack(conv_p), jnp.stack(conv_s))
```

```python
import functools
import math

import jax
import jax.numpy as jnp
import numpy as np
from jax import lax
from jax.experimental import pallas as pl
from jax.experimental.pallas import tpu as pltpu

D_MODEL = 2048
DEPTH = 4
N_MIXERS = 3
PAST_LEN = 8192
PAGE_SIZE = 128

POOL_WINDOWS = (2, 4, 8, 16)
POOL_GROUPS = len(POOL_WINDOWS)
POOL_GC = D_MODEL // POOL_GROUPS
POOL_BUF = max(POOL_WINDOWS) - 1

MLA_HEADS = 16
MLA_Q_RANK = 512
MLA_KV_RANK = 512
MLA_NOPE = 128
MLA_ROPE = 64
MLA_V = 128
MLA_SCALE = 1.0 / math.sqrt(MLA_NOPE + MLA_ROPE)
ROPE_THETA = 10000.0

SWA_HEADS = 32
SWA_KV_HEADS = 8
SWA_GROUP = SWA_HEADS // SWA_KV_HEADS
SWA_HEAD_DIM = 64
SWA_SCALE = 1.0 / math.sqrt(SWA_HEAD_DIM)
WINDOW = 128

D_FF = 5632
CONV_W = 3

ALPHA = (2.0 * DEPTH) ** 0.25
LN_EPS = 1e-5
RMS_EPS = 1e-6
NEG_INF = -1e30

LANES = 128
SUBLANES = 8
BF16_ROWS = 16
VMEM_LIMIT = 56 * 1024 * 1024

ROW_TILE = 512
FFN_TF = 512
FLASH_T = 512
FLASH_HEADS = 2
DECODE_PAGES = 32
DECODE_CHAINS = 2
SWA_SEQ_TILE = 8

F32 = jnp.float32
BF16 = jnp.bfloat16


def _dot(a, b):
    return jnp.dot(a, b, preferred_element_type=F32)


def _dot_nt(a, b):
    return lax.dot_general(a, b, (((1,), (1,)), ((), ())), preferred_element_type=F32)


def _layer_norm(y, g, b):
    mu = jnp.mean(y, axis=-1, keepdims=True)
    d = y - mu
    var = jnp.mean(d * d, axis=-1, keepdims=True)
    return d * lax.rsqrt(var + LN_EPS) * g + b


def _rms_norm(a, g):
    return a * lax.rsqrt(jnp.mean(a * a, axis=-1, keepdims=True) + RMS_EPS) * g


def _params(*sem):
    return pltpu.CompilerParams(dimension_semantics=sem, vmem_limit_bytes=VMEM_LIMIT)


def _row_spec(tm, cols):
    return pl.BlockSpec((tm, cols), lambda i: (i, 0))


def _const_spec(shape):
    nd = len(shape)
    return pl.BlockSpec(shape, lambda i: (0,) * nd)


def _ffn_body(x_ref, halo_ref, wg_ref, wv_ref, wo_ref, cw_ref, cb_ref, lng_ref, lnb_ref,
              out_ref, tail_ref, xb_scr, g_scr, acc_scr, *, tm, halo, shift, halo_is_gate,
              tail_rows, tiles_per_seq):
    i = pl.program_id(0)
    j = pl.program_id(1)
    xoff = 0 if halo_is_gate else halo

    @pl.when(j == 0)
    def _():
        if not halo_is_gate:
            keep = (i % tiles_per_seq != 0).astype(F32)
            xb_scr[0:halo, :] = (halo_ref[...] * keep).astype(BF16)
        xb_scr[xoff:xoff + tm, :] = x_ref[...].astype(BF16)
        acc_scr[...] = jnp.zeros_like(acc_scr)

    if halo_is_gate:
        g_scr[0:halo, :] = halo_ref[...]
        g_scr[halo:halo + tm, :] = _dot(xb_scr[...], wg_ref[...])
    else:
        g_scr[...] = _dot(xb_scr[...], wg_ref[...])
    val = _dot(xb_scr[xoff:xoff + tm, :], wv_ref[...])
    conv = (cb_ref[...]
            + g_scr[halo - 2 * shift:halo - 2 * shift + tm, :] * cw_ref[0:1, :]
            + g_scr[halo - shift:halo - shift + tm, :] * cw_ref[1:2, :]
            + g_scr[halo:halo + tm, :] * cw_ref[2:3, :])
    h = 0.5 * conv * (1.0 + lax.erf(conv * math.sqrt(0.5))) * val
    acc_scr[...] += _dot(h.astype(BF16), wo_ref[...])
    tail_ref[...] = g_scr[halo + tm - tail_rows:halo + tm, :]

    @pl.when(j == pl.num_programs(1) - 1)
    def _():
        y = ALPHA * x_ref[...] + acc_scr[...]
        out_ref[...] = _layer_norm(y, lng_ref[...], lnb_ref[...])


def _prev_rows_spec(tm, rows, cols, grid_rank):
    per_tile = tm // rows
    if grid_rank == 1:
        return pl.BlockSpec((rows, cols), lambda i: (jnp.maximum(i * per_tile - 1, 0), 0))
    return pl.BlockSpec((rows, cols), lambda i, j: (jnp.maximum(i * per_tile - 1, 0), 0))


def _ffn_call(x, gate_prev, ffn, ln, layer, *, tm, shift, tiles_per_seq):
    w_in, w_out, cw, cb = ffn
    lng, lnb = ln
    n = x.shape[0]
    nt = n // tm
    nf = D_FF // FFN_TF
    halo_is_gate = gate_prev is not None
    if halo_is_gate:
        halo_rows = tail_rows = gate_prev.shape[0]
        halo, halo_spec = gate_prev, pl.BlockSpec((halo_rows, FFN_TF), lambda i, j: (0, j))
        x_rows = tm
        tail_shape = (tail_rows, D_FF)
        tail_spec = pl.BlockSpec((tail_rows, FFN_TF), lambda i, j: (0, j))
    else:
        halo_rows, tail_rows = BF16_ROWS, SUBLANES
        halo, halo_spec = x, _prev_rows_spec(tm, halo_rows, D_MODEL, 2)
        x_rows = tm + halo_rows
        tail_shape = (nt, tail_rows, D_FF)
        tail_spec = pl.BlockSpec((None, tail_rows, FFN_TF), lambda i, j: (i, 0, j))
    body = functools.partial(_ffn_body, tm=tm, halo=halo_rows, shift=shift,
                             halo_is_gate=halo_is_gate, tail_rows=tail_rows,
                             tiles_per_seq=tiles_per_seq)
    ln_row = 2 * layer + 1
    return pl.pallas_call(
        body,
        grid=(nt, nf),
        in_specs=[
            pl.BlockSpec((tm, D_MODEL), lambda i, j: (i, 0)),
            halo_spec,
            pl.BlockSpec((None, D_MODEL, FFN_TF), lambda i, j: (layer, 0, j)),
            pl.BlockSpec((None, D_MODEL, FFN_TF), lambda i, j: (layer, 0, j + nf)),
            pl.BlockSpec((None, FFN_TF, D_MODEL), lambda i, j: (layer, j, 0)),
            pl.BlockSpec((None, CONV_W, FFN_TF), lambda i, j: (layer, 0, j)),
            pl.BlockSpec((None, 1, FFN_TF), lambda i, j: (layer, 0, j)),
            pl.BlockSpec((None, 1, D_MODEL), lambda i, j: (ln_row, 0, 0)),
            pl.BlockSpec((None, 1, D_MODEL), lambda i, j: (ln_row, 0, 0)),
        ],
        out_specs=[pl.BlockSpec((tm, D_MODEL), lambda i, j: (i, 0)), tail_spec],
        out_shape=[jax.ShapeDtypeStruct((n, D_MODEL), F32),
                   jax.ShapeDtypeStruct(tail_shape, F32)],
        scratch_shapes=[pltpu.VMEM((x_rows, D_MODEL), BF16),
                        pltpu.VMEM((tm + halo_rows, FFN_TF), F32),
                        pltpu.VMEM((tm, D_MODEL), F32)],
        compiler_params=_params("parallel", "arbitrary"),
        name="conv_ffn_ln",
    )(x, halo, w_in, w_in, w_out, cw, cb, lng, lnb)


def _proj_ln_body(a_ref, x_ref, w_ref, bias_ref, lng_ref, lnb_ref, out_ref):
    h = _dot(a_ref[...], w_ref[...]) + bias_ref[...]
    out_ref[...] = _layer_norm(ALPHA * x_ref[...] + h, lng_ref[...], lnb_ref[...])


def _proj_ln_call(a, x, w, bias, lng, lnb, tm):
    n, k = a.shape
    return pl.pallas_call(
        _proj_ln_body,
        grid=(n // tm,),
        in_specs=[_row_spec(tm, k), _row_spec(tm, D_MODEL), _const_spec(w.shape),
                  _const_spec((1, D_MODEL)), _const_spec((1, D_MODEL)), _const_spec((1, D_MODEL))],
        out_specs=_row_spec(tm, D_MODEL),
        out_shape=jax.ShapeDtypeStruct((n, D_MODEL), F32),
        compiler_params=_params("parallel"),
        name="proj_res_ln",
    )(a, x, w, bias, lng, lnb)


def _pool_prompt_body(x_ref, halo_ref, w_ref, scale_ref, lng_ref, lnb_ref, out_ref, ext_scr,
                      *, tm, tiles_per_seq):
    i = pl.program_id(0)
    hb = POOL_BUF + 1
    ext_scr[0:hb, :] = halo_ref[...] * (i % tiles_per_seq != 0).astype(F32)
    ext_scr[hb:hb + tm, :] = x_ref[...]
    pos = (i % tiles_per_seq) * tm + lax.broadcasted_iota(jnp.int32, (tm, 1), 0)
    for g, wnd in enumerate(POOL_WINDOWS):
        c0 = g * POOL_GC
        xg = x_ref[:, c0:c0 + POOL_GC]
        win = xg
        for k in range(1, wnd):
            win = win + ext_scr[hb - k:hb - k + tm, c0:c0 + POOL_GC]
        cnt = jnp.minimum(pos + 1, wnd).astype(F32)
        pooled = win / cnt - xg
        y = _dot(pooled.astype(BF16), w_ref[g]) * scale_ref[:, c0:c0 + POOL_GC]
        ext_scr[hb:hb + tm, c0:c0 + POOL_GC] = ALPHA * xg + y
    out_ref[...] = _layer_norm(ext_scr[hb:hb + tm, :], lng_ref[...], lnb_ref[...])


def _pool_prompt_call(x, w, scale, lng, lnb, tm, tiles_per_seq):
    n = x.shape[0]
    hb = POOL_BUF + 1
    body = functools.partial(_pool_prompt_body, tm=tm, tiles_per_seq=tiles_per_seq)
    return pl.pallas_call(
        body,
        grid=(n // tm,),
        in_specs=[_row_spec(tm, D_MODEL),
                  _prev_rows_spec(tm, hb, D_MODEL, 1),
                  _const_spec(w.shape), _const_spec((1, D_MODEL)),
                  _const_spec((1, D_MODEL)), _const_spec((1, D_MODEL))],
        out_specs=_row_spec(tm, D_MODEL),
        out_shape=jax.ShapeDtypeStruct((n, D_MODEL), F32),
        scratch_shapes=[pltpu.VMEM((tm + hb, D_MODEL), F32)],
        compiler_params=_params("arbitrary"),
        name="pool_mix_ln_prompt",
    )(x, x, w, scale, lng, lnb)


def _pool_sample_body(ext_ref, w_ref, scale_ref, lng_ref, lnb_ref, out_ref, y_scr, *, t_new, bt):
    for g, wnd in enumerate(POOL_WINDOWS):
        c0 = g * POOL_GC
        pooled = []
        for t in range(t_new):
            xg = ext_ref[POOL_BUF + t, :, c0:c0 + POOL_GC]
            win = xg
            for k in range(1, wnd):
                win = win + ext_ref[POOL_BUF + t - k, :, c0:c0 + POOL_GC]
            pooled.append(win / float(wnd) - xg)
        pooled = jnp.concatenate(pooled, axis=0)
        y = _dot(pooled.astype(BF16), w_ref[g]) * scale_ref[:, c0:c0 + POOL_GC]
        for t in range(t_new):
            y_scr[t, :, c0:c0 + POOL_GC] = (ALPHA * ext_ref[POOL_BUF + t, :, c0:c0 + POOL_GC]
                                            + y[t * bt:(t + 1) * bt])
    for t in range(t_new):
        out_ref[t] = _layer_norm(y_scr[t], lng_ref[...], lnb_ref[...])


def _pool_sample_call(ext, w, scale, lng, lnb, bt):
    rows, nb, _ = ext.shape
    t_new = rows - POOL_BUF
    body = functools.partial(_pool_sample_body, t_new=t_new, bt=bt)
    return pl.pallas_call(
        body,
        grid=(nb // bt,),
        in_specs=[pl.BlockSpec((rows, bt, D_MODEL), lambda i: (0, i, 0)),
                  _const_spec(w.shape), _const_spec((1, D_MODEL)),
                  _const_spec((1, D_MODEL)), _const_spec((1, D_MODEL))],
        out_specs=pl.BlockSpec((t_new, bt, D_MODEL), lambda i: (0, i, 0)),
        out_shape=jax.ShapeDtypeStruct((t_new, nb, D_MODEL), F32),
        scratch_shapes=[pltpu.VMEM((t_new, bt, D_MODEL), F32)],
        compiler_params=_params("parallel"),
        name="pool_mix_ln_sample",
    )(ext, w, scale, lng, lnb)


def _rope_pair(pair, cs):
    t = pair * cs
    return t + pltpu.roll(t, MLA_ROPE, 1)


def _mla_a_body(x_ref, cs_ref, w_ref, gq_ref, gkv_ref, cq_ref, ckv_ref, kpe_ref, kpad_ref):
    a = _dot(x_ref[...].astype(BF16), w_ref[...])
    cq_ref[...] = _rms_norm(a[:, :MLA_Q_RANK], gq_ref[...]).astype(BF16)
    ckv_ref[...] = _rms_norm(a[:, MLA_Q_RANK:MLA_Q_RANK + MLA_KV_RANK], gkv_ref[...])
    r = _rope_pair(a[:, MLA_Q_RANK + MLA_KV_RANK:], cs_ref[...])
    kpe_ref[...] = r[:, :MLA_ROPE]
    lane = lax.broadcasted_iota(jnp.int32, r.shape, 1)
    kpad_ref[...] = jnp.where(lane < MLA_ROPE, r, 0.0).astype(BF16)


def _mla_a_call(x, cs, w, gq, gkv, tm, tiles_per_seq):
    n = x.shape[0]
    return pl.pallas_call(
        _mla_a_body,
        grid=(n // tm,),
        in_specs=[_row_spec(tm, D_MODEL),
                  pl.BlockSpec((tm, LANES), lambda i: (i % tiles_per_seq, 0)),
                  _const_spec(w.shape), _const_spec((1, MLA_Q_RANK)),
                  _const_spec((1, MLA_KV_RANK))],
        out_specs=[_row_spec(tm, MLA_Q_RANK), _row_spec(tm, MLA_KV_RANK),
                   _row_spec(tm, MLA_ROPE), _row_spec(tm, LANES)],
        out_shape=[jax.ShapeDtypeStruct((n, MLA_Q_RANK), BF16),
                   jax.ShapeDtypeStruct((n, MLA_KV_RANK), F32),
                   jax.ShapeDtypeStruct((n, MLA_ROPE), F32),
                   jax.ShapeDtypeStruct((n, LANES), BF16)],
        compiler_params=_params("parallel"),
        name="mla_down_proj",
    )(x, cs, w, gq, gkv)


MLA_QW = MLA_NOPE + 2 * MLA_ROPE


def _mla_q_body(cq_ref, cs_ref, w_ref, q_ref):
    cs = cs_ref[...]
    for h in range(MLA_HEADS):
        c0 = h * MLA_QW
        qh = _dot(cq_ref[...], w_ref[:, c0:c0 + MLA_QW])
        q_ref[:, c0:c0 + MLA_NOPE] = qh[:, :MLA_NOPE].astype(BF16)
        q_ref[:, c0 + MLA_NOPE:c0 + MLA_QW] = _rope_pair(qh[:, MLA_NOPE:], cs).astype(BF16)


def _mla_q_call(cq, cs, w, tm, tiles_per_seq):
    n = cq.shape[0]
    return pl.pallas_call(
        _mla_q_body,
        grid=(n // tm,),
        in_specs=[_row_spec(tm, MLA_Q_RANK),
                  pl.BlockSpec((tm, LANES), lambda i: (i % tiles_per_seq, 0)),
                  _const_spec(w.shape)],
        out_specs=_row_spec(tm, MLA_HEADS * MLA_QW),
        out_shape=jax.ShapeDtypeStruct((n, MLA_HEADS * MLA_QW), BF16),
        compiler_params=_params("parallel"),
        name="mla_q_proj",
    )(cq, cs, w)


def _mla_kv_up_body(ckv_ref, kpad_ref, wuk_ref, wuv_ref, k_ref, v_ref):
    ckv = ckv_ref[...].astype(BF16)
    v_ref[...] = _dot(ckv, wuv_ref[...]).astype(BF16)
    for h in range(MLA_HEADS):
        c0 = h * MLA_QW
        k_ref[:, c0:c0 + MLA_NOPE] = _dot(ckv, wuk_ref[:, h * MLA_NOPE:(h + 1) * MLA_NOPE]).astype(BF16)
        k_ref[:, c0 + MLA_NOPE:c0 + MLA_QW] = kpad_ref[...]


def _mla_kv_up_call(ckv, kpad, wuk, wuv, tm):
    n = ckv.shape[0]
    return pl.pallas_call(
        _mla_kv_up_body,
        grid=(n // tm,),
        in_specs=[_row_spec(tm, MLA_KV_RANK), _row_spec(tm, LANES),
                  _const_spec(wuk.shape), _const_spec(wuv.shape)],
        out_specs=[_row_spec(tm, MLA_HEADS * MLA_QW), _row_spec(tm, MLA_HEADS * MLA_V)],
        out_shape=[jax.ShapeDtypeStruct((n, MLA_HEADS * MLA_QW), BF16),
                   jax.ShapeDtypeStruct((n, MLA_HEADS * MLA_V), BF16)],
        compiler_params=_params("parallel"),
        name="mla_kv_up_proj",
    )(ckv, kpad, wuk, wuv)


def _softmax_init(m_scr, l_scr, acc_scr):
    m_scr[...] = jnp.full(m_scr.shape, NEG_INF, F32)
    l_scr[...] = jnp.zeros(l_scr.shape, F32)
    acc_scr[...] = jnp.zeros(acc_scr.shape, F32)


MLA_EXP2_SCALE = MLA_SCALE * math.log2(math.e)


def _flash_update(s, v, m_ref, l_ref, acc_ref):
    m_prev = m_ref[...]
    m_new = jnp.maximum(m_prev, jnp.max(s, axis=-1, keepdims=True))
    corr = jnp.exp2((m_prev - m_new) * MLA_EXP2_SCALE)
    p = jnp.exp2(s * MLA_EXP2_SCALE - m_new * MLA_EXP2_SCALE)
    l_ref[...] = l_ref[...] * corr + jnp.sum(p, axis=-1, keepdims=True)
    acc_ref[...] = acc_ref[...] * corr + _dot(p.astype(BF16), v)
    m_ref[...] = m_new


def _mla_flash_body(q_ref, k_ref, v_ref, o_ref, s_scr, m_scr, l_scr, acc_scr, *, t, heads):
    qi = pl.program_id(2)

    def scores(hh, ki):
        k = k_ref[pl.ds(pl.multiple_of(ki * t, t), t), hh * MLA_QW:(hh + 1) * MLA_QW]
        return _dot_nt(q_ref[:, hh * MLA_QW:(hh + 1) * MLA_QW], k)

    def values(hh, ki):
        return v_ref[pl.ds(pl.multiple_of(ki * t, t), t), hh * MLA_V:(hh + 1) * MLA_V]

    def advance(ki, slot):
        for hh in range(heads):
            s_scr[hh, 1 - slot] = scores(hh, ki + 1)
        for hh in range(heads):
            _flash_update(s_scr[hh, slot], values(hh, ki), m_scr.at[hh], l_scr.at[hh],
                          acc_scr.at[hh])

    def finish(slot):
        row = lax.broadcasted_iota(jnp.int32, (t, t), 0)
        col = lax.broadcasted_iota(jnp.int32, (t, t), 1)
        for hh in range(heads):
            s = jnp.where(col <= row, s_scr[hh, slot], NEG_INF)
            _flash_update(s, values(hh, qi), m_scr.at[hh], l_scr.at[hh], acc_scr.at[hh])
            o_ref[:, hh * MLA_V:(hh + 1) * MLA_V] = (acc_scr[hh] / l_scr[hh]).astype(BF16)

    for hh in range(heads):
        _softmax_init(m_scr.at[hh], l_scr.at[hh], acc_scr.at[hh])
        s_scr[hh, 0] = scores(hh, 0)

    def pair_step(kk, carry):
        advance(2 * kk, 0)
        advance(2 * kk + 1, 1)
        return carry

    lax.fori_loop(0, qi // 2, pair_step, 0)

    @pl.when(qi % 2 == 0)
    def _():
        finish(0)

    @pl.when(qi % 2 == 1)
    def _():
        advance(qi - 1, 0)
        finish(1)


def _mla_flash_call(q, k, v, batch, seq):
    t, heads = FLASH_T, FLASH_HEADS
    nq = seq // t
    body = functools.partial(_mla_flash_body, t=t, heads=heads)
    return pl.pallas_call(
        body,
        grid=(batch, MLA_HEADS // heads, nq),
        in_specs=[pl.BlockSpec((t, heads * MLA_QW), lambda b, h, i: (b * nq + i, h)),
                  pl.BlockSpec((seq, heads * MLA_QW), lambda b, h, i: (b, h)),
                  pl.BlockSpec((seq, heads * MLA_V), lambda b, h, i: (b, h))],
        out_specs=pl.BlockSpec((t, heads * MLA_V), lambda b, h, i: (b * nq + i, h)),
        out_shape=jax.ShapeDtypeStruct((batch * seq, MLA_HEADS * MLA_V), BF16),
        scratch_shapes=[pltpu.VMEM((heads, 2, t, t), F32),
                        pltpu.VMEM((heads, t, 1), F32), pltpu.VMEM((heads, t, 1), F32),
                        pltpu.VMEM((heads, t, MLA_V), F32)],
        compiler_params=_params("parallel", "parallel", "arbitrary"),
        name="mla_prompt_attention",
    )(q, k, v)


MLA_LATQ = MLA_KV_RANK + 2 * MLA_ROPE


def _mla_absorb_q_body(q_ref, wuk_ref, out_ref):
    q = q_ref[...]
    out_ref[:, :MLA_KV_RANK] = _dot_nt(q[:, :MLA_NOPE], wuk_ref[...]).astype(BF16)
    out_ref[:, MLA_KV_RANK:] = q[:, MLA_NOPE:]


def _mla_absorb_q_call(q, wuk):
    n = q.shape[0]
    return pl.pallas_call(
        _mla_absorb_q_body,
        grid=(MLA_HEADS,),
        in_specs=[pl.BlockSpec((n, MLA_QW), lambda h: (0, h)),
                  pl.BlockSpec((MLA_KV_RANK, MLA_NOPE), lambda h: (0, h))],
        out_specs=pl.BlockSpec((n, MLA_LATQ), lambda h: (0, h)),
        out_shape=jax.ShapeDtypeStruct((n, MLA_HEADS * MLA_LATQ), BF16),
        compiler_params=_params("parallel"),
        name="mla_absorb_q",
    )(q, wuk)


def _mla_decode_body(pt_ref, q_ref, nck_ref, nkp_ref, *rest, pages, chains, t_new):
    ck_refs = rest[:pages]
    kp_refs = rest[pages:2 * pages]
    o_ref = rest[2 * pages]
    ck_scr, kp_scr, m_scr, l_scr, acc_scr = rest[2 * pages + 1:]
    step = pl.program_id(1)
    per_chain = pages // chains
    span = per_chain * PAGE_SIZE

    @pl.when(step == 0)
    def _():
        _softmax_init(m_scr, l_scr, acc_scr)

    q_lat = q_ref[0, :, :MLA_KV_RANK]
    q_pe = q_ref[0, :, MLA_KV_RANK:MLA_KV_RANK + MLA_ROPE]
    scores = []
    for c in range(chains):
        for g in range(c * per_chain, (c + 1) * per_chain):
            ck_scr[g * PAGE_SIZE:(g + 1) * PAGE_SIZE, :] = ck_refs[g][0, 0].astype(BF16)
            kp_scr[:, g * PAGE_SIZE:(g + 1) * PAGE_SIZE] = kp_refs[g][0, 0].astype(BF16)
        scores.append(_dot_nt(q_lat, ck_scr[c * span:(c + 1) * span, :])
                      + _dot(q_pe, kp_scr[:, c * span:(c + 1) * span]))
    for c in range(chains):
        _flash_update(scores[c], ck_scr[c * span:(c + 1) * span, :], m_scr.at[c], l_scr.at[c],
                      acc_scr.at[c])

    @pl.when(step == pl.num_programs(1) - 1)
    def _():
        for c in range(1, chains):
            m = jnp.maximum(m_scr[0], m_scr[c])
            w0 = jnp.exp2((m_scr[0] - m) * MLA_EXP2_SCALE)
            wc = jnp.exp2((m_scr[c] - m) * MLA_EXP2_SCALE)
            l_scr[0] = l_scr[0] * w0 + l_scr[c] * wc
            acc_scr[0] = acc_scr[0] * w0 + acc_scr[c] * wc
            m_scr[0] = m
        pad = PAGE_SIZE - nck_ref.shape[1]
        nck = jnp.concatenate([nck_ref[0], jnp.zeros((pad, MLA_KV_RANK), F32)],
                              axis=0).astype(BF16)
        nkp = jnp.concatenate([nkp_ref[0], jnp.zeros((pad, MLA_ROPE), F32)],
                              axis=0).astype(BF16)
        s2 = _dot_nt(q_lat, nck) + _dot_nt(q_pe, nkp)
        tq = lax.broadcasted_iota(jnp.int32, s2.shape, 0) // MLA_HEADS
        tk = lax.broadcasted_iota(jnp.int32, s2.shape, 1)
        s2 = jnp.where((tk <= tq) & (tk < t_new), s2, NEG_INF)
        _flash_update(s2, nck, m_scr.at[0], l_scr.at[0], acc_scr.at[0])
        o_ref[0] = (acc_scr[0] / l_scr[0]).astype(BF16)


def _mla_decode_call(page_table, q, new_ckv, new_kpe, cache_ckv, cache_kpe_t, layer, t_new):
    nb, rows, _ = q.shape
    n_pages = page_table.shape[1]
    pages, chains = DECODE_PAGES, DECODE_CHAINS
    pad_rows = new_ckv.shape[1]

    def page_spec(shape, g):
        return pl.BlockSpec((1, 1) + shape, lambda b, s, pt: (layer, pt[b, s * pages + g], 0, 0))

    body = functools.partial(_mla_decode_body, pages=pages, chains=chains, t_new=t_new)
    grid_spec = pltpu.PrefetchScalarGridSpec(
        num_scalar_prefetch=1,
        grid=(nb, n_pages // pages),
        in_specs=([pl.BlockSpec((1, rows, MLA_LATQ), lambda b, s, pt: (b, 0, 0)),
                   pl.BlockSpec((1, pad_rows, MLA_KV_RANK), lambda b, s, pt: (b, 0, 0)),
                   pl.BlockSpec((1, pad_rows, MLA_ROPE), lambda b, s, pt: (b, 0, 0))]
                  + [page_spec((PAGE_SIZE, MLA_KV_RANK), g) for g in range(pages)]
                  + [page_spec((MLA_ROPE, PAGE_SIZE), g) for g in range(pages)]),
        out_specs=pl.BlockSpec((1, rows, MLA_KV_RANK), lambda b, s, pt: (b, 0, 0)),
        scratch_shapes=[pltpu.VMEM((pages * PAGE_SIZE, MLA_KV_RANK), BF16),
                        pltpu.VMEM((MLA_ROPE, pages * PAGE_SIZE), BF16),
                        pltpu.VMEM((chains, rows, 1), F32), pltpu.VMEM((chains, rows, 1), F32),
                        pltpu.VMEM((chains, rows, MLA_KV_RANK), F32)],
    )
    return pl.pallas_call(
        body,
        grid_spec=grid_spec,
        out_shape=jax.ShapeDtypeStruct((nb, rows, MLA_KV_RANK), BF16),
        compiler_params=_params("parallel", "arbitrary"),
        name="mla_paged_decode",
    )(page_table, q, new_ckv, new_kpe, *([cache_ckv] * pages), *([cache_kpe_t] * pages))


def _mla_absorb_o_body(o_ref, wuv_ref, out_ref):
    out_ref[...] = _dot(o_ref[0], wuv_ref[...]).astype(BF16)


def _mla_absorb_o_call(o_lat, wuv):
    _, n, _ = o_lat.shape
    return pl.pallas_call(
        _mla_absorb_o_body,
        grid=(MLA_HEADS,),
        in_specs=[pl.BlockSpec((1, n, MLA_KV_RANK), lambda h: (h, 0, 0)),
                  pl.BlockSpec((MLA_KV_RANK, MLA_V), lambda h: (0, h))],
        out_specs=pl.BlockSpec((n, MLA_V), lambda h: (0, h)),
        out_shape=jax.ShapeDtypeStruct((n, MLA_HEADS * MLA_V), BF16),
        compiler_params=_params("parallel"),
        name="mla_absorb_o",
    )(o_lat, wuv)


SWA_NQ = SWA_HEADS * SWA_HEAD_DIM
SWA_NKV = SWA_KV_HEADS * SWA_HEAD_DIM
SWA_PAIRS = SWA_KV_HEADS // 2


def _swa_q_perm():
    cols = []
    for p in range(SWA_PAIRS):
        for g in range(SWA_GROUP):
            for half in range(2):
                head = (2 * p + half) * SWA_GROUP + g
                cols.extend(range(head * SWA_HEAD_DIM, (head + 1) * SWA_HEAD_DIM))
    return np.asarray(cols, dtype=np.int32)


def _swa_slope(head):
    return 2.0 ** (-8.0 * (head + 1) / SWA_HEADS)


def _swa_qkv_body(x_ref, w_ref, b_ref, q_ref, k_ref, v_ref):
    qkv = _dot(x_ref[...].astype(BF16), w_ref[...]) + b_ref[...]
    q_ref[...] = qkv[:, :SWA_NQ].astype(BF16)
    k_ref[...] = qkv[:, SWA_NQ:SWA_NQ + SWA_NKV]
    v_ref[...] = qkv[:, SWA_NQ + SWA_NKV:]


def _swa_qkv_call(x, w, b, tm):
    n = x.shape[0]
    return pl.pallas_call(
        _swa_qkv_body,
        grid=(n // tm,),
        in_specs=[_row_spec(tm, D_MODEL), _const_spec(w.shape), _const_spec(b.shape)],
        out_specs=[_row_spec(tm, SWA_NQ), _row_spec(tm, SWA_NKV), _row_spec(tm, SWA_NKV)],
        out_shape=[jax.ShapeDtypeStruct((n, SWA_NQ), BF16),
                   jax.ShapeDtypeStruct((n, SWA_NKV), F32),
                   jax.ShapeDtypeStruct((n, SWA_NKV), F32)],
        compiler_params=_params("parallel"),
        name="swa_qkv_proj",
    )(x, w, b)


def _swa_heads(q_blk, k_pair, v_pair, sink_ref, p, g, dist, valid):
    lane_q = lax.broadcasted_iota(jnp.int32, q_blk.shape, 1)
    outs = []
    for half in range(2):
        head = (2 * p + half) * SWA_GROUP + g
        in_half = (lane_q >= SWA_HEAD_DIM) if half else (lane_q < SWA_HEAD_DIM)
        qm = jnp.where(in_half, q_blk, jnp.zeros_like(q_blk))
        s = _dot_nt(qm, k_pair) * SWA_SCALE - _swa_slope(head) * dist
        s = jnp.where(valid, s, NEG_INF)
        sink = sink_ref[head]
        m = jnp.maximum(jnp.max(s, axis=-1, keepdims=True), sink)
        e = jnp.exp(s - m)
        denom = jnp.sum(e, axis=-1, keepdims=True) + jnp.exp(sink - m)
        outs.append(_dot((e / denom).astype(BF16), v_pair))
    lane_o = lax.broadcasted_iota(jnp.int32, outs[0].shape, 1)
    return jnp.where(lane_o < SWA_HEAD_DIM, outs[0], outs[1])


def _swa_prompt_body(sink_ref, q_ref, kc_ref, kp_ref, vc_ref, vp_ref, o_ref):
    i = pl.program_id(1)
    k = jnp.concatenate([kp_ref[...].astype(BF16), kc_ref[...].astype(BF16)], axis=0)
    v = jnp.concatenate([vp_ref[...].astype(BF16), vc_ref[...].astype(BF16)], axis=0)
    row = lax.broadcasted_iota(jnp.int32, (WINDOW, 2 * WINDOW), 0)
    col = lax.broadcasted_iota(jnp.int32, (WINDOW, 2 * WINDOW), 1)
    dist = row + WINDOW - col
    valid = (dist >= 0) & (dist < WINDOW) & ((col >= WINDOW) | (i > 0))
    distf = dist.astype(F32)
    for p in range(SWA_PAIRS):
        kpair = k[:, p * LANES:(p + 1) * LANES]
        vpair = v[:, p * LANES:(p + 1) * LANES]
        for g in range(SWA_GROUP):
            c0 = (p * SWA_GROUP + g) * LANES
            o = _swa_heads(q_ref[:, c0:c0 + LANES], kpair, vpair, sink_ref, p, g, distf, valid)
            o_ref[:, c0:c0 + LANES] = o.astype(BF16)


def _swa_prompt_call(sinks, q, k, v, batch, seq):
    nb = seq // WINDOW
    cur = lambda b, i: (b * nb + i, 0)
    prev = lambda b, i: (b * nb + jnp.maximum(i - 1, 0), 0)
    return pl.pallas_call(
        _swa_prompt_body,
        grid=(batch, nb),
        in_specs=[pl.BlockSpec(memory_space=pltpu.SMEM),
                  pl.BlockSpec((WINDOW, SWA_NQ), cur),
                  pl.BlockSpec((WINDOW, SWA_NKV), cur), pl.BlockSpec((WINDOW, SWA_NKV), prev),
                  pl.BlockSpec((WINDOW, SWA_NKV), cur), pl.BlockSpec((WINDOW, SWA_NKV), prev)],
        out_specs=pl.BlockSpec((WINDOW, SWA_NQ), cur),
        out_shape=jax.ShapeDtypeStruct((batch * seq, SWA_NQ), BF16),
        compiler_params=_params("parallel", "arbitrary"),
        name="swa_prompt_attention",
    )(sinks, q, k, k, v, v)


def _swa_sample_body(sink_ref, q_ref, kc_ref, vc_ref, kn_ref, vn_ref, o_ref, *, bt, t_pad):
    n_keys = 2 * WINDOW
    fill = jnp.zeros((n_keys - WINDOW - t_pad, SWA_NKV), F32)
    rows = 2 * SWA_GROUP * t_pad
    ridx = lax.broadcasted_iota(jnp.int32, (rows, 1), 0) // t_pad
    tq = lax.broadcasted_iota(jnp.int32, (rows, n_keys), 0) % t_pad
    col = lax.broadcasted_iota(jnp.int32, (rows, n_keys), 1)
    dist = tq + WINDOW - col
    valid = (dist >= 0) & (dist < WINDOW)
    distf = dist.astype(F32)
    lane = lax.broadcasted_iota(jnp.int32, (t_pad, LANES), 1)
    bias, sink_cols = [], []
    for p in range(SWA_PAIRS):
        slope = jnp.zeros((rows, 1), F32)
        sink = jnp.zeros((rows, 1), F32)
        for half in range(2):
            for g in range(SWA_GROUP):
                head = (2 * p + half) * SWA_GROUP + g
                here = ridx == half * SWA_GROUP + g
                slope = jnp.where(here, _swa_slope(head), slope)
                sink = jnp.where(here, sink_ref[head], sink)
        bias.append(slope * distf)
        sink_cols.append(sink)

    def one_seq(bb, carry):
        k = jnp.concatenate([kc_ref[bb], kn_ref[bb], fill], axis=0).astype(BF16)
        v = jnp.concatenate([vc_ref[bb], vn_ref[bb], fill], axis=0).astype(BF16)
        for p in range(SWA_PAIRS):
            blocks = [q_ref[bb, :, (p * SWA_GROUP + g) * LANES:(p * SWA_GROUP + g + 1) * LANES]
                      .astype(F32) for g in range(SWA_GROUP)]
            stack = ([jnp.where(lane < SWA_HEAD_DIM, blk, 0.0) for blk in blocks]
                     + [jnp.where(lane >= SWA_HEAD_DIM, blk, 0.0) for blk in blocks])
            qs = jnp.concatenate(stack, axis=0).astype(BF16)
            s = _dot_nt(qs, k[:, p * LANES:(p + 1) * LANES]) * SWA_SCALE - bias[p]
            s = jnp.where(valid, s, NEG_INF)
            m = jnp.maximum(jnp.max(s, axis=-1, keepdims=True), sink_cols[p])
            e = jnp.exp(s - m)
            denom = jnp.sum(e, axis=-1, keepdims=True) + jnp.exp(sink_cols[p] - m)
            o = _dot((e / denom).astype(BF16), v[:, p * LANES:(p + 1) * LANES])
            for g in range(SWA_GROUP):
                lo = o[g * t_pad:(g + 1) * t_pad]
                hi = o[(SWA_GROUP + g) * t_pad:(SWA_GROUP + g + 1) * t_pad]
                c0 = (p * SWA_GROUP + g) * LANES
                o_ref[bb, :, c0:c0 + LANES] = jnp.where(lane < SWA_HEAD_DIM, lo, hi).astype(BF16)
        return carry

    lax.fori_loop(0, bt, one_seq, 0)


def _swa_sample_call(sinks, q, kc, vc, kn, vn):
    nb, t_pad, _ = q.shape
    bt = SWA_SEQ_TILE
    body = functools.partial(_swa_sample_body, bt=bt, t_pad=t_pad)
    seq3 = lambda rows, cols: pl.BlockSpec((bt, rows, cols), lambda i: (i, 0, 0))
    return pl.pallas_call(
        body,
        grid=(nb // bt,),
        in_specs=[pl.BlockSpec(memory_space=pltpu.SMEM),
                  seq3(t_pad, SWA_NQ), seq3(WINDOW, SWA_NKV), seq3(WINDOW, SWA_NKV),
                  seq3(t_pad, SWA_NKV), seq3(t_pad, SWA_NKV)],
        out_specs=seq3(t_pad, SWA_NQ),
        out_shape=jax.ShapeDtypeStruct((nb, t_pad, SWA_NQ), BF16),
        compiler_params=_params("parallel"),
        name="swa_sample_attention",
    )(sinks, q, kc, vc, kn, vn)


def _rot_cols(w):
    half = w.shape[-1] // 2
    return jnp.concatenate([-w[..., half:], w[..., :half]], axis=-1)


def _rope_table(pos):
    half = MLA_ROPE // 2
    inv = ROPE_THETA ** (-jnp.arange(half, dtype=F32) / half)
    ang = pos.astype(F32)[:, None] * inv[None, :]
    cos, sin = jnp.cos(ang), jnp.sin(ang)
    return jnp.concatenate([cos, cos, sin, sin], axis=-1)


def _time_major(a, nb, t):
    return jnp.swapaxes(a, 0, 1).reshape((t * nb,) + a.shape[2:])


def _batch_major(a, nb, t):
    return jnp.swapaxes(a.reshape((t, nb) + a.shape[1:]), 0, 1)


def kernel(x_prompt, x_sample, state_pool, cache_mla_ckv, cache_mla_kpe, page_table, cache_swa_k, cache_swa_v, state_ffn_conv, ln_g, ln_b, pool_w, pool_scale, mla_w_a, mla_g_q, mla_g_kv, mla_w_uq, mla_w_uk, mla_w_uv, mla_w_o, swa_w_qkv, swa_b_qkv, swa_sinks, swa_w_o, swa_b_o, ffn_w_in, ffn_conv_w, ffn_conv_b, ffn_w_out):
    B, S, D = x_prompt.shape
    NB, T, _ = x_sample.shape
    NS = NB * T
    tm = ROW_TILE
    tps = S // tm
    assert S % tm == 0 and NS == tm and S % FLASH_T == 0 and T <= SUBLANES

    xp = x_prompt.reshape(B * S, D)
    xs = _time_major(x_sample, NB, T)
    cs_p = _rope_table(jnp.arange(S))
    cs_s = jnp.repeat(_rope_table(PAST_LEN + jnp.arange(T)), NB, axis=0)
    zero_bias = jnp.zeros((1, D), F32)
    q_perm = _swa_q_perm()
    ffn = (ffn_w_in.astype(BF16), ffn_w_out.astype(BF16), ffn_conv_w,
           ffn_conv_b.reshape(DEPTH, 1, D_FF))
    ln_rows = (ln_g.reshape(2 * DEPTH, 1, D), ln_b.reshape(2 * DEPTH, 1, D))

    pool_p, pool_s, ckv_p, ckv_s, kpe_p, kpe_s = [], [], [], [], [], []
    swk_p, swk_s, swv_p, swv_s, conv_p, conv_s = [], [], [], [], [], []

    for i in range(DEPTH):
        kind, j = i % N_MIXERS, i // N_MIXERS
        g1, b1 = ln_g[i, 0][None, :], ln_b[i, 0][None, :]
        if kind == 0:
            w = pool_w[j].astype(BF16)
            sc = pool_scale[j][None, :]
            pool_p.append(xp.reshape(B, S, D)[:, S - POOL_BUF:])
            xs_bm = _batch_major(xs, NB, T)
            ext = jnp.concatenate([state_pool[j], xs_bm], axis=1)
            pool_s.append(ext[:, -POOL_BUF:])
            xp = _pool_prompt_call(xp, w, sc, g1, b1, tm, tps)
            xs = _pool_sample_call(jnp.swapaxes(ext, 0, 1), w, sc, g1, b1, 32).reshape(NS, D)
        elif kind == 1:
            w_a = mla_w_a[j]
            n_lat = MLA_Q_RANK + MLA_KV_RANK
            w_a_ext = jnp.concatenate([w_a, _rot_cols(w_a[:, n_lat:])], axis=1).astype(BF16)
            w_uq = mla_w_uq[j]
            w_uq_ext = jnp.concatenate([w_uq, _rot_cols(w_uq[..., MLA_NOPE:])], axis=-1)
            w_uq_ext = w_uq_ext.reshape(MLA_Q_RANK, MLA_HEADS * MLA_QW).astype(BF16)
            w_uk = mla_w_uk[j].reshape(MLA_KV_RANK, MLA_HEADS * MLA_NOPE).astype(BF16)
            w_uv = mla_w_uv[j].reshape(MLA_KV_RANK, MLA_HEADS * MLA_V).astype(BF16)
            w_o = mla_w_o[j].reshape(MLA_HEADS * MLA_V, D).astype(BF16)
            gq, gkv = mla_g_q[j][None, :], mla_g_kv[j][None, :]

            cq, ckv, kpe, kpad = _mla_a_call(xp, cs_p, w_a_ext, gq, gkv, tm, tps)
            q = _mla_q_call(cq, cs_p, w_uq_ext, tm, tps)
            k_full, v_full = _mla_kv_up_call(ckv, kpad, w_uk, w_uv, tm)
            o = _mla_flash_call(q, k_full, v_full, B, S)
            ckv_p.append(ckv.reshape(B, S, MLA_KV_RANK))
            kpe_p.append(kpe.reshape(B, S, MLA_ROPE))
            xp = _proj_ln_call(o, xp, w_o, zero_bias, g1, b1, tm)

            cq, ckv, kpe, _ = _mla_a_call(xs, cs_s, w_a_ext, gq, gkv, tm, 1)
            q = _mla_q_call(cq, cs_s, w_uq_ext, tm, 1)
            q_abs = _mla_absorb_q_call(q, w_uk)
            q_abs = _batch_major(q_abs.reshape(NS, MLA_HEADS, MLA_LATQ), NB, T)
            q_abs = q_abs.reshape(NB, T * MLA_HEADS, MLA_LATQ)
            ckv_bm = _batch_major(ckv, NB, T)
            kpe_bm = _batch_major(kpe, NB, T)
            pad = ((0, 0), (0, SUBLANES - T), (0, 0))
            o_lat = _mla_decode_call(page_table, q_abs, jnp.pad(ckv_bm, pad), jnp.pad(kpe_bm, pad),
                                     cache_mla_ckv, jnp.swapaxes(cache_mla_kpe, 2, 3), j, T)
            o_lat = o_lat.reshape(NB, T, MLA_HEADS, MLA_KV_RANK).transpose(2, 1, 0, 3)
            o = _mla_absorb_o_call(o_lat.reshape(MLA_HEADS, NS, MLA_KV_RANK), w_uv)
            ckv_s.append(ckv_bm)
            kpe_s.append(kpe_bm)
            xs = _proj_ln_call(o, xs, w_o, zero_bias, g1, b1, tm)
        else:
            w_qkv, b_qkv = swa_w_qkv[j], swa_b_qkv[j]
            w_qkv = jnp.concatenate([w_qkv[:, :SWA_NQ][:, q_perm], w_qkv[:, SWA_NQ:]], axis=1)
            b_qkv = jnp.concatenate([b_qkv[:SWA_NQ][q_perm], b_qkv[SWA_NQ:]])[None, :]
            w_qkv = w_qkv.astype(BF16)
            w_o = swa_w_o[j][q_perm, :].astype(BF16)
            b_o = swa_b_o[j][None, :]
            sinks = swa_sinks[j]

            q, k, v = _swa_qkv_call(xp, w_qkv, b_qkv, tm // 2)
            o = _swa_prompt_call(sinks, q, k, v, B, S)
            kv_shape = (B, WINDOW, SWA_KV_HEADS, SWA_HEAD_DIM)
            swk_p.append(k.reshape(B, S, SWA_NKV)[:, S - WINDOW:].reshape(kv_shape))
            swv_p.append(v.reshape(B, S, SWA_NKV)[:, S - WINDOW:].reshape(kv_shape))
            xp = _proj_ln_call(o, xp, w_o, b_o, g1, b1, tm)

            q, k, v = _swa_qkv_call(xs, w_qkv, b_qkv, tm // 2)
            pad = ((0, 0), (0, SUBLANES - T), (0, 0))
            q_bm = jnp.pad(_batch_major(q, NB, T), pad)
            k_bm, v_bm = _batch_major(k, NB, T), _batch_major(v, NB, T)
            kc = cache_swa_k[j].reshape(NB, WINDOW, SWA_NKV)
            vc = cache_swa_v[j].reshape(NB, WINDOW, SWA_NKV)
            o = _swa_sample_call(sinks, q_bm, kc, vc, jnp.pad(k_bm, pad), jnp.pad(v_bm, pad))
            o = _time_major(o[:, :T], NB, T)
            kv_shape = (NB, WINDOW, SWA_KV_HEADS, SWA_HEAD_DIM)
            swk_s.append(jnp.concatenate([kc, k_bm], axis=1)[:, T:].reshape(kv_shape))
            swv_s.append(jnp.concatenate([vc, v_bm], axis=1)[:, T:].reshape(kv_shape))
            xs = _proj_ln_call(o, xs, w_o, b_o, g1, b1, tm)

        xp, tails = _ffn_call(xp, None, ffn, ln_rows, i, tm=tm, shift=1, tiles_per_seq=tps)
        conv_p.append(tails.reshape(B, tps, SUBLANES, D_FF)[:, -1, SUBLANES - (CONV_W - 1):])
        prev = _time_major(state_ffn_conv[i], NB, CONV_W - 1)
        xs, tails = _ffn_call(xs, prev, ffn, ln_rows, i, tm=tm, shift=NB, tiles_per_seq=1)
        conv_s.append(_batch_major(tails, NB, CONV_W - 1))

    y_p = xp.reshape(B, S, D)
    y_s = _batch_major(xs, NB, T)
    return (y_p, y_s, jnp.stack(pool_p), jnp.stack(pool_s), jnp.stack(ckv_p), jnp.stack(ckv_s),
            jnp.stack(kpe_p), jnp.stack(kpe_s), jnp.stack(swk_p), jnp.stack(swk_s),
            jnp.stack(swv_p), jnp.stack(swv_s), jnp.stack(conv_p), jnp.stack(conv_s))
```

```python
import functools
import math

import jax
import jax.numpy as jnp
import numpy as np
from jax import lax
from jax.experimental import pallas as pl
from jax.experimental.pallas import tpu as pltpu

D_MODEL = 2048
DEPTH = 4
N_MIXERS = 3
PAST_LEN = 8192
PAGE_SIZE = 128

POOL_WINDOWS = (2, 4, 8, 16)
POOL_GROUPS = len(POOL_WINDOWS)
POOL_GC = D_MODEL // POOL_GROUPS
POOL_BUF = max(POOL_WINDOWS) - 1

MLA_HEADS = 16
MLA_Q_RANK = 512
MLA_KV_RANK = 512
MLA_NOPE = 128
MLA_ROPE = 64
MLA_V = 128
MLA_SCALE = 1.0 / math.sqrt(MLA_NOPE + MLA_ROPE)
ROPE_THETA = 10000.0

SWA_HEADS = 32
SWA_KV_HEADS = 8
SWA_GROUP = SWA_HEADS // SWA_KV_HEADS
SWA_HEAD_DIM = 64
SWA_SCALE = 1.0 / math.sqrt(SWA_HEAD_DIM)
WINDOW = 128

D_FF = 5632
CONV_W = 3

ALPHA = (2.0 * DEPTH) ** 0.25
LN_EPS = 1e-5
RMS_EPS = 1e-6
NEG_INF = -1e30

LANES = 128
SUBLANES = 8
BF16_ROWS = 16
VMEM_LIMIT = 56 * 1024 * 1024

ROW_TILE = 512
FFN_TF = 512
FFN_SAMPLE_TF = 256
FLASH_T = 512
FLASH_HEADS = 2
DECODE_PAGES = 32
DECODE_CHAINS = 2
SWA_SEQ_TILE = 8

F32 = jnp.float32
BF16 = jnp.bfloat16


def _dot(a, b):
    return jnp.dot(a, b, preferred_element_type=F32)


def _dot_nt(a, b):
    return lax.dot_general(a, b, (((1,), (1,)), ((), ())), preferred_element_type=F32)


def _layer_norm(y, g, b):
    mu = jnp.mean(y, axis=-1, keepdims=True)
    d = y - mu
    var = jnp.mean(d * d, axis=-1, keepdims=True)
    return d * lax.rsqrt(var + LN_EPS) * g + b


def _rms_norm(a, g):
    return a * lax.rsqrt(jnp.mean(a * a, axis=-1, keepdims=True) + RMS_EPS) * g


def _params(*sem):
    return pltpu.CompilerParams(dimension_semantics=sem, vmem_limit_bytes=VMEM_LIMIT)


def _row_spec(tm, cols):
    return pl.BlockSpec((tm, cols), lambda i: (i, 0))


def _const_spec(shape):
    nd = len(shape)
    return pl.BlockSpec(shape, lambda i: (0,) * nd)


def _ffn_body(x_ref, halo_ref, wg_ref, wv_ref, wo_ref, cw_ref, cb_ref, lng_ref, lnb_ref,
              out_ref, tail_ref, *rest, tm, halo, shift, halo_is_gate, tail_rows,
              tiles_per_seq, emit_bf16):
    i = pl.program_id(0)
    j = pl.program_id(1)
    xoff = 0 if halo_is_gate else halo
    if emit_bf16:
        wg_out, wv_out, wo_out, xb_scr, g_scr, acc_scr = rest
        wg_out[...] = wg_ref[...].astype(BF16)
        wv_out[...] = wv_ref[...].astype(BF16)
        wo_out[...] = wo_ref[...].astype(BF16)
        wg_ref, wv_ref, wo_ref = wg_out, wv_out, wo_out
    else:
        xb_scr, g_scr, acc_scr = rest

    @pl.when(j == 0)
    def _():
        if not halo_is_gate:
            keep = (i % tiles_per_seq != 0).astype(F32)
            xb_scr[0:halo, :] = (halo_ref[...] * keep).astype(BF16)
        xb_scr[xoff:xoff + tm, :] = x_ref[...].astype(BF16)
        acc_scr[...] = jnp.zeros_like(acc_scr)

    if halo_is_gate:
        g_scr[0:halo, :] = halo_ref[...]
        g_scr[halo:halo + tm, :] = _dot(xb_scr[...], wg_ref[...])
    else:
        g_scr[...] = _dot(xb_scr[...], wg_ref[...])
    val = _dot(xb_scr[xoff:xoff + tm, :], wv_ref[...])
    conv = (cb_ref[...]
            + g_scr[halo - 2 * shift:halo - 2 * shift + tm, :] * cw_ref[0:1, :]
            + g_scr[halo - shift:halo - shift + tm, :] * cw_ref[1:2, :]
            + g_scr[halo:halo + tm, :] * cw_ref[2:3, :])
    h = 0.5 * conv * (1.0 + lax.erf(conv * math.sqrt(0.5))) * val
    acc_scr[...] += _dot(h.astype(BF16), wo_ref[...])
    tail_ref[...] = g_scr[halo + tm - tail_rows:halo + tm, :]

    @pl.when(j == pl.num_programs(1) - 1)
    def _():
        y = ALPHA * x_ref[...] + acc_scr[...]
        out_ref[...] = _layer_norm(y, lng_ref[...], lnb_ref[...])


def _prev_rows_spec(tm, rows, cols, grid_rank):
    per_tile = tm // rows
    if grid_rank == 1:
        return pl.BlockSpec((rows, cols), lambda i: (jnp.maximum(i * per_tile - 1, 0), 0))
    return pl.BlockSpec((rows, cols), lambda i, j: (jnp.maximum(i * per_tile - 1, 0), 0))


def _ffn_call(x, gate_prev, weights, conv, ln, layer, *, tm, tf, shift, tiles_per_seq):
    cw, cb = conv
    lng, lnb = ln
    n = x.shape[0]
    nt = n // tm
    nf = D_FF // tf
    halo_is_gate = gate_prev is not None
    if halo_is_gate:
        halo_rows = tail_rows = gate_prev.shape[0]
        halo, halo_spec = gate_prev, pl.BlockSpec((halo_rows, tf), lambda i, j: (0, j))
        x_rows = tm
        tail_shape = (tail_rows, D_FF)
        tail_spec = pl.BlockSpec((tail_rows, tf), lambda i, j: (0, j))
        w_in, w_out = weights
        w_args = (w_in, w_in, w_out)
        w_specs = [pl.BlockSpec((None, D_MODEL, tf), lambda i, j: (layer, 0, j)),
                   pl.BlockSpec((None, D_MODEL, tf), lambda i, j: (layer, 0, j + nf)),
                   pl.BlockSpec((None, tf, D_MODEL), lambda i, j: (layer, j, 0))]
        extra_specs = [pl.BlockSpec((D_MODEL, tf), lambda i, j: (0, j)),
                       pl.BlockSpec((D_MODEL, tf), lambda i, j: (0, j)),
                       pl.BlockSpec((tf, D_MODEL), lambda i, j: (j, 0))]
        extra_shapes = [jax.ShapeDtypeStruct((D_MODEL, D_FF), BF16),
                        jax.ShapeDtypeStruct((D_MODEL, D_FF), BF16),
                        jax.ShapeDtypeStruct((D_FF, D_MODEL), BF16)]
    else:
        halo_rows, tail_rows = BF16_ROWS, SUBLANES
        halo, halo_spec = x, _prev_rows_spec(tm, halo_rows, D_MODEL, 2)
        x_rows = tm + halo_rows
        tail_shape = (nt, tail_rows, D_FF)
        tail_spec = pl.BlockSpec((None, tail_rows, tf), lambda i, j: (i, 0, j))
        w_args = weights
        w_specs = [pl.BlockSpec((D_MODEL, tf), lambda i, j: (0, j)),
                   pl.BlockSpec((D_MODEL, tf), lambda i, j: (0, j)),
                   pl.BlockSpec((tf, D_MODEL), lambda i, j: (j, 0))]
        extra_specs, extra_shapes = [], []
    body = functools.partial(_ffn_body, tm=tm, halo=halo_rows, shift=shift,
                             halo_is_gate=halo_is_gate, tail_rows=tail_rows,
                             tiles_per_seq=tiles_per_seq, emit_bf16=halo_is_gate)
    ln_row = 2 * layer + 1
    outs = pl.pallas_call(
        body,
        grid=(nt, nf),
        in_specs=[
            pl.BlockSpec((tm, D_MODEL), lambda i, j: (i, 0)),
            halo_spec,
            *w_specs,
            pl.BlockSpec((None, CONV_W, tf), lambda i, j: (layer, 0, j)),
            pl.BlockSpec((None, 1, tf), lambda i, j: (layer, 0, j)),
            pl.BlockSpec((None, 1, D_MODEL), lambda i, j: (ln_row, 0, 0)),
            pl.BlockSpec((None, 1, D_MODEL), lambda i, j: (ln_row, 0, 0)),
        ],
        out_specs=[pl.BlockSpec((tm, D_MODEL), lambda i, j: (i, 0)), tail_spec, *extra_specs],
        out_shape=[jax.ShapeDtypeStruct((n, D_MODEL), F32),
                   jax.ShapeDtypeStruct(tail_shape, F32), *extra_shapes],
        scratch_shapes=[pltpu.VMEM((x_rows, D_MODEL), BF16),
                        pltpu.VMEM((tm + halo_rows, tf), F32),
                        pltpu.VMEM((tm, D_MODEL), F32)],
        compiler_params=_params("parallel", "arbitrary"),
        name="conv_ffn_ln",
    )(x, halo, *w_args, cw, cb, lng, lnb)
    return outs[0], outs[1], tuple(outs[2:])


def _proj_ln_body(a_ref, x_ref, w_ref, bias_ref, lng_ref, lnb_ref, out_ref):
    h = _dot(a_ref[...], w_ref[...]) + bias_ref[...]
    out_ref[...] = _layer_norm(ALPHA * x_ref[...] + h, lng_ref[...], lnb_ref[...])


def _proj_ln_call(a, x, w, bias, lng, lnb, tm):
    n, k = a.shape
    return pl.pallas_call(
        _proj_ln_body,
        grid=(n // tm,),
        in_specs=[_row_spec(tm, k), _row_spec(tm, D_MODEL), _const_spec(w.shape),
                  _const_spec((1, D_MODEL)), _const_spec((1, D_MODEL)), _const_spec((1, D_MODEL))],
        out_specs=_row_spec(tm, D_MODEL),
        out_shape=jax.ShapeDtypeStruct((n, D_MODEL), F32),
        compiler_params=_params("parallel"),
        name="proj_res_ln",
    )(a, x, w, bias, lng, lnb)


POOL_HALO = 4 * SUBLANES


def _pool_prompt_body(x_ref, halo_ref, w_ref, scale_ref, lng_ref, lnb_ref, out_ref, ext_scr,
                      tmp_scr, *, tm, tiles_per_seq):
    i = pl.program_id(0)
    hb = POOL_HALO
    end = hb + tm
    ext_scr[0:hb, :] = halo_ref[...] * (i % tiles_per_seq != 0).astype(F32)
    ext_scr[hb:end, :] = x_ref[...]
    pos = (i % tiles_per_seq) * tm + lax.broadcasted_iota(jnp.int32, (tm, 1), 0)
    for g, wnd in enumerate(POOL_WINDOWS):
        c0 = g * POOL_GC
        stages = wnd.bit_length() - 1
        src, cols = ext_scr, slice(c0, c0 + POOL_GC)
        for k in range(1, stages + 1):
            w = 2 ** (k - 1)
            start = hb if k == stages else SUBLANES * k
            win = src[start:end, cols] + src[start - w:end - w, cols]
            if k < stages:
                dst = tmp_scr.at[k % 2]
                dst[start:end, :] = win
                src, cols = dst, slice(0, POOL_GC)
        xg = x_ref[:, c0:c0 + POOL_GC]
        inv_cnt = 1.0 / jnp.minimum(pos + 1, wnd).astype(F32)
        pooled = win * inv_cnt - xg
        y = _dot(pooled.astype(BF16), w_ref[g]) * scale_ref[:, c0:c0 + POOL_GC]
        ext_scr[hb:end, c0:c0 + POOL_GC] = ALPHA * xg + y
    out_ref[...] = _layer_norm(ext_scr[hb:end, :], lng_ref[...], lnb_ref[...])


def _pool_prompt_call(x, w, scale, lng, lnb, tm, tiles_per_seq):
    n = x.shape[0]
    hb = POOL_HALO
    body = functools.partial(_pool_prompt_body, tm=tm, tiles_per_seq=tiles_per_seq)
    return pl.pallas_call(
        body,
        grid=(n // tm,),
        in_specs=[_row_spec(tm, D_MODEL),
                  _prev_rows_spec(tm, hb, D_MODEL, 1),
                  _const_spec(w.shape), _const_spec((1, D_MODEL)),
                  _const_spec((1, D_MODEL)), _const_spec((1, D_MODEL))],
        out_specs=_row_spec(tm, D_MODEL),
        out_shape=jax.ShapeDtypeStruct((n, D_MODEL), F32),
        scratch_shapes=[pltpu.VMEM((tm + hb, D_MODEL), F32),
                        pltpu.VMEM((2, tm + hb, POOL_GC), F32)],
        compiler_params=_params("parallel"),
        name="pool_mix_ln_prompt",
    )(x, x, w, scale, lng, lnb)


def _pool_sample_body(prev_ref, x_ref, w_ref, scale_ref, lng_ref, lnb_ref, out_ref, y_scr,
                      *, t_new, bt):
    def row(r, cols):
        return prev_ref[r, :, cols] if r < POOL_BUF else x_ref[r - POOL_BUF, :, cols]

    for g, wnd in enumerate(POOL_WINDOWS):
        cols = slice(g * POOL_GC, (g + 1) * POOL_GC)
        pooled = []
        for t in range(t_new):
            win = x_ref[t, :, cols]
            for k in range(1, wnd):
                win = win + row(POOL_BUF + t - k, cols)
            pooled.append(win / float(wnd) - x_ref[t, :, cols])
        pooled = jnp.concatenate(pooled, axis=0)
        y = _dot(pooled.astype(BF16), w_ref[g]) * scale_ref[:, cols]
        for t in range(t_new):
            y_scr[t, :, cols] = ALPHA * x_ref[t, :, cols] + y[t * bt:(t + 1) * bt]
    for t in range(t_new):
        out_ref[t] = _layer_norm(y_scr[t], lng_ref[...], lnb_ref[...])


def _pool_sample_call(prev, x, w, scale, lng, lnb, bt):
    t_new, nb, _ = x.shape
    body = functools.partial(_pool_sample_body, t_new=t_new, bt=bt)
    return pl.pallas_call(
        body,
        grid=(nb // bt,),
        in_specs=[pl.BlockSpec((POOL_BUF, bt, D_MODEL), lambda i: (0, i, 0)),
                  pl.BlockSpec((t_new, bt, D_MODEL), lambda i: (0, i, 0)),
                  _const_spec(w.shape), _const_spec((1, D_MODEL)),
                  _const_spec((1, D_MODEL)), _const_spec((1, D_MODEL))],
        out_specs=pl.BlockSpec((t_new, bt, D_MODEL), lambda i: (0, i, 0)),
        out_shape=jax.ShapeDtypeStruct((t_new, nb, D_MODEL), F32),
        scratch_shapes=[pltpu.VMEM((t_new, bt, D_MODEL), F32)],
        compiler_params=_params("parallel"),
        name="pool_mix_ln_sample",
    )(prev, x, w, scale, lng, lnb)


def _rope_pair(pair, cs):
    t = pair * cs
    return t + pltpu.roll(t, MLA_ROPE, 1)


def _mla_a_body(x_ref, cs_ref, w_ref, gq_ref, gkv_ref, cq_ref, ckv_ref, kpe_ref, kpad_ref):
    a = _dot(x_ref[...].astype(BF16), w_ref[...])
    cq_ref[...] = _rms_norm(a[:, :MLA_Q_RANK], gq_ref[...]).astype(BF16)
    ckv_ref[...] = _rms_norm(a[:, MLA_Q_RANK:MLA_Q_RANK + MLA_KV_RANK], gkv_ref[...])
    r = _rope_pair(a[:, MLA_Q_RANK + MLA_KV_RANK:], cs_ref[...])
    kpe_ref[...] = r[:, :MLA_ROPE]
    lane = lax.broadcasted_iota(jnp.int32, r.shape, 1)
    kpad_ref[...] = jnp.where(lane < MLA_ROPE, r, 0.0).astype(BF16)


def _mla_a_call(x, cs, w, gq, gkv, tm, tiles_per_seq):
    n = x.shape[0]
    return pl.pallas_call(
        _mla_a_body,
        grid=(n // tm,),
        in_specs=[_row_spec(tm, D_MODEL),
                  pl.BlockSpec((tm, LANES), lambda i: (i % tiles_per_seq, 0)),
                  _const_spec(w.shape), _const_spec((1, MLA_Q_RANK)),
                  _const_spec((1, MLA_KV_RANK))],
        out_specs=[_row_spec(tm, MLA_Q_RANK), _row_spec(tm, MLA_KV_RANK),
                   _row_spec(tm, MLA_ROPE), _row_spec(tm, LANES)],
        out_shape=[jax.ShapeDtypeStruct((n, MLA_Q_RANK), BF16),
                   jax.ShapeDtypeStruct((n, MLA_KV_RANK), F32),
                   jax.ShapeDtypeStruct((n, MLA_ROPE), F32),
                   jax.ShapeDtypeStruct((n, LANES), BF16)],
        compiler_params=_params("parallel"),
        name="mla_down_proj",
    )(x, cs, w, gq, gkv)


MLA_QW = MLA_NOPE + 2 * MLA_ROPE


def _mla_q_body(cq_ref, cs_ref, w_ref, q_ref):
    cs = cs_ref[...]
    for h in range(MLA_HEADS):
        c0 = h * MLA_QW
        qh = _dot(cq_ref[...], w_ref[:, c0:c0 + MLA_QW])
        q_ref[:, c0:c0 + MLA_NOPE] = qh[:, :MLA_NOPE].astype(BF16)
        q_ref[:, c0 + MLA_NOPE:c0 + MLA_QW] = _rope_pair(qh[:, MLA_NOPE:], cs).astype(BF16)


def _mla_q_call(cq, cs, w, tm, tiles_per_seq):
    n = cq.shape[0]
    return pl.pallas_call(
        _mla_q_body,
        grid=(n // tm,),
        in_specs=[_row_spec(tm, MLA_Q_RANK),
                  pl.BlockSpec((tm, LANES), lambda i: (i % tiles_per_seq, 0)),
                  _const_spec(w.shape)],
        out_specs=_row_spec(tm, MLA_HEADS * MLA_QW),
        out_shape=jax.ShapeDtypeStruct((n, MLA_HEADS * MLA_QW), BF16),
        compiler_params=_params("parallel"),
        name="mla_q_proj",
    )(cq, cs, w)


def _mla_kv_up_body(ckv_ref, kpad_ref, wuk_ref, wuv_ref, k_ref, v_ref):
    ckv = ckv_ref[...].astype(BF16)
    v_ref[...] = _dot(ckv, wuv_ref[...]).astype(BF16)
    for h in range(MLA_HEADS):
        c0 = h * MLA_QW
        k_ref[:, c0:c0 + MLA_NOPE] = _dot(ckv, wuk_ref[:, h * MLA_NOPE:(h + 1) * MLA_NOPE]).astype(BF16)
        k_ref[:, c0 + MLA_NOPE:c0 + MLA_QW] = kpad_ref[...]


def _mla_kv_up_call(ckv, kpad, wuk, wuv, tm):
    n = ckv.shape[0]
    return pl.pallas_call(
        _mla_kv_up_body,
        grid=(n // tm,),
        in_specs=[_row_spec(tm, MLA_KV_RANK), _row_spec(tm, LANES),
                  _const_spec(wuk.shape), _const_spec(wuv.shape)],
        out_specs=[_row_spec(tm, MLA_HEADS * MLA_QW), _row_spec(tm, MLA_HEADS * MLA_V)],
        out_shape=[jax.ShapeDtypeStruct((n, MLA_HEADS * MLA_QW), BF16),
                   jax.ShapeDtypeStruct((n, MLA_HEADS * MLA_V), BF16)],
        compiler_params=_params("parallel"),
        name="mla_kv_up_proj",
    )(ckv, kpad, wuk, wuv)


def _softmax_init(m_scr, l_scr, acc_scr):
    m_scr[...] = jnp.full(m_scr.shape, NEG_INF, F32)
    l_scr[...] = jnp.zeros(l_scr.shape, F32)
    acc_scr[...] = jnp.zeros(acc_scr.shape, F32)


MLA_EXP2_SCALE = MLA_SCALE * math.log2(math.e)


def _flash_update(s, v, m_ref, l_ref, acc_ref):
    m_prev = m_ref[...]
    m_new = jnp.maximum(m_prev, jnp.max(s, axis=-1, keepdims=True))
    corr = jnp.exp2((m_prev - m_new) * MLA_EXP2_SCALE)
    p = jnp.exp2(s * MLA_EXP2_SCALE - m_new * MLA_EXP2_SCALE)
    l_ref[...] = l_ref[...] * corr + jnp.sum(p, axis=-1, keepdims=True)
    acc_ref[...] = acc_ref[...] * corr + _dot(p.astype(BF16), v)
    m_ref[...] = m_new


def _mla_flash_body(q_ref, k_ref, v_ref, o_ref, s_scr, m_scr, l_scr, acc_scr, *, t, heads):
    qi = pl.program_id(2)

    def scores(hh, ki):
        k = k_ref[pl.ds(pl.multiple_of(ki * t, t), t), hh * MLA_QW:(hh + 1) * MLA_QW]
        return _dot_nt(q_ref[:, hh * MLA_QW:(hh + 1) * MLA_QW], k)

    def values(hh, ki):
        return v_ref[pl.ds(pl.multiple_of(ki * t, t), t), hh * MLA_V:(hh + 1) * MLA_V]

    def advance(ki, slot):
        for hh in range(heads):
            s_scr[hh, 1 - slot] = scores(hh, ki + 1)
        for hh in range(heads):
            _flash_update(s_scr[hh, slot], values(hh, ki), m_scr.at[hh], l_scr.at[hh],
                          acc_scr.at[hh])

    def finish(slot):
        row = lax.broadcasted_iota(jnp.int32, (t, t), 0)
        col = lax.broadcasted_iota(jnp.int32, (t, t), 1)
        for hh in range(heads):
            s = jnp.where(col <= row, s_scr[hh, slot], NEG_INF)
            _flash_update(s, values(hh, qi), m_scr.at[hh], l_scr.at[hh], acc_scr.at[hh])
            o_ref[:, hh * MLA_V:(hh + 1) * MLA_V] = (acc_scr[hh] / l_scr[hh]).astype(BF16)

    for hh in range(heads):
        _softmax_init(m_scr.at[hh], l_scr.at[hh], acc_scr.at[hh])
        s_scr[hh, 0] = scores(hh, 0)

    def pair_step(kk, carry):
        advance(2 * kk, 0)
        advance(2 * kk + 1, 1)
        return carry

    lax.fori_loop(0, qi // 2, pair_step, 0)

    @pl.when(qi % 2 == 0)
    def _():
        finish(0)

    @pl.when(qi % 2 == 1)
    def _():
        advance(qi - 1, 0)
        finish(1)


def _mla_flash_call(q, k, v, batch, seq):
    t, heads = FLASH_T, FLASH_HEADS
    nq = seq // t
    body = functools.partial(_mla_flash_body, t=t, heads=heads)
    return pl.pallas_call(
        body,
        grid=(batch, MLA_HEADS // heads, nq),
        in_specs=[pl.BlockSpec((t, heads * MLA_QW), lambda b, h, i: (b * nq + i, h)),
                  pl.BlockSpec((seq, heads * MLA_QW), lambda b, h, i: (b, h)),
                  pl.BlockSpec((seq, heads * MLA_V), lambda b, h, i: (b, h))],
        out_specs=pl.BlockSpec((t, heads * MLA_V), lambda b, h, i: (b * nq + i, h)),
        out_shape=jax.ShapeDtypeStruct((batch * seq, MLA_HEADS * MLA_V), BF16),
        scratch_shapes=[pltpu.VMEM((heads, 2, t, t), F32),
                        pltpu.VMEM((heads, t, 1), F32), pltpu.VMEM((heads, t, 1), F32),
                        pltpu.VMEM((heads, t, MLA_V), F32)],
        compiler_params=_params("parallel", "parallel", "arbitrary"),
        name="mla_prompt_attention",
    )(q, k, v)


MLA_LATQ = MLA_KV_RANK + 2 * MLA_ROPE


def _mla_absorb_q_body(q_ref, wuk_ref, out_ref):
    q = q_ref[...]
    out_ref[:, :MLA_KV_RANK] = _dot_nt(q[:, :MLA_NOPE], wuk_ref[...]).astype(BF16)
    out_ref[:, MLA_KV_RANK:] = q[:, MLA_NOPE:]


def _mla_absorb_q_call(q, wuk):
    n = q.shape[0]
    return pl.pallas_call(
        _mla_absorb_q_body,
        grid=(MLA_HEADS,),
        in_specs=[pl.BlockSpec((n, MLA_QW), lambda h: (0, h)),
                  pl.BlockSpec((MLA_KV_RANK, MLA_NOPE), lambda h: (0, h))],
        out_specs=pl.BlockSpec((n, MLA_LATQ), lambda h: (0, h)),
        out_shape=jax.ShapeDtypeStruct((n, MLA_HEADS * MLA_LATQ), BF16),
        compiler_params=_params("parallel"),
        name="mla_absorb_q",
    )(q, wuk)


def _mla_decode_body(pt_ref, q_ref, nck_ref, nkp_ref, ck_hbm, kp_hbm, o_ref, ck_buf, kp_buf, sem,
                     ck_scr, kp_scr, m_scr, l_scr, acc_scr, *, pages, chains, t_new, layer):
    b = pl.program_id(0)
    step = pl.program_id(1)
    n_steps = pl.num_programs(1)
    idx = b * n_steps + step
    slot = idx % 2
    per_chain = pages // chains
    span = per_chain * PAGE_SIZE

    def page_copies(page_of, sl):
        copies = []
        for g in range(pages):
            page = page_of(g)
            copies.append(pltpu.make_async_copy(ck_hbm.at[layer, page], ck_buf.at[sl, g],
                                                sem.at[0, sl]))
            copies.append(pltpu.make_async_copy(kp_hbm.at[layer, page], kp_buf.at[sl, g],
                                                sem.at[1, sl]))
        return copies

    @pl.when(idx == 0)
    def _():
        for cp in page_copies(lambda g: pt_ref[0, g], 0):
            cp.start()

    @pl.when(idx + 1 < pl.num_programs(0) * n_steps)
    def _():
        wrap = step == n_steps - 1
        nb = jnp.where(wrap, b + 1, b)
        ns = jnp.where(wrap, 0, step + 1)
        for cp in page_copies(lambda g: pt_ref[nb, ns * pages + g], 1 - slot):
            cp.start()

    for cp in page_copies(lambda g: 0, slot):
        cp.wait()

    @pl.when(step == 0)
    def _():
        _softmax_init(m_scr, l_scr, acc_scr)

    q_lat = q_ref[0, :, :MLA_KV_RANK]
    q_pe = q_ref[0, :, MLA_KV_RANK:MLA_KV_RANK + MLA_ROPE]
    scores = []
    for c in range(chains):
        for g in range(c * per_chain, (c + 1) * per_chain):
            ck_scr[g * PAGE_SIZE:(g + 1) * PAGE_SIZE, :] = ck_buf[slot, g].astype(BF16)
            kp_scr[:, g * PAGE_SIZE:(g + 1) * PAGE_SIZE] = kp_buf[slot, g].astype(BF16)
        scores.append(_dot_nt(q_lat, ck_scr[c * span:(c + 1) * span, :])
                      + _dot(q_pe, kp_scr[:, c * span:(c + 1) * span]))
    for c in range(chains):
        _flash_update(scores[c], ck_scr[c * span:(c + 1) * span, :], m_scr.at[c], l_scr.at[c],
                      acc_scr.at[c])

    @pl.when(step == pl.num_programs(1) - 1)
    def _():
        for c in range(1, chains):
            m = jnp.maximum(m_scr[0], m_scr[c])
            w0 = jnp.exp2((m_scr[0] - m) * MLA_EXP2_SCALE)
            wc = jnp.exp2((m_scr[c] - m) * MLA_EXP2_SCALE)
            l_scr[0] = l_scr[0] * w0 + l_scr[c] * wc
            acc_scr[0] = acc_scr[0] * w0 + acc_scr[c] * wc
            m_scr[0] = m
        pad = PAGE_SIZE - nck_ref.shape[1]
        nck = jnp.concatenate([nck_ref[0], jnp.zeros((pad, MLA_KV_RANK), F32)],
                              axis=0).astype(BF16)
        nkp = jnp.concatenate([nkp_ref[0], jnp.zeros((pad, MLA_ROPE), F32)],
                              axis=0).astype(BF16)
        s2 = _dot_nt(q_lat, nck) + _dot_nt(q_pe, nkp)
        tq = lax.broadcasted_iota(jnp.int32, s2.shape, 0) // MLA_HEADS
        tk = lax.broadcasted_iota(jnp.int32, s2.shape, 1)
        s2 = jnp.where((tk <= tq) & (tk < t_new), s2, NEG_INF)
        _flash_update(s2, nck, m_scr.at[0], l_scr.at[0], acc_scr.at[0])
        o_ref[0] = (acc_scr[0] / l_scr[0]).astype(BF16)


def _mla_decode_call(page_table, q, new_ckv, new_kpe, cache_ckv, cache_kpe_t, layer, t_new):
    nb, rows, _ = q.shape
    n_pages = page_table.shape[1]
    pages, chains = DECODE_PAGES, DECODE_CHAINS
    pad_rows = new_ckv.shape[1]

    body = functools.partial(_mla_decode_body, pages=pages, chains=chains, t_new=t_new,
                             layer=layer)
    grid_spec = pltpu.PrefetchScalarGridSpec(
        num_scalar_prefetch=1,
        grid=(nb, n_pages // pages),
        in_specs=[pl.BlockSpec((1, rows, MLA_LATQ), lambda b, s, pt: (b, 0, 0)),
                  pl.BlockSpec((1, pad_rows, MLA_KV_RANK), lambda b, s, pt: (b, 0, 0)),
                  pl.BlockSpec((1, pad_rows, MLA_ROPE), lambda b, s, pt: (b, 0, 0)),
                  pl.BlockSpec(memory_space=pl.ANY),
                  pl.BlockSpec(memory_space=pl.ANY)],
        out_specs=pl.BlockSpec((1, rows, MLA_KV_RANK), lambda b, s, pt: (b, 0, 0)),
        scratch_shapes=[pltpu.VMEM((2, pages, PAGE_SIZE, MLA_KV_RANK), F32),
                        pltpu.VMEM((2, pages, MLA_ROPE, PAGE_SIZE), F32),
                        pltpu.SemaphoreType.DMA((2, 2)),
                        pltpu.VMEM((pages * PAGE_SIZE, MLA_KV_RANK), BF16),
                        pltpu.VMEM((MLA_ROPE, pages * PAGE_SIZE), BF16),
                        pltpu.VMEM((chains, rows, 1), F32), pltpu.VMEM((chains, rows, 1), F32),
                        pltpu.VMEM((chains, rows, MLA_KV_RANK), F32)],
    )
    return pl.pallas_call(
        body,
        grid_spec=grid_spec,
        out_shape=jax.ShapeDtypeStruct((nb, rows, MLA_KV_RANK), BF16),
        compiler_params=_params("arbitrary", "arbitrary"),
        name="mla_paged_decode",
    )(page_table, q, new_ckv, new_kpe, cache_ckv, cache_kpe_t)


def _mla_absorb_o_body(o_ref, wuv_ref, out_ref):
    out_ref[...] = _dot(o_ref[0], wuv_ref[...]).astype(BF16)


def _mla_absorb_o_call(o_lat, wuv):
    _, n, _ = o_lat.shape
    return pl.pallas_call(
        _mla_absorb_o_body,
        grid=(MLA_HEADS,),
        in_specs=[pl.BlockSpec((1, n, MLA_KV_RANK), lambda h: (h, 0, 0)),
                  pl.BlockSpec((MLA_KV_RANK, MLA_V), lambda h: (0, h))],
        out_specs=pl.BlockSpec((n, MLA_V), lambda h: (0, h)),
        out_shape=jax.ShapeDtypeStruct((n, MLA_HEADS * MLA_V), BF16),
        compiler_params=_params("parallel"),
        name="mla_absorb_o",
    )(o_lat, wuv)


SWA_NQ = SWA_HEADS * SWA_HEAD_DIM
SWA_NKV = SWA_KV_HEADS * SWA_HEAD_DIM
SWA_PAIRS = SWA_KV_HEADS // 2


def _swa_regroup(a, axis):
    shape = a.shape
    split = shape[:axis] + (SWA_PAIRS, 2, SWA_GROUP, SWA_HEAD_DIM) + shape[axis + 1:]
    return jnp.swapaxes(a.reshape(split), axis + 1, axis + 2).reshape(shape)


def _swa_slope(head):
    return 2.0 ** (-8.0 * (head + 1) / SWA_HEADS)


def _swa_qkv_body(x_ref, w_ref, b_ref, q_ref, k_ref, v_ref):
    qkv = _dot(x_ref[...].astype(BF16), w_ref[...]) + b_ref[...]
    q_ref[...] = qkv[:, :SWA_NQ].astype(BF16)
    k_ref[...] = qkv[:, SWA_NQ:SWA_NQ + SWA_NKV]
    v_ref[...] = qkv[:, SWA_NQ + SWA_NKV:]


def _swa_qkv_call(x, w, b, tm):
    n = x.shape[0]
    return pl.pallas_call(
        _swa_qkv_body,
        grid=(n // tm,),
        in_specs=[_row_spec(tm, D_MODEL), _const_spec(w.shape), _const_spec(b.shape)],
        out_specs=[_row_spec(tm, SWA_NQ), _row_spec(tm, SWA_NKV), _row_spec(tm, SWA_NKV)],
        out_shape=[jax.ShapeDtypeStruct((n, SWA_NQ), BF16),
                   jax.ShapeDtypeStruct((n, SWA_NKV), F32),
                   jax.ShapeDtypeStruct((n, SWA_NKV), F32)],
        compiler_params=_params("parallel"),
        name="swa_qkv_proj",
    )(x, w, b)


def _swa_heads(q_blk, k_pair, v_pair, sink_ref, p, g, dist, valid):
    lane_q = lax.broadcasted_iota(jnp.int32, q_blk.shape, 1)
    outs = []
    for half in range(2):
        head = (2 * p + half) * SWA_GROUP + g
        in_half = (lane_q >= SWA_HEAD_DIM) if half else (lane_q < SWA_HEAD_DIM)
        qm = jnp.where(in_half, q_blk, jnp.zeros_like(q_blk))
        s = _dot_nt(qm, k_pair) * SWA_SCALE - _swa_slope(head) * dist
        s = jnp.where(valid, s, NEG_INF)
        sink = sink_ref[head]
        m = jnp.maximum(jnp.max(s, axis=-1, keepdims=True), sink)
        e = jnp.exp(s - m)
        denom = jnp.sum(e, axis=-1, keepdims=True) + jnp.exp(sink - m)
        outs.append(_dot((e / denom).astype(BF16), v_pair))
    lane_o = lax.broadcasted_iota(jnp.int32, outs[0].shape, 1)
    return jnp.where(lane_o < SWA_HEAD_DIM, outs[0], outs[1])


def _swa_prompt_body(sink_ref, q_ref, kc_ref, kp_ref, vc_ref, vp_ref, o_ref):
    i = pl.program_id(1)
    k = jnp.concatenate([kp_ref[...].astype(BF16), kc_ref[...].astype(BF16)], axis=0)
    v = jnp.concatenate([vp_ref[...].astype(BF16), vc_ref[...].astype(BF16)], axis=0)
    row = lax.broadcasted_iota(jnp.int32, (WINDOW, 2 * WINDOW), 0)
    col = lax.broadcasted_iota(jnp.int32, (WINDOW, 2 * WINDOW), 1)
    dist = row + WINDOW - col
    valid = (dist >= 0) & (dist < WINDOW) & ((col >= WINDOW) | (i > 0))
    distf = dist.astype(F32)
    for p in range(SWA_PAIRS):
        kpair = k[:, p * LANES:(p + 1) * LANES]
        vpair = v[:, p * LANES:(p + 1) * LANES]
        for g in range(SWA_GROUP):
            c0 = (p * SWA_GROUP + g) * LANES
            o = _swa_heads(q_ref[:, c0:c0 + LANES], kpair, vpair, sink_ref, p, g, distf, valid)
            o_ref[:, c0:c0 + LANES] = o.astype(BF16)


def _swa_prompt_call(sinks, q, k, v, batch, seq):
    nb = seq // WINDOW
    cur = lambda b, i: (b * nb + i, 0)
    prev = lambda b, i: (b * nb + jnp.maximum(i - 1, 0), 0)
    return pl.pallas_call(
        _swa_prompt_body,
        grid=(batch, nb),
        in_specs=[pl.BlockSpec(memory_space=pltpu.SMEM),
                  pl.BlockSpec((WINDOW, SWA_NQ), cur),
                  pl.BlockSpec((WINDOW, SWA_NKV), cur), pl.BlockSpec((WINDOW, SWA_NKV), prev),
                  pl.BlockSpec((WINDOW, SWA_NKV), cur), pl.BlockSpec((WINDOW, SWA_NKV), prev)],
        out_specs=pl.BlockSpec((WINDOW, SWA_NQ), cur),
        out_shape=jax.ShapeDtypeStruct((batch * seq, SWA_NQ), BF16),
        compiler_params=_params("parallel", "arbitrary"),
        name="swa_prompt_attention",
    )(sinks, q, k, k, v, v)


def _swa_sample_body(sink_ref, q_ref, kc_ref, vc_ref, kn_ref, vn_ref, o_ref, *, bt, t_pad):
    n_keys = 2 * WINDOW
    fill = jnp.zeros((WINDOW - t_pad, LANES), F32)
    rows = 2 * SWA_GROUP * t_pad
    ridx = lax.broadcasted_iota(jnp.int32, (rows, 1), 0) // t_pad
    tq = lax.broadcasted_iota(jnp.int32, (rows, n_keys), 0) % t_pad
    col = lax.broadcasted_iota(jnp.int32, (rows, n_keys), 1)
    dist = tq + WINDOW - col
    valid = (dist >= 0) & (dist < WINDOW)
    distf = dist.astype(F32)
    lane = lax.broadcasted_iota(jnp.int32, (t_pad, LANES), 1)
    bias, sink_cols = [], []
    for p in range(SWA_PAIRS):
        slope = jnp.zeros((rows, 1), F32)
        sink = jnp.zeros((rows, 1), F32)
        for half in range(2):
            for g in range(SWA_GROUP):
                head = (2 * p + half) * SWA_GROUP + g
                here = ridx == half * SWA_GROUP + g
                slope = jnp.where(here, _swa_slope(head), slope)
                sink = jnp.where(here, sink_ref[head], sink)
        bias.append(slope * distf)
        sink_cols.append(sink)

    def one_seq(bb, carry):
        scores = []
        for p in range(SWA_PAIRS):
            pair = slice(p * LANES, (p + 1) * LANES)
            k_old = kc_ref[bb, pair, :].astype(BF16)
            k_new = jnp.concatenate([kn_ref[bb, :, pair], fill], axis=0).astype(BF16)
            blocks = [q_ref[bb, :, (p * SWA_GROUP + g) * LANES:(p * SWA_GROUP + g + 1) * LANES]
                      .astype(F32) for g in range(SWA_GROUP)]
            stack = ([jnp.where(lane < SWA_HEAD_DIM, blk, 0.0) for blk in blocks]
                     + [jnp.where(lane >= SWA_HEAD_DIM, blk, 0.0) for blk in blocks])
            qs = jnp.concatenate(stack, axis=0).astype(BF16)
            scores.append(jnp.concatenate([_dot(qs, k_old), _dot_nt(qs, k_new)], axis=1))
        probs = []
        for p in range(SWA_PAIRS):
            s = scores[p] * SWA_SCALE - bias[p]
            s = jnp.where(valid, s, NEG_INF)
            m = jnp.maximum(jnp.max(s, axis=-1, keepdims=True), sink_cols[p])
            e = jnp.exp(s - m)
            denom = jnp.sum(e, axis=-1, keepdims=True) + jnp.exp(sink_cols[p] - m)
            probs.append((e / denom).astype(BF16))
        for p in range(SWA_PAIRS):
            pair = slice(p * LANES, (p + 1) * LANES)
            v_old = vc_ref[bb, pair, :].astype(BF16)
            v_new = jnp.concatenate([vn_ref[bb, :, pair], fill], axis=0).astype(BF16)
            pr = probs[p]
            o = _dot_nt(pr[:, :WINDOW], v_old) + _dot(pr[:, WINDOW:], v_new)
            for g in range(SWA_GROUP):
                lo = o[g * t_pad:(g + 1) * t_pad]
                hi = o[(SWA_GROUP + g) * t_pad:(SWA_GROUP + g + 1) * t_pad]
                c0 = (p * SWA_GROUP + g) * LANES
                o_ref[bb, :, c0:c0 + LANES] = jnp.where(lane < SWA_HEAD_DIM, lo, hi).astype(BF16)
        return carry

    lax.fori_loop(0, bt, one_seq, 0)


def _swa_sample_call(sinks, q, kc, vc, kn, vn):
    nb, t_pad, _ = q.shape
    bt = SWA_SEQ_TILE
    body = functools.partial(_swa_sample_body, bt=bt, t_pad=t_pad)
    seq3 = lambda rows, cols: pl.BlockSpec((bt, rows, cols), lambda i: (i, 0, 0))
    return pl.pallas_call(
        body,
        grid=(nb // bt,),
        in_specs=[pl.BlockSpec(memory_space=pltpu.SMEM),
                  seq3(t_pad, SWA_NQ), seq3(SWA_NKV, WINDOW), seq3(SWA_NKV, WINDOW),
                  seq3(t_pad, SWA_NKV), seq3(t_pad, SWA_NKV)],
        out_specs=seq3(t_pad, SWA_NQ),
        out_shape=jax.ShapeDtypeStruct((nb, t_pad, SWA_NQ), BF16),
        compiler_params=_params("parallel"),
        name="swa_sample_attention",
    )(sinks, q, kc, vc, kn, vn)


def _rot_cols(w):
    half = w.shape[-1] // 2
    return jnp.concatenate([-w[..., half:], w[..., :half]], axis=-1)


def _rope_table(pos):
    half = MLA_ROPE // 2
    inv = ROPE_THETA ** (-jnp.arange(half, dtype=F32) / half)
    ang = pos.astype(F32)[:, None] * inv[None, :]
    cos, sin = jnp.cos(ang), jnp.sin(ang)
    return jnp.concatenate([cos, cos, sin, sin], axis=-1)


def _time_major(a, nb, t):
    return jnp.swapaxes(a, 0, 1).reshape((t * nb,) + a.shape[2:])


def _batch_major(a, nb, t):
    return jnp.swapaxes(a.reshape((t, nb) + a.shape[1:]), 0, 1)


def kernel(x_prompt, x_sample, state_pool, cache_mla_ckv, cache_mla_kpe, page_table, cache_swa_k, cache_swa_v, state_ffn_conv, ln_g, ln_b, pool_w, pool_scale, mla_w_a, mla_g_q, mla_g_kv, mla_w_uq, mla_w_uk, mla_w_uv, mla_w_o, swa_w_qkv, swa_b_qkv, swa_sinks, swa_w_o, swa_b_o, ffn_w_in, ffn_conv_w, ffn_conv_b, ffn_w_out):
    B, S, D = x_prompt.shape
    NB, T, _ = x_sample.shape
    NS = NB * T
    tm = ROW_TILE
    tps = S // tm
    assert S % tm == 0 and NS == tm and S % FLASH_T == 0 and T <= SUBLANES

    xp = x_prompt.reshape(B * S, D)
    xs = _time_major(x_sample, NB, T)
    cs_p = _rope_table(jnp.arange(S))
    cs_s = jnp.repeat(_rope_table(PAST_LEN + jnp.arange(T)), NB, axis=0)
    zero_bias = jnp.zeros((1, D), F32)
    ffn_conv = (ffn_conv_w, ffn_conv_b.reshape(DEPTH, 1, D_FF))
    ln_rows = (ln_g.reshape(2 * DEPTH, 1, D), ln_b.reshape(2 * DEPTH, 1, D))

    pool_p, pool_s, ckv_p, ckv_s, kpe_p, kpe_s = [], [], [], [], [], []
    swk_p, swk_s, swv_p, swv_s, conv_p, conv_s = [], [], [], [], [], []

    for i in range(DEPTH):
        kind, j = i % N_MIXERS, i // N_MIXERS
        g1, b1 = ln_g[i, 0][None, :], ln_b[i, 0][None, :]
        if kind == 0:
            w = pool_w[j].astype(BF16)
            sc = pool_scale[j][None, :]
            pool_p.append(xp.reshape(B, S, D)[:, S - POOL_BUF:])
            prev_tm = jnp.swapaxes(state_pool[j], 0, 1)
            xs_tm = xs.reshape(T, NB, D)
            pool_s.append(jnp.swapaxes(jnp.concatenate([prev_tm[T:], xs_tm], axis=0), 0, 1))
            xp = _pool_prompt_call(xp, w, sc, g1, b1, tm, tps)
            xs = _pool_sample_call(prev_tm, xs_tm, w, sc, g1, b1, 32).reshape(NS, D)
        elif kind == 1:
            w_a = mla_w_a[j]
            n_lat = MLA_Q_RANK + MLA_KV_RANK
            w_a_ext = jnp.concatenate([w_a, _rot_cols(w_a[:, n_lat:])], axis=1).astype(BF16)
            w_uq = mla_w_uq[j]
            w_uq_ext = jnp.concatenate([w_uq, _rot_cols(w_uq[..., MLA_NOPE:])], axis=-1)
            w_uq_ext = w_uq_ext.reshape(MLA_Q_RANK, MLA_HEADS * MLA_QW).astype(BF16)
            w_uk = mla_w_uk[j].reshape(MLA_KV_RANK, MLA_HEADS * MLA_NOPE).astype(BF16)
            w_uv = mla_w_uv[j].reshape(MLA_KV_RANK, MLA_HEADS * MLA_V).astype(BF16)
            w_o = mla_w_o[j].reshape(MLA_HEADS * MLA_V, D).astype(BF16)
            gq, gkv = mla_g_q[j][None, :], mla_g_kv[j][None, :]

            cq, ckv, kpe, kpad = _mla_a_call(xp, cs_p, w_a_ext, gq, gkv, tm, tps)
            q = _mla_q_call(cq, cs_p, w_uq_ext, tm, tps)
            k_full, v_full = _mla_kv_up_call(ckv, kpad, w_uk, w_uv, tm)
            o = _mla_flash_call(q, k_full, v_full, B, S)
            ckv_p.append(ckv.reshape(B, S, MLA_KV_RANK))
            kpe_p.append(kpe.reshape(B, S, MLA_ROPE))
            xp = _proj_ln_call(o, xp, w_o, zero_bias, g1, b1, tm)

            cq, ckv, kpe, _ = _mla_a_call(xs, cs_s, w_a_ext, gq, gkv, tm, 1)
            q = _mla_q_call(cq, cs_s, w_uq_ext, tm, 1)
            q_abs = _mla_absorb_q_call(q, w_uk)
            q_abs = _batch_major(q_abs.reshape(NS, MLA_HEADS, MLA_LATQ), NB, T)
            q_abs = q_abs.reshape(NB, T * MLA_HEADS, MLA_LATQ)
            ckv_bm = _batch_major(ckv, NB, T)
            kpe_bm = _batch_major(kpe, NB, T)
            pad = ((0, 0), (0, SUBLANES - T), (0, 0))
            o_lat = _mla_decode_call(page_table, q_abs, jnp.pad(ckv_bm, pad), jnp.pad(kpe_bm, pad),
                                     cache_mla_ckv, jnp.swapaxes(cache_mla_kpe, 2, 3), j, T)
            o_lat = o_lat.reshape(NB, T, MLA_HEADS, MLA_KV_RANK).transpose(2, 1, 0, 3)
            o = _mla_absorb_o_call(o_lat.reshape(MLA_HEADS, NS, MLA_KV_RANK), w_uv)
            ckv_s.append(ckv_bm)
            kpe_s.append(kpe_bm)
            xs = _proj_ln_call(o, xs, w_o, zero_bias, g1, b1, tm)
        else:
            w_qkv, b_qkv = swa_w_qkv[j], swa_b_qkv[j]
            w_qkv = jnp.concatenate([_swa_regroup(w_qkv[:, :SWA_NQ], 1), w_qkv[:, SWA_NQ:]],
                                    axis=1)
            b_qkv = jnp.concatenate([_swa_regroup(b_qkv[:SWA_NQ], 0), b_qkv[SWA_NQ:]])[None, :]
            w_qkv = w_qkv.astype(BF16)
            w_o = _swa_regroup(swa_w_o[j], 0).astype(BF16)
            b_o = swa_b_o[j][None, :]
            sinks = swa_sinks[j]

            q, k, v = _swa_qkv_call(xp, w_qkv, b_qkv, tm // 2)
            o = _swa_prompt_call(sinks, q, k, v, B, S)
            kv_shape = (B, WINDOW, SWA_KV_HEADS, SWA_HEAD_DIM)
            swk_p.append(k.reshape(B, S, SWA_NKV)[:, S - WINDOW:].reshape(kv_shape))
            swv_p.append(v.reshape(B, S, SWA_NKV)[:, S - WINDOW:].reshape(kv_shape))
            xp = _proj_ln_call(o, xp, w_o, b_o, g1, b1, tm)

            q, k, v = _swa_qkv_call(xs, w_qkv, b_qkv, tm // 2)
            pad = ((0, 0), (0, SUBLANES - T), (0, 0))
            q_bm = jnp.pad(_batch_major(q, NB, T), pad)
            k_bm, v_bm = _batch_major(k, NB, T), _batch_major(v, NB, T)
            kc = jnp.transpose(cache_swa_k[j], (0, 2, 3, 1)).reshape(NB, SWA_NKV, WINDOW)
            vc = jnp.transpose(cache_swa_v[j], (0, 2, 3, 1)).reshape(NB, SWA_NKV, WINDOW)
            o = _swa_sample_call(sinks, q_bm, kc, vc, jnp.pad(k_bm, pad), jnp.pad(v_bm, pad))
            o = _time_major(o[:, :T], NB, T)

            def updated(cache_t, new):
                ext = jnp.concatenate([cache_t[:, :, T:], jnp.swapaxes(new, 1, 2)], axis=2)
                ext = ext.reshape(NB, SWA_KV_HEADS, SWA_HEAD_DIM, WINDOW)
                return jnp.transpose(ext, (0, 3, 1, 2))

            swk_s.append(updated(kc, k_bm))
            swv_s.append(updated(vc, v_bm))
            xs = _proj_ln_call(o, xs, w_o, b_o, g1, b1, tm)

        prev = _time_major(state_ffn_conv[i], NB, CONV_W - 1)
        xs, tails, w_bf16 = _ffn_call(xs, prev, (ffn_w_in, ffn_w_out), ffn_conv, ln_rows, i,
                                      tm=tm, tf=FFN_SAMPLE_TF, shift=NB, tiles_per_seq=1)
        conv_s.append(_batch_major(tails, NB, CONV_W - 1))
        xp, tails, _ = _ffn_call(xp, None, w_bf16, ffn_conv, ln_rows, i, tm=tm, tf=FFN_TF,
                                 shift=1, tiles_per_seq=tps)
        conv_p.append(tails.reshape(B, tps, SUBLANES, D_FF)[:, -1, SUBLANES - (CONV_W - 1):])

    y_p = xp.reshape(B, S, D)
    y_s = _batch_major(xs, NB, T)
    return (y_p, y_s, jnp.stack(pool_p), jnp.stack(pool_s), jnp.stack(ckv_p), jnp.stack(ckv_s),
            jnp.stack(kpe_p), jnp.stack(kpe_s), jnp.stack(swk_p), jnp.stack(swk_s),
            jnp.stack(swv_p), jnp.stack(swv_s), jnp.stack(conv_p), jnp.stack(conv_s))
```

```python
import functools
import math

import jax
import jax.numpy as jnp
import numpy as np
from jax import lax
from jax.experimental import pallas as pl
from jax.experimental.pallas import tpu as pltpu

D_MODEL = 2048
DEPTH = 4
N_MIXERS = 3
PAST_LEN = 8192
PAGE_SIZE = 128

POOL_WINDOWS = (2, 4, 8, 16)
POOL_GROUPS = len(POOL_WINDOWS)
POOL_GC = D_MODEL // POOL_GROUPS
POOL_BUF = max(POOL_WINDOWS) - 1

MLA_HEADS = 16
MLA_Q_RANK = 512
MLA_KV_RANK = 512
MLA_NOPE = 128
MLA_ROPE = 64
MLA_V = 128
MLA_SCALE = 1.0 / math.sqrt(MLA_NOPE + MLA_ROPE)
ROPE_THETA = 10000.0

SWA_HEADS = 32
SWA_KV_HEADS = 8
SWA_GROUP = SWA_HEADS // SWA_KV_HEADS
SWA_HEAD_DIM = 64
SWA_SCALE = 1.0 / math.sqrt(SWA_HEAD_DIM)
WINDOW = 128

D_FF = 5632
CONV_W = 3

ALPHA = (2.0 * DEPTH) ** 0.25
LN_EPS = 1e-5
RMS_EPS = 1e-6
NEG_INF = -1e30

LANES = 128
SUBLANES = 8
BF16_ROWS = 16
VMEM_LIMIT = 56 * 1024 * 1024

ROW_TILE = 512
FFN_TF = 512
FFN_SAMPLE_TF = 256
FLASH_T = 512
FLASH_HEADS = 2
DECODE_PAGES = 32
DECODE_CHAINS = 2
SWA_SEQ_TILE = 8

F32 = jnp.float32
BF16 = jnp.bfloat16


def _dot(a, b):
    return jnp.dot(a, b, preferred_element_type=F32)


def _dot_nt(a, b):
    return lax.dot_general(a, b, (((1,), (1,)), ((), ())), preferred_element_type=F32)


def _layer_norm(y, g, b):
    mu = jnp.mean(y, axis=-1, keepdims=True)
    d = y - mu
    var = jnp.mean(d * d, axis=-1, keepdims=True)
    return d * lax.rsqrt(var + LN_EPS) * g + b


def _rms_norm(a, g):
    return a * lax.rsqrt(jnp.mean(a * a, axis=-1, keepdims=True) + RMS_EPS) * g


def _params(*sem):
    return pltpu.CompilerParams(dimension_semantics=sem, vmem_limit_bytes=VMEM_LIMIT)


def _row_spec(tm, cols):
    return pl.BlockSpec((tm, cols), lambda i: (i, 0))


def _const_spec(shape):
    nd = len(shape)
    return pl.BlockSpec(shape, lambda i: (0,) * nd)


def _ffn_body(x_ref, halo_ref, wg_ref, wv_ref, wo_ref, cw_ref, cb_ref, lng_ref, lnb_ref,
              out_ref, tail_ref, *rest, tm, halo, shift, halo_is_gate, tail_rows,
              tiles_per_seq, emit_bf16):
    i = pl.program_id(0)
    j = pl.program_id(1)
    xoff = 0 if halo_is_gate else halo
    if emit_bf16:
        wg_out, wv_out, wo_out, xb_scr, g_scr, acc_scr = rest
        wg_out[...] = wg_ref[...].astype(BF16)
        wv_out[...] = wv_ref[...].astype(BF16)
        wo_out[...] = wo_ref[...].astype(BF16)
        wg_ref, wv_ref, wo_ref = wg_out, wv_out, wo_out
    else:
        xb_scr, g_scr, acc_scr = rest

    @pl.when(j == 0)
    def _():
        if not halo_is_gate:
            keep = (i % tiles_per_seq != 0).astype(F32)
            xb_scr[0:halo, :] = (halo_ref[...] * keep).astype(BF16)
        xb_scr[xoff:xoff + tm, :] = x_ref[...].astype(BF16)
        acc_scr[...] = jnp.zeros_like(acc_scr)

    if halo_is_gate:
        g_scr[0:halo, :] = halo_ref[...]
        g_scr[halo:halo + tm, :] = _dot(xb_scr[...], wg_ref[...])
    else:
        g_scr[...] = _dot(xb_scr[...], wg_ref[...])
    val = _dot(xb_scr[xoff:xoff + tm, :], wv_ref[...])
    conv = (cb_ref[...]
            + g_scr[halo - 2 * shift:halo - 2 * shift + tm, :] * cw_ref[0:1, :]
            + g_scr[halo - shift:halo - shift + tm, :] * cw_ref[1:2, :]
            + g_scr[halo:halo + tm, :] * cw_ref[2:3, :])
    h = 0.5 * conv * (1.0 + lax.erf(conv * math.sqrt(0.5))) * val
    acc_scr[...] += _dot(h.astype(BF16), wo_ref[...])
    tail_ref[...] = g_scr[halo + tm - tail_rows:halo + tm, :]

    @pl.when(j == pl.num_programs(1) - 1)
    def _():
        y = ALPHA * x_ref[...] + acc_scr[...]
        out_ref[...] = _layer_norm(y, lng_ref[...], lnb_ref[...])


def _prev_rows_spec(tm, rows, cols, grid_rank):
    per_tile = tm // rows
    if grid_rank == 1:
        return pl.BlockSpec((rows, cols), lambda i: (jnp.maximum(i * per_tile - 1, 0), 0))
    return pl.BlockSpec((rows, cols), lambda i, j: (jnp.maximum(i * per_tile - 1, 0), 0))


def _ffn_call(x, gate_prev, weights, conv, ln, layer, *, tm, tf, shift, tiles_per_seq):
    cw, cb = conv
    lng, lnb = ln
    n = x.shape[0]
    nt = n // tm
    nf = D_FF // tf
    halo_is_gate = gate_prev is not None
    if halo_is_gate:
        halo_rows = tail_rows = gate_prev.shape[0]
        halo, halo_spec = gate_prev, pl.BlockSpec((halo_rows, tf), lambda i, j: (0, j))
        x_rows = tm
        tail_shape = (tail_rows, D_FF)
        tail_spec = pl.BlockSpec((tail_rows, tf), lambda i, j: (0, j))
        w_in, w_out = weights
        w_args = (w_in, w_in, w_out)
        w_specs = [pl.BlockSpec((None, D_MODEL, tf), lambda i, j: (layer, 0, j)),
                   pl.BlockSpec((None, D_MODEL, tf), lambda i, j: (layer, 0, j + nf)),
                   pl.BlockSpec((None, tf, D_MODEL), lambda i, j: (layer, j, 0))]
        extra_specs = [pl.BlockSpec((D_MODEL, tf), lambda i, j: (0, j)),
                       pl.BlockSpec((D_MODEL, tf), lambda i, j: (0, j)),
                       pl.BlockSpec((tf, D_MODEL), lambda i, j: (j, 0))]
        extra_shapes = [jax.ShapeDtypeStruct((D_MODEL, D_FF), BF16),
                        jax.ShapeDtypeStruct((D_MODEL, D_FF), BF16),
                        jax.ShapeDtypeStruct((D_FF, D_MODEL), BF16)]
    else:
        halo_rows, tail_rows = BF16_ROWS, SUBLANES
        halo, halo_spec = x, _prev_rows_spec(tm, halo_rows, D_MODEL, 2)
        x_rows = tm + halo_rows
        tail_shape = (nt, tail_rows, D_FF)
        tail_spec = pl.BlockSpec((None, tail_rows, tf), lambda i, j: (i, 0, j))
        w_args = weights
        w_specs = [pl.BlockSpec((D_MODEL, tf), lambda i, j: (0, j)),
                   pl.BlockSpec((D_MODEL, tf), lambda i, j: (0, j)),
                   pl.BlockSpec((tf, D_MODEL), lambda i, j: (j, 0))]
        extra_specs, extra_shapes = [], []
    body = functools.partial(_ffn_body, tm=tm, halo=halo_rows, shift=shift,
                             halo_is_gate=halo_is_gate, tail_rows=tail_rows,
                             tiles_per_seq=tiles_per_seq, emit_bf16=halo_is_gate)
    ln_row = 2 * layer + 1
    outs = pl.pallas_call(
        body,
        grid=(nt, nf),
        in_specs=[
            pl.BlockSpec((tm, D_MODEL), lambda i, j: (i, 0)),
            halo_spec,
            *w_specs,
            pl.BlockSpec((None, CONV_W, tf), lambda i, j: (layer, 0, j)),
            pl.BlockSpec((None, 1, tf), lambda i, j: (layer, 0, j)),
            pl.BlockSpec((None, 1, D_MODEL), lambda i, j: (ln_row, 0, 0)),
            pl.BlockSpec((None, 1, D_MODEL), lambda i, j: (ln_row, 0, 0)),
        ],
        out_specs=[pl.BlockSpec((tm, D_MODEL), lambda i, j: (i, 0)), tail_spec, *extra_specs],
        out_shape=[jax.ShapeDtypeStruct((n, D_MODEL), F32),
                   jax.ShapeDtypeStruct(tail_shape, F32), *extra_shapes],
        scratch_shapes=[pltpu.VMEM((x_rows, D_MODEL), BF16),
                        pltpu.VMEM((tm + halo_rows, tf), F32),
                        pltpu.VMEM((tm, D_MODEL), F32)],
        compiler_params=_params("parallel", "arbitrary"),
        name="conv_ffn_ln",
    )(x, halo, *w_args, cw, cb, lng, lnb)
    return outs[0], outs[1], tuple(outs[2:])


def _proj_ln_body(a_ref, x_ref, w_ref, bias_ref, lng_ref, lnb_ref, out_ref):
    h = _dot(a_ref[...], w_ref[...]) + bias_ref[...]
    out_ref[...] = _layer_norm(ALPHA * x_ref[...] + h, lng_ref[...], lnb_ref[...])


def _proj_ln_call(a, x, w, bias, lng, lnb, tm):
    n, k = a.shape
    return pl.pallas_call(
        _proj_ln_body,
        grid=(n // tm,),
        in_specs=[_row_spec(tm, k), _row_spec(tm, D_MODEL), _const_spec(w.shape),
                  _const_spec((1, D_MODEL)), _const_spec((1, D_MODEL)), _const_spec((1, D_MODEL))],
        out_specs=_row_spec(tm, D_MODEL),
        out_shape=jax.ShapeDtypeStruct((n, D_MODEL), F32),
        compiler_params=_params("parallel"),
        name="proj_res_ln",
    )(a, x, w, bias, lng, lnb)


POOL_HALO = 4 * SUBLANES


def _pool_prompt_body(x_ref, halo_ref, w_ref, scale_ref, lng_ref, lnb_ref, out_ref, ext_scr,
                      tmp_scr, *, tm, tiles_per_seq):
    i = pl.program_id(0)
    hb = POOL_HALO
    end = hb + tm
    ext_scr[0:hb, :] = halo_ref[...] * (i % tiles_per_seq != 0).astype(F32)
    ext_scr[hb:end, :] = x_ref[...]
    pos = (i % tiles_per_seq) * tm + lax.broadcasted_iota(jnp.int32, (tm, 1), 0)
    for g, wnd in enumerate(POOL_WINDOWS):
        c0 = g * POOL_GC
        stages = wnd.bit_length() - 1
        src, cols = ext_scr, slice(c0, c0 + POOL_GC)
        for k in range(1, stages + 1):
            w = 2 ** (k - 1)
            start = hb if k == stages else SUBLANES * k
            win = src[start:end, cols] + src[start - w:end - w, cols]
            if k < stages:
                dst = tmp_scr.at[k % 2]
                dst[start:end, :] = win
                src, cols = dst, slice(0, POOL_GC)
        xg = x_ref[:, c0:c0 + POOL_GC]
        inv_cnt = 1.0 / jnp.minimum(pos + 1, wnd).astype(F32)
        pooled = win * inv_cnt - xg
        y = _dot(pooled.astype(BF16), w_ref[g]) * scale_ref[:, c0:c0 + POOL_GC]
        ext_scr[hb:end, c0:c0 + POOL_GC] = ALPHA * xg + y
    out_ref[...] = _layer_norm(ext_scr[hb:end, :], lng_ref[...], lnb_ref[...])


def _pool_prompt_call(x, w, scale, lng, lnb, tm, tiles_per_seq):
    n = x.shape[0]
    hb = POOL_HALO
    body = functools.partial(_pool_prompt_body, tm=tm, tiles_per_seq=tiles_per_seq)
    return pl.pallas_call(
        body,
        grid=(n // tm,),
        in_specs=[_row_spec(tm, D_MODEL),
                  _prev_rows_spec(tm, hb, D_MODEL, 1),
                  _const_spec(w.shape), _const_spec((1, D_MODEL)),
                  _const_spec((1, D_MODEL)), _const_spec((1, D_MODEL))],
        out_specs=_row_spec(tm, D_MODEL),
        out_shape=jax.ShapeDtypeStruct((n, D_MODEL), F32),
        scratch_shapes=[pltpu.VMEM((tm + hb, D_MODEL), F32),
                        pltpu.VMEM((2, tm + hb, POOL_GC), F32)],
        compiler_params=_params("parallel"),
        name="pool_mix_ln_prompt",
    )(x, x, w, scale, lng, lnb)


def _pool_sample_body(prev_ref, x_ref, w_ref, scale_ref, lng_ref, lnb_ref, out_ref, y_scr,
                      *, t_new, bt):
    def row(r, cols):
        return prev_ref[r, :, cols] if r < POOL_BUF else x_ref[r - POOL_BUF, :, cols]

    for g, wnd in enumerate(POOL_WINDOWS):
        cols = slice(g * POOL_GC, (g + 1) * POOL_GC)
        pooled = []
        for t in range(t_new):
            win = x_ref[t, :, cols]
            for k in range(1, wnd):
                win = win + row(POOL_BUF + t - k, cols)
            pooled.append(win / float(wnd) - x_ref[t, :, cols])
        pooled = jnp.concatenate(pooled, axis=0)
        y = _dot(pooled.astype(BF16), w_ref[g]) * scale_ref[:, cols]
        for t in range(t_new):
            y_scr[t, :, cols] = ALPHA * x_ref[t, :, cols] + y[t * bt:(t + 1) * bt]
    for t in range(t_new):
        out_ref[t] = _layer_norm(y_scr[t], lng_ref[...], lnb_ref[...])


def _pool_sample_call(prev, x, w, scale, lng, lnb, bt):
    t_new, nb, _ = x.shape
    body = functools.partial(_pool_sample_body, t_new=t_new, bt=bt)
    return pl.pallas_call(
        body,
        grid=(nb // bt,),
        in_specs=[pl.BlockSpec((POOL_BUF, bt, D_MODEL), lambda i: (0, i, 0)),
                  pl.BlockSpec((t_new, bt, D_MODEL), lambda i: (0, i, 0)),
                  _const_spec(w.shape), _const_spec((1, D_MODEL)),
                  _const_spec((1, D_MODEL)), _const_spec((1, D_MODEL))],
        out_specs=pl.BlockSpec((t_new, bt, D_MODEL), lambda i: (0, i, 0)),
        out_shape=jax.ShapeDtypeStruct((t_new, nb, D_MODEL), F32),
        scratch_shapes=[pltpu.VMEM((t_new, bt, D_MODEL), F32)],
        compiler_params=_params("parallel"),
        name="pool_mix_ln_sample",
    )(prev, x, w, scale, lng, lnb)


def _rope_pair(pair, cs):
    t = pair * cs
    return t + pltpu.roll(t, MLA_ROPE, 1)


def _mla_a_body(x_ref, cs_ref, w_ref, gq_ref, gkv_ref, cq_ref, ckv_ref, kpe_ref, kpad_ref):
    a = _dot(x_ref[...].astype(BF16), w_ref[...])
    cq_ref[...] = _rms_norm(a[:, :MLA_Q_RANK], gq_ref[...]).astype(BF16)
    ckv_ref[...] = _rms_norm(a[:, MLA_Q_RANK:MLA_Q_RANK + MLA_KV_RANK], gkv_ref[...])
    r = _rope_pair(a[:, MLA_Q_RANK + MLA_KV_RANK:], cs_ref[...])
    kpe_ref[...] = r[:, :MLA_ROPE]
    lane = lax.broadcasted_iota(jnp.int32, r.shape, 1)
    kpad_ref[...] = jnp.where(lane < MLA_ROPE, r, 0.0).astype(BF16)


def _mla_a_call(x, cs, w, gq, gkv, tm, tiles_per_seq):
    n = x.shape[0]
    return pl.pallas_call(
        _mla_a_body,
        grid=(n // tm,),
        in_specs=[_row_spec(tm, D_MODEL),
                  pl.BlockSpec((tm, LANES), lambda i: (i % tiles_per_seq, 0)),
                  _const_spec(w.shape), _const_spec((1, MLA_Q_RANK)),
                  _const_spec((1, MLA_KV_RANK))],
        out_specs=[_row_spec(tm, MLA_Q_RANK), _row_spec(tm, MLA_KV_RANK),
                   _row_spec(tm, MLA_ROPE), _row_spec(tm, LANES)],
        out_shape=[jax.ShapeDtypeStruct((n, MLA_Q_RANK), BF16),
                   jax.ShapeDtypeStruct((n, MLA_KV_RANK), F32),
                   jax.ShapeDtypeStruct((n, MLA_ROPE), F32),
                   jax.ShapeDtypeStruct((n, LANES), BF16)],
        compiler_params=_params("parallel"),
        name="mla_down_proj",
    )(x, cs, w, gq, gkv)


MLA_QW = MLA_NOPE + 2 * MLA_ROPE


def _mla_q_body(cq_ref, cs_ref, w_ref, q_ref):
    cs = cs_ref[...]
    for h in range(MLA_HEADS):
        c0 = h * MLA_QW
        qh = _dot(cq_ref[...], w_ref[:, c0:c0 + MLA_QW])
        q_ref[:, c0:c0 + MLA_NOPE] = qh[:, :MLA_NOPE].astype(BF16)
        q_ref[:, c0 + MLA_NOPE:c0 + MLA_QW] = _rope_pair(qh[:, MLA_NOPE:], cs).astype(BF16)


def _mla_q_call(cq, cs, w, tm, tiles_per_seq):
    n = cq.shape[0]
    return pl.pallas_call(
        _mla_q_body,
        grid=(n // tm,),
        in_specs=[_row_spec(tm, MLA_Q_RANK),
                  pl.BlockSpec((tm, LANES), lambda i: (i % tiles_per_seq, 0)),
                  _const_spec(w.shape)],
        out_specs=_row_spec(tm, MLA_HEADS * MLA_QW),
        out_shape=jax.ShapeDtypeStruct((n, MLA_HEADS * MLA_QW), BF16),
        compiler_params=_params("parallel"),
        name="mla_q_proj",
    )(cq, cs, w)


def _mla_kv_up_body(ckv_ref, kpad_ref, wuk_ref, wuv_t_ref, k_ref, vt_ref):
    ckv = ckv_ref[...].astype(BF16)
    vt_ref[...] = _dot_nt(wuv_t_ref[...], ckv).astype(BF16)
    k_nope = _dot(ckv, wuk_ref[...]).astype(BF16)
    for h in range(MLA_HEADS):
        c0 = h * MLA_QW
        k_ref[:, c0:c0 + MLA_NOPE] = k_nope[:, h * MLA_NOPE:(h + 1) * MLA_NOPE]
        k_ref[:, c0 + MLA_NOPE:c0 + MLA_QW] = kpad_ref[...]


def _mla_kv_up_call(ckv, kpad, wuk, wuv_t, tm):
    n = ckv.shape[0]
    return pl.pallas_call(
        _mla_kv_up_body,
        grid=(n // tm,),
        in_specs=[_row_spec(tm, MLA_KV_RANK), _row_spec(tm, LANES),
                  _const_spec(wuk.shape), _const_spec(wuv_t.shape)],
        out_specs=[_row_spec(tm, MLA_HEADS * MLA_QW),
                   pl.BlockSpec((MLA_HEADS * MLA_V, tm), lambda i: (0, i))],
        out_shape=[jax.ShapeDtypeStruct((n, MLA_HEADS * MLA_QW), BF16),
                   jax.ShapeDtypeStruct((MLA_HEADS * MLA_V, n), BF16)],
        compiler_params=_params("parallel"),
        name="mla_kv_up_proj",
    )(ckv, kpad, wuk, wuv_t)


def _softmax_init(m_scr, l_scr, acc_scr):
    m_scr[...] = jnp.full(m_scr.shape, NEG_INF, F32)
    l_scr[...] = jnp.zeros(l_scr.shape, F32)
    acc_scr[...] = jnp.zeros(acc_scr.shape, F32)


LOG2E = math.log2(math.e)
MLA_EXP2_SCALE = MLA_SCALE * LOG2E


def _flash_update(s, v, m_ref, l_ref, acc_ref):
    m_prev = m_ref[...]
    m_new = jnp.maximum(m_prev, jnp.max(s, axis=-1, keepdims=True))
    corr = jnp.exp2((m_prev - m_new) * MLA_EXP2_SCALE)
    p = jnp.exp2(s * MLA_EXP2_SCALE - m_new * MLA_EXP2_SCALE)
    l_ref[...] = l_ref[...] * corr + jnp.sum(p, axis=-1, keepdims=True)
    acc_ref[...] = acc_ref[...] * corr + _dot(p.astype(BF16), v)
    m_ref[...] = m_new


def _flash_update_t(s_t, v_t, m_ref, l_ref, acc_ref):
    m_prev = m_ref[...]
    m_new = jnp.maximum(m_prev, jnp.max(s_t, axis=0, keepdims=True))
    corr = jnp.exp2((m_prev - m_new) * MLA_EXP2_SCALE)
    p = jnp.exp2(s_t * MLA_EXP2_SCALE - m_new * MLA_EXP2_SCALE)
    l_ref[...] = l_ref[...] * corr + jnp.sum(p, axis=0, keepdims=True)
    acc_ref[...] = acc_ref[...] * corr + _dot(v_t, p.astype(BF16))
    m_ref[...] = m_new


def _mla_flash_body(q_ref, k_ref, vt_ref, o_ref, s_scr, m_scr, l_scr, acc_scr, *, t, heads):
    qi = pl.program_id(2)

    def scores(hh, ki):
        k = k_ref[pl.ds(pl.multiple_of(ki * t, t), t), hh * MLA_QW:(hh + 1) * MLA_QW]
        return _dot_nt(k, q_ref[:, hh * MLA_QW:(hh + 1) * MLA_QW])

    def values(hh, ki):
        return vt_ref[hh * MLA_V:(hh + 1) * MLA_V, pl.ds(pl.multiple_of(ki * t, t), t)]

    def advance(ki, slot):
        for hh in range(heads):
            s_scr[hh, 1 - slot] = scores(hh, ki + 1)
        for hh in range(heads):
            _flash_update_t(s_scr[hh, slot], values(hh, ki), m_scr.at[hh], l_scr.at[hh],
                            acc_scr.at[hh])

    def finish(slot):
        key = lax.broadcasted_iota(jnp.int32, (t, t), 0)
        query = lax.broadcasted_iota(jnp.int32, (t, t), 1)
        for hh in range(heads):
            s = jnp.where(key <= query, s_scr[hh, slot], NEG_INF)
            _flash_update_t(s, values(hh, qi), m_scr.at[hh], l_scr.at[hh], acc_scr.at[hh])
            o_t = acc_scr[hh] / l_scr[hh]
            o_ref[:, hh * MLA_V:(hh + 1) * MLA_V] = o_t.T.astype(BF16)

    for hh in range(heads):
        _softmax_init(m_scr.at[hh], l_scr.at[hh], acc_scr.at[hh])
        s_scr[hh, 0] = scores(hh, 0)

    def pair_step(kk, carry):
        advance(2 * kk, 0)
        advance(2 * kk + 1, 1)
        return carry

    lax.fori_loop(0, qi // 2, pair_step, 0)

    @pl.when(qi % 2 == 0)
    def _():
        finish(0)

    @pl.when(qi % 2 == 1)
    def _():
        advance(qi - 1, 0)
        finish(1)


def _mla_flash_call(q, k, v_t, batch, seq):
    t, heads = FLASH_T, FLASH_HEADS
    nq = seq // t
    body = functools.partial(_mla_flash_body, t=t, heads=heads)
    return pl.pallas_call(
        body,
        grid=(batch, MLA_HEADS // heads, nq),
        in_specs=[pl.BlockSpec((t, heads * MLA_QW), lambda b, h, i: (b * nq + i, h)),
                  pl.BlockSpec((seq, heads * MLA_QW), lambda b, h, i: (b, h)),
                  pl.BlockSpec((heads * MLA_V, seq), lambda b, h, i: (h, b))],
        out_specs=pl.BlockSpec((t, heads * MLA_V), lambda b, h, i: (b * nq + i, h)),
        out_shape=jax.ShapeDtypeStruct((batch * seq, MLA_HEADS * MLA_V), BF16),
        scratch_shapes=[pltpu.VMEM((heads, 2, t, t), F32),
                        pltpu.VMEM((heads, 1, t), F32), pltpu.VMEM((heads, 1, t), F32),
                        pltpu.VMEM((heads, MLA_V, t), F32)],
        compiler_params=_params("parallel", "parallel", "arbitrary"),
        name="mla_prompt_attention",
    )(q, k, v_t)


MLA_LATQ = MLA_KV_RANK + 2 * MLA_ROPE


def _mla_absorb_q_body(q_ref, wuk_ref, out_ref):
    q = q_ref[...]
    out_ref[:, :MLA_KV_RANK] = _dot_nt(q[:, :MLA_NOPE], wuk_ref[...]).astype(BF16)
    out_ref[:, MLA_KV_RANK:] = q[:, MLA_NOPE:]


def _mla_absorb_q_call(q, wuk):
    n = q.shape[0]
    return pl.pallas_call(
        _mla_absorb_q_body,
        grid=(MLA_HEADS,),
        in_specs=[pl.BlockSpec((n, MLA_QW), lambda h: (0, h)),
                  pl.BlockSpec((MLA_KV_RANK, MLA_NOPE), lambda h: (0, h))],
        out_specs=pl.BlockSpec((n, MLA_LATQ), lambda h: (0, h)),
        out_shape=jax.ShapeDtypeStruct((n, MLA_HEADS * MLA_LATQ), BF16),
        compiler_params=_params("parallel"),
        name="mla_absorb_q",
    )(q, wuk)


def _mla_decode_body(pt_ref, q_ref, nck_ref, nkp_ref, ck_hbm, kp_hbm, o_ref, ck_buf, kp_buf, sem,
                     ck_scr, kp_scr, m_scr, l_scr, acc_scr, *, pages, chains, t_new, layer):
    b = pl.program_id(0)
    step = pl.program_id(1)
    n_steps = pl.num_programs(1)
    idx = b * n_steps + step
    slot = idx % 2
    per_chain = pages // chains
    span = per_chain * PAGE_SIZE

    def page_copies(page_of, sl):
        copies = []
        for g in range(pages):
            page = page_of(g)
            copies.append(pltpu.make_async_copy(ck_hbm.at[layer, page], ck_buf.at[sl, g],
                                                sem.at[0, sl]))
            copies.append(pltpu.make_async_copy(kp_hbm.at[layer, page], kp_buf.at[sl, g],
                                                sem.at[1, sl]))
        return copies

    @pl.when(idx == 0)
    def _():
        for cp in page_copies(lambda g: pt_ref[0, g], 0):
            cp.start()

    @pl.when(idx + 1 < pl.num_programs(0) * n_steps)
    def _():
        wrap = step == n_steps - 1
        nb = jnp.where(wrap, b + 1, b)
        ns = jnp.where(wrap, 0, step + 1)
        for cp in page_copies(lambda g: pt_ref[nb, ns * pages + g], 1 - slot):
            cp.start()

    for cp in page_copies(lambda g: 0, slot):
        cp.wait()

    @pl.when(step == 0)
    def _():
        _softmax_init(m_scr, l_scr, acc_scr)

    q_lat = q_ref[0, :, :MLA_KV_RANK]
    q_pe = q_ref[0, :, MLA_KV_RANK:MLA_KV_RANK + MLA_ROPE]
    scores = []
    for c in range(chains):
        for g in range(c * per_chain, (c + 1) * per_chain):
            ck_scr[g * PAGE_SIZE:(g + 1) * PAGE_SIZE, :] = ck_buf[slot, g].astype(BF16)
            kp_scr[:, g * PAGE_SIZE:(g + 1) * PAGE_SIZE] = kp_buf[slot, g].astype(BF16)
        scores.append(_dot_nt(q_lat, ck_scr[c * span:(c + 1) * span, :])
                      + _dot(q_pe, kp_scr[:, c * span:(c + 1) * span]))
    for c in range(chains):
        _flash_update(scores[c], ck_scr[c * span:(c + 1) * span, :], m_scr.at[c], l_scr.at[c],
                      acc_scr.at[c])

    @pl.when(step == pl.num_programs(1) - 1)
    def _():
        for c in range(1, chains):
            m = jnp.maximum(m_scr[0], m_scr[c])
            w0 = jnp.exp2((m_scr[0] - m) * MLA_EXP2_SCALE)
            wc = jnp.exp2((m_scr[c] - m) * MLA_EXP2_SCALE)
            l_scr[0] = l_scr[0] * w0 + l_scr[c] * wc
            acc_scr[0] = acc_scr[0] * w0 + acc_scr[c] * wc
            m_scr[0] = m
        pad = PAGE_SIZE - nck_ref.shape[1]
        nck = jnp.concatenate([nck_ref[0], jnp.zeros((pad, MLA_KV_RANK), F32)],
                              axis=0).astype(BF16)
        nkp = jnp.concatenate([nkp_ref[0], jnp.zeros((pad, MLA_ROPE), F32)],
                              axis=0).astype(BF16)
        s2 = _dot_nt(q_lat, nck) + _dot_nt(q_pe, nkp)
        tq = lax.broadcasted_iota(jnp.int32, s2.shape, 0) // MLA_HEADS
        tk = lax.broadcasted_iota(jnp.int32, s2.shape, 1)
        s2 = jnp.where((tk <= tq) & (tk < t_new), s2, NEG_INF)
        _flash_update(s2, nck, m_scr.at[0], l_scr.at[0], acc_scr.at[0])
        o_ref[0] = (acc_scr[0] / l_scr[0]).astype(BF16)


def _mla_decode_call(page_table, q, new_ckv, new_kpe, cache_ckv, cache_kpe_t, layer, t_new):
    nb, rows, _ = q.shape
    n_pages = page_table.shape[1]
    pages, chains = DECODE_PAGES, DECODE_CHAINS
    pad_rows = new_ckv.shape[1]

    body = functools.partial(_mla_decode_body, pages=pages, chains=chains, t_new=t_new,
                             layer=layer)
    grid_spec = pltpu.PrefetchScalarGridSpec(
        num_scalar_prefetch=1,
        grid=(nb, n_pages // pages),
        in_specs=[pl.BlockSpec((1, rows, MLA_LATQ), lambda b, s, pt: (b, 0, 0)),
                  pl.BlockSpec((1, pad_rows, MLA_KV_RANK), lambda b, s, pt: (b, 0, 0)),
                  pl.BlockSpec((1, pad_rows, MLA_ROPE), lambda b, s, pt: (b, 0, 0)),
                  pl.BlockSpec(memory_space=pl.ANY),
                  pl.BlockSpec(memory_space=pl.ANY)],
        out_specs=pl.BlockSpec((1, rows, MLA_KV_RANK), lambda b, s, pt: (b, 0, 0)),
        scratch_shapes=[pltpu.VMEM((2, pages, PAGE_SIZE, MLA_KV_RANK), F32),
                        pltpu.VMEM((2, pages, MLA_ROPE, PAGE_SIZE), F32),
                        pltpu.SemaphoreType.DMA((2, 2)),
                        pltpu.VMEM((pages * PAGE_SIZE, MLA_KV_RANK), BF16),
                        pltpu.VMEM((MLA_ROPE, pages * PAGE_SIZE), BF16),
                        pltpu.VMEM((chains, rows, 1), F32), pltpu.VMEM((chains, rows, 1), F32),
                        pltpu.VMEM((chains, rows, MLA_KV_RANK), F32)],
    )
    return pl.pallas_call(
        body,
        grid_spec=grid_spec,
        out_shape=jax.ShapeDtypeStruct((nb, rows, MLA_KV_RANK), BF16),
        compiler_params=_params("arbitrary", "arbitrary"),
        name="mla_paged_decode",
    )(page_table, q, new_ckv, new_kpe, cache_ckv, cache_kpe_t)


def _mla_absorb_o_body(o_ref, wuv_ref, out_ref):
    out_ref[...] = _dot(o_ref[0], wuv_ref[...]).astype(BF16)


def _mla_absorb_o_call(o_lat, wuv):
    _, n, _ = o_lat.shape
    return pl.pallas_call(
        _mla_absorb_o_body,
        grid=(MLA_HEADS,),
        in_specs=[pl.BlockSpec((1, n, MLA_KV_RANK), lambda h: (h, 0, 0)),
                  pl.BlockSpec((MLA_KV_RANK, MLA_V), lambda h: (0, h))],
        out_specs=pl.BlockSpec((n, MLA_V), lambda h: (0, h)),
        out_shape=jax.ShapeDtypeStruct((n, MLA_HEADS * MLA_V), BF16),
        compiler_params=_params("parallel"),
        name="mla_absorb_o",
    )(o_lat, wuv)


SWA_NQ = SWA_HEADS * SWA_HEAD_DIM
SWA_NKV = SWA_KV_HEADS * SWA_HEAD_DIM
SWA_PAIRS = SWA_KV_HEADS // 2


def _swa_regroup(a, axis):
    shape = a.shape
    split = shape[:axis] + (SWA_PAIRS, 2, SWA_GROUP, SWA_HEAD_DIM) + shape[axis + 1:]
    return jnp.swapaxes(a.reshape(split), axis + 1, axis + 2).reshape(shape)


def _swa_slope(head):
    return 2.0 ** (-8.0 * (head + 1) / SWA_HEADS)


def _swa_qkv_body(x_ref, w_ref, b_ref, q_ref, k_ref, v_ref):
    qkv = _dot(x_ref[...].astype(BF16), w_ref[...]) + b_ref[...]
    q_ref[...] = qkv[:, :SWA_NQ].astype(BF16)
    k_ref[...] = qkv[:, SWA_NQ:SWA_NQ + SWA_NKV]
    v_ref[...] = qkv[:, SWA_NQ + SWA_NKV:]


def _swa_qkv_call(x, w, b, tm):
    n = x.shape[0]
    return pl.pallas_call(
        _swa_qkv_body,
        grid=(n // tm,),
        in_specs=[_row_spec(tm, D_MODEL), _const_spec(w.shape), _const_spec(b.shape)],
        out_specs=[_row_spec(tm, SWA_NQ), _row_spec(tm, SWA_NKV), _row_spec(tm, SWA_NKV)],
        out_shape=[jax.ShapeDtypeStruct((n, SWA_NQ), BF16),
                   jax.ShapeDtypeStruct((n, SWA_NKV), F32),
                   jax.ShapeDtypeStruct((n, SWA_NKV), F32)],
        compiler_params=_params("parallel"),
        name="swa_qkv_proj",
    )(x, w, b)


def _swa_heads(q_blk, k_pair, v_pair, sink_ref, p, g, dist, valid):
    lane_q = lax.broadcasted_iota(jnp.int32, q_blk.shape, 1)
    outs = []
    for half in range(2):
        head = (2 * p + half) * SWA_GROUP + g
        in_half = (lane_q >= SWA_HEAD_DIM) if half else (lane_q < SWA_HEAD_DIM)
        qm = jnp.where(in_half, q_blk, jnp.zeros_like(q_blk))
        s = _dot_nt(qm, k_pair) * (SWA_SCALE * LOG2E) - (_swa_slope(head) * LOG2E) * dist
        s = jnp.where(valid, s, NEG_INF)
        sink = sink_ref[head] * LOG2E
        m = jnp.maximum(jnp.max(s, axis=-1, keepdims=True), sink)
        e = jnp.exp2(s - m)
        denom = jnp.sum(e, axis=-1, keepdims=True) + jnp.exp2(sink - m)
        outs.append(_dot(e.astype(BF16), v_pair) * (1.0 / denom))
    lane_o = lax.broadcasted_iota(jnp.int32, outs[0].shape, 1)
    return jnp.where(lane_o < SWA_HEAD_DIM, outs[0], outs[1])


def _swa_prompt_body(sink_ref, q_ref, kc_ref, kp_ref, vc_ref, vp_ref, o_ref):
    i = pl.program_id(1)
    k = jnp.concatenate([kp_ref[...].astype(BF16), kc_ref[...].astype(BF16)], axis=0)
    v = jnp.concatenate([vp_ref[...].astype(BF16), vc_ref[...].astype(BF16)], axis=0)
    row = lax.broadcasted_iota(jnp.int32, (WINDOW, 2 * WINDOW), 0)
    col = lax.broadcasted_iota(jnp.int32, (WINDOW, 2 * WINDOW), 1)
    dist = row + WINDOW - col
    valid = (dist >= 0) & (dist < WINDOW) & ((col >= WINDOW) | (i > 0))
    distf = dist.astype(F32)
    for p in range(SWA_PAIRS):
        kpair = k[:, p * LANES:(p + 1) * LANES]
        vpair = v[:, p * LANES:(p + 1) * LANES]
        for g in range(SWA_GROUP):
            c0 = (p * SWA_GROUP + g) * LANES
            o = _swa_heads(q_ref[:, c0:c0 + LANES], kpair, vpair, sink_ref, p, g, distf, valid)
            o_ref[:, c0:c0 + LANES] = o.astype(BF16)


def _swa_prompt_call(sinks, q, k, v, batch, seq):
    nb = seq // WINDOW
    cur = lambda b, i: (b * nb + i, 0)
    prev = lambda b, i: (b * nb + jnp.maximum(i - 1, 0), 0)
    return pl.pallas_call(
        _swa_prompt_body,
        grid=(batch, nb),
        in_specs=[pl.BlockSpec(memory_space=pltpu.SMEM),
                  pl.BlockSpec((WINDOW, SWA_NQ), cur),
                  pl.BlockSpec((WINDOW, SWA_NKV), cur), pl.BlockSpec((WINDOW, SWA_NKV), prev),
                  pl.BlockSpec((WINDOW, SWA_NKV), cur), pl.BlockSpec((WINDOW, SWA_NKV), prev)],
        out_specs=pl.BlockSpec((WINDOW, SWA_NQ), cur),
        out_shape=jax.ShapeDtypeStruct((batch * seq, SWA_NQ), BF16),
        compiler_params=_params("parallel", "arbitrary"),
        name="swa_prompt_attention",
    )(sinks, q, k, k, v, v)


def _swa_sample_body(sink_ref, q_ref, kc_ref, vc_ref, kn_ref, vn_ref, o_ref, *, bt, t_pad):
    n_keys = 2 * WINDOW
    fill = jnp.zeros((WINDOW - t_pad, LANES), F32)
    rows = 2 * SWA_GROUP * t_pad
    ridx = lax.broadcasted_iota(jnp.int32, (rows, 1), 0) // t_pad
    tq = lax.broadcasted_iota(jnp.int32, (rows, n_keys), 0) % t_pad
    col = lax.broadcasted_iota(jnp.int32, (rows, n_keys), 1)
    dist = tq + WINDOW - col
    valid = (dist >= 0) & (dist < WINDOW)
    distf = dist.astype(F32)
    lane = lax.broadcasted_iota(jnp.int32, (t_pad, LANES), 1)
    bias, sink_cols = [], []
    for p in range(SWA_PAIRS):
        slope = jnp.zeros((rows, 1), F32)
        sink = jnp.zeros((rows, 1), F32)
        for half in range(2):
            for g in range(SWA_GROUP):
                head = (2 * p + half) * SWA_GROUP + g
                here = ridx == half * SWA_GROUP + g
                slope = jnp.where(here, _swa_slope(head), slope)
                sink = jnp.where(here, sink_ref[head], sink)
        bias.append(slope * distf)
        sink_cols.append(sink)

    def one_seq(bb, carry):
        scores = []
        for p in range(SWA_PAIRS):
            pair = slice(p * LANES, (p + 1) * LANES)
            k_old = kc_ref[bb, pair, :].astype(BF16)
            k_new = jnp.concatenate([kn_ref[bb, :, pair], fill], axis=0).astype(BF16)
            blocks = [q_ref[bb, :, (p * SWA_GROUP + g) * LANES:(p * SWA_GROUP + g + 1) * LANES]
                      .astype(F32) for g in range(SWA_GROUP)]
            stack = ([jnp.where(lane < SWA_HEAD_DIM, blk, 0.0) for blk in blocks]
                     + [jnp.where(lane >= SWA_HEAD_DIM, blk, 0.0) for blk in blocks])
            qs = jnp.concatenate(stack, axis=0).astype(BF16)
            scores.append(jnp.concatenate([_dot(qs, k_old), _dot_nt(qs, k_new)], axis=1))
        probs = []
        for p in range(SWA_PAIRS):
            s = scores[p] * SWA_SCALE - bias[p]
            s = jnp.where(valid, s, NEG_INF)
            m = jnp.maximum(jnp.max(s, axis=-1, keepdims=True), sink_cols[p])
            e = jnp.exp(s - m)
            denom = jnp.sum(e, axis=-1, keepdims=True) + jnp.exp(sink_cols[p] - m)
            probs.append((e / denom).astype(BF16))
        for p in range(SWA_PAIRS):
            pair = slice(p * LANES, (p + 1) * LANES)
            v_old = vc_ref[bb, pair, :].astype(BF16)
            v_new = jnp.concatenate([vn_ref[bb, :, pair], fill], axis=0).astype(BF16)
            pr = probs[p]
            o = _dot_nt(pr[:, :WINDOW], v_old) + _dot(pr[:, WINDOW:], v_new)
            for g in range(SWA_GROUP):
                lo = o[g * t_pad:(g + 1) * t_pad]
                hi = o[(SWA_GROUP + g) * t_pad:(SWA_GROUP + g + 1) * t_pad]
                c0 = (p * SWA_GROUP + g) * LANES
                o_ref[bb, :, c0:c0 + LANES] = jnp.where(lane < SWA_HEAD_DIM, lo, hi).astype(BF16)
        return carry

    lax.fori_loop(0, bt, one_seq, 0)


def _swa_sample_call(sinks, q, kc, vc, kn, vn):
    nb, t_pad, _ = q.shape
    bt = SWA_SEQ_TILE
    body = functools.partial(_swa_sample_body, bt=bt, t_pad=t_pad)
    seq3 = lambda rows, cols: pl.BlockSpec((bt, rows, cols), lambda i: (i, 0, 0))
    return pl.pallas_call(
        body,
        grid=(nb // bt,),
        in_specs=[pl.BlockSpec(memory_space=pltpu.SMEM),
                  seq3(t_pad, SWA_NQ), seq3(SWA_NKV, WINDOW), seq3(SWA_NKV, WINDOW),
                  seq3(t_pad, SWA_NKV), seq3(t_pad, SWA_NKV)],
        out_specs=seq3(t_pad, SWA_NQ),
        out_shape=jax.ShapeDtypeStruct((nb, t_pad, SWA_NQ), BF16),
        compiler_params=_params("parallel"),
        name="swa_sample_attention",
    )(sinks, q, kc, vc, kn, vn)


def _rot_cols(w):
    half = w.shape[-1] // 2
    return jnp.concatenate([-w[..., half:], w[..., :half]], axis=-1)


def _rope_table(pos):
    half = MLA_ROPE // 2
    inv = ROPE_THETA ** (-jnp.arange(half, dtype=F32) / half)
    ang = pos.astype(F32)[:, None] * inv[None, :]
    cos, sin = jnp.cos(ang), jnp.sin(ang)
    return jnp.concatenate([cos, cos, sin, sin], axis=-1)


def _time_major(a, nb, t):
    return jnp.swapaxes(a, 0, 1).reshape((t * nb,) + a.shape[2:])


def _batch_major(a, nb, t):
    return jnp.swapaxes(a.reshape((t, nb) + a.shape[1:]), 0, 1)


def kernel(x_prompt, x_sample, state_pool, cache_mla_ckv, cache_mla_kpe, page_table, cache_swa_k, cache_swa_v, state_ffn_conv, ln_g, ln_b, pool_w, pool_scale, mla_w_a, mla_g_q, mla_g_kv, mla_w_uq, mla_w_uk, mla_w_uv, mla_w_o, swa_w_qkv, swa_b_qkv, swa_sinks, swa_w_o, swa_b_o, ffn_w_in, ffn_conv_w, ffn_conv_b, ffn_w_out):
    B, S, D = x_prompt.shape
    NB, T, _ = x_sample.shape
    NS = NB * T
    tm = ROW_TILE
    tps = S // tm
    assert S % tm == 0 and NS == tm and S % FLASH_T == 0 and T <= SUBLANES

    xp = x_prompt.reshape(B * S, D)
    xs = _time_major(x_sample, NB, T)
    cs_p = _rope_table(jnp.arange(S))
    cs_s = jnp.repeat(_rope_table(PAST_LEN + jnp.arange(T)), NB, axis=0)
    zero_bias = jnp.zeros((1, D), F32)
    ffn_conv = (ffn_conv_w, ffn_conv_b.reshape(DEPTH, 1, D_FF))
    ln_rows = (ln_g.reshape(2 * DEPTH, 1, D), ln_b.reshape(2 * DEPTH, 1, D))

    pool_p, pool_s, ckv_p, ckv_s, kpe_p, kpe_s = [], [], [], [], [], []
    swk_p, swk_s, swv_p, swv_s, conv_p, conv_s = [], [], [], [], [], []

    for i in range(DEPTH):
        kind, j = i % N_MIXERS, i // N_MIXERS
        g1, b1 = ln_g[i, 0][None, :], ln_b[i, 0][None, :]
        if kind == 0:
            w = pool_w[j].astype(BF16)
            sc = pool_scale[j][None, :]
            pool_p.append(xp.reshape(B, S, D)[:, S - POOL_BUF:])
            prev_tm = jnp.swapaxes(state_pool[j], 0, 1)
            xs_tm = xs.reshape(T, NB, D)
            pool_s.append(jnp.swapaxes(jnp.concatenate([prev_tm[T:], xs_tm], axis=0), 0, 1))
            xp = _pool_prompt_call(xp, w, sc, g1, b1, tm, tps)
            xs = _pool_sample_call(prev_tm, xs_tm, w, sc, g1, b1, 32).reshape(NS, D)
        elif kind == 1:
            w_a = mla_w_a[j]
            n_lat = MLA_Q_RANK + MLA_KV_RANK
            w_a_ext = jnp.concatenate([w_a, _rot_cols(w_a[:, n_lat:])], axis=1).astype(BF16)
            w_uq = mla_w_uq[j]
            w_uq_ext = jnp.concatenate([w_uq, _rot_cols(w_uq[..., MLA_NOPE:])], axis=-1)
            w_uq_ext = w_uq_ext.reshape(MLA_Q_RANK, MLA_HEADS * MLA_QW).astype(BF16)
            w_uk = mla_w_uk[j].reshape(MLA_KV_RANK, MLA_HEADS * MLA_NOPE).astype(BF16)
            w_uv = mla_w_uv[j].reshape(MLA_KV_RANK, MLA_HEADS * MLA_V).astype(BF16)
            w_o = mla_w_o[j].reshape(MLA_HEADS * MLA_V, D).astype(BF16)
            gq, gkv = mla_g_q[j][None, :], mla_g_kv[j][None, :]

            cq, ckv, kpe, kpad = _mla_a_call(xp, cs_p, w_a_ext, gq, gkv, tm, tps)
            q = _mla_q_call(cq, cs_p, w_uq_ext, tm, tps)
            k_full, v_full = _mla_kv_up_call(ckv, kpad, w_uk, w_uv.T, tm)
            o = _mla_flash_call(q, k_full, v_full, B, S)
            ckv_p.append(ckv.reshape(B, S, MLA_KV_RANK))
            kpe_p.append(kpe.reshape(B, S, MLA_ROPE))
            xp = _proj_ln_call(o, xp, w_o, zero_bias, g1, b1, tm)

            cq, ckv, kpe, _ = _mla_a_call(xs, cs_s, w_a_ext, gq, gkv, tm, 1)
            q = _mla_q_call(cq, cs_s, w_uq_ext, tm, 1)
            q_abs = _mla_absorb_q_call(q, w_uk)
            q_abs = _batch_major(q_abs.reshape(NS, MLA_HEADS, MLA_LATQ), NB, T)
            q_abs = q_abs.reshape(NB, T * MLA_HEADS, MLA_LATQ)
            ckv_bm = _batch_major(ckv, NB, T)
            kpe_bm = _batch_major(kpe, NB, T)
            pad = ((0, 0), (0, SUBLANES - T), (0, 0))
            o_lat = _mla_decode_call(page_table, q_abs, jnp.pad(ckv_bm, pad), jnp.pad(kpe_bm, pad),
                                     cache_mla_ckv, jnp.swapaxes(cache_mla_kpe, 2, 3), j, T)
            o_lat = o_lat.reshape(NB, T, MLA_HEADS, MLA_KV_RANK).transpose(2, 1, 0, 3)
            o = _mla_absorb_o_call(o_lat.reshape(MLA_HEADS, NS, MLA_KV_RANK), w_uv)
            ckv_s.append(ckv_bm)
            kpe_s.append(kpe_bm)
            xs = _proj_ln_call(o, xs, w_o, zero_bias, g1, b1, tm)
        else:
            w_qkv, b_qkv = swa_w_qkv[j], swa_b_qkv[j]
            w_qkv = jnp.concatenate([_swa_regroup(w_qkv[:, :SWA_NQ], 1), w_qkv[:, SWA_NQ:]],
                                    axis=1)
            b_qkv = jnp.concatenate([_swa_regroup(b_qkv[:SWA_NQ], 0), b_qkv[SWA_NQ:]])[None, :]
            w_qkv = w_qkv.astype(BF16)
            w_o = _swa_regroup(swa_w_o[j], 0).astype(BF16)
            b_o = swa_b_o[j][None, :]
            sinks = swa_sinks[j]

            q, k, v = _swa_qkv_call(xp, w_qkv, b_qkv, tm // 2)
            o = _swa_prompt_call(sinks, q, k, v, B, S)
            kv_shape = (B, WINDOW, SWA_KV_HEADS, SWA_HEAD_DIM)
            swk_p.append(k.reshape(B, S, SWA_NKV)[:, S - WINDOW:].reshape(kv_shape))
            swv_p.append(v.reshape(B, S, SWA_NKV)[:, S - WINDOW:].reshape(kv_shape))
            xp = _proj_ln_call(o, xp, w_o, b_o, g1, b1, tm)

            q, k, v = _swa_qkv_call(xs, w_qkv, b_qkv, tm // 2)
            pad = ((0, 0), (0, SUBLANES - T), (0, 0))
            q_bm = jnp.pad(_batch_major(q, NB, T), pad)
            k_bm, v_bm = _batch_major(k, NB, T), _batch_major(v, NB, T)
            kc = jnp.transpose(cache_swa_k[j], (0, 2, 3, 1)).reshape(NB, SWA_NKV, WINDOW)
            vc = jnp.transpose(cache_swa_v[j], (0, 2, 3, 1)).reshape(NB, SWA_NKV, WINDOW)
            o = _swa_sample_call(sinks, q_bm, kc, vc, jnp.pad(k_bm, pad), jnp.pad(v_bm, pad))
            o = _time_major(o[:, :T], NB, T)

            def updated(cache_t, new):
                ext = jnp.concatenate([cache_t[:, :, T:], jnp.swapaxes(new, 1, 2)], axis=2)
                ext = ext.reshape(NB, SWA_KV_HEADS, SWA_HEAD_DIM, WINDOW)
                return jnp.transpose(ext, (0, 3, 1, 2))

            swk_s.append(updated(kc, k_bm))
            swv_s.append(updated(vc, v_bm))
            xs = _proj_ln_call(o, xs, w_o, b_o, g1, b1, tm)

        prev = _time_major(state_ffn_conv[i], NB, CONV_W - 1)
        xs, tails, w_bf16 = _ffn_call(xs, prev, (ffn_w_in, ffn_w_out), ffn_conv, ln_rows, i,
                                      tm=tm, tf=FFN_SAMPLE_TF, shift=NB, tiles_per_seq=1)
        conv_s.append(_batch_major(tails, NB, CONV_W - 1))
        xp, tails, _ = _ffn_call(xp, None, w_bf16, ffn_conv, ln_rows, i, tm=tm, tf=FFN_TF,
                                 shift=1, tiles_per_seq=tps)
        conv_p.append(tails.reshape(B, tps, SUBLANES, D_FF)[:, -1, SUBLANES - (CONV_W - 1):])

    y_p = xp.reshape(B, S, D)
    y_s = _batch_major(xs, NB, T)
    return (y_p, y_s, jnp.stack(pool_p), jnp.stack(pool_s), jnp.stack(ckv_p), jnp.stack(ckv_s),
            jnp.stack(kpe_p), jnp.stack(kpe_s), jnp.stack(swk_p), jnp.stack(swk_s),
            jnp.stack(swv_p), jnp.stack(swv_s), jnp.stack(conv_p), jnp.stack(conv_s))
```

```python
import functools
import math

import jax
import jax.numpy as jnp
import numpy as np
from jax import lax
from jax.experimental import pallas as pl
from jax.experimental.pallas import tpu as pltpu

D_MODEL = 2048
DEPTH = 4
N_MIXERS = 3
PAST_LEN = 8192
PAGE_SIZE = 128

POOL_WINDOWS = (2, 4, 8, 16)
POOL_GROUPS = len(POOL_WINDOWS)
POOL_GC = D_MODEL // POOL_GROUPS
POOL_BUF = max(POOL_WINDOWS) - 1

MLA_HEADS = 16
MLA_Q_RANK = 512
MLA_KV_RANK = 512
MLA_NOPE = 128
MLA_ROPE = 64
MLA_V = 128
MLA_SCALE = 1.0 / math.sqrt(MLA_NOPE + MLA_ROPE)
ROPE_THETA = 10000.0

SWA_HEADS = 32
SWA_KV_HEADS = 8
SWA_GROUP = SWA_HEADS // SWA_KV_HEADS
SWA_HEAD_DIM = 64
SWA_SCALE = 1.0 / math.sqrt(SWA_HEAD_DIM)
WINDOW = 128

D_FF = 5632
CONV_W = 3

ALPHA = (2.0 * DEPTH) ** 0.25
LN_EPS = 1e-5
RMS_EPS = 1e-6
NEG_INF = -1e30

LANES = 128
SUBLANES = 8
BF16_ROWS = 16
VMEM_LIMIT = 56 * 1024 * 1024

ROW_TILE = 512
FFN_TF = 512
FFN_SAMPLE_TF = 256
FLASH_T = 512
FLASH_HEADS = 2
DECODE_PAGES = 32
DECODE_CHAINS = 2
SWA_SEQ_TILE = 8

F32 = jnp.float32
BF16 = jnp.bfloat16


def _dot(a, b):
    return jnp.dot(a, b, preferred_element_type=F32)


def _dot_nt(a, b):
    return lax.dot_general(a, b, (((1,), (1,)), ((), ())), preferred_element_type=F32)


def _layer_norm(y, g, b):
    mu = jnp.mean(y, axis=-1, keepdims=True)
    d = y - mu
    var = jnp.mean(d * d, axis=-1, keepdims=True)
    return d * lax.rsqrt(var + LN_EPS) * g + b


def _rms_norm(a, g):
    return a * lax.rsqrt(jnp.mean(a * a, axis=-1, keepdims=True) + RMS_EPS) * g


def _params(*sem):
    return pltpu.CompilerParams(dimension_semantics=sem, vmem_limit_bytes=VMEM_LIMIT)


def _row_spec(tm, cols):
    return pl.BlockSpec((tm, cols), lambda i: (i, 0))


def _const_spec(shape):
    nd = len(shape)
    return pl.BlockSpec(shape, lambda i: (0,) * nd)


def _ffn_body(x_ref, halo_ref, wg_ref, wv_ref, wo_ref, cw_ref, cb_ref, lng_ref, lnb_ref,
              out_ref, tail_ref, *rest, tm, halo, shift, halo_is_gate, tail_rows,
              tiles_per_seq, emit_bf16):
    i = pl.program_id(0)
    j = pl.program_id(1)
    xoff = 0 if halo_is_gate else halo
    if emit_bf16:
        wg_out, wv_out, wo_out, xb_scr, g_scr, acc_scr = rest
        wg_out[...] = wg_ref[...].astype(BF16)
        wv_out[...] = wv_ref[...].astype(BF16)
        wo_out[...] = wo_ref[...].astype(BF16)
        wg_ref, wv_ref, wo_ref = wg_out, wv_out, wo_out
    else:
        xb_scr, g_scr, acc_scr = rest

    @pl.when(j == 0)
    def _():
        if not halo_is_gate:
            keep = (i % tiles_per_seq != 0).astype(F32)
            xb_scr[0:halo, :] = (halo_ref[...] * keep).astype(BF16)
        xb_scr[xoff:xoff + tm, :] = x_ref[...].astype(BF16)
        acc_scr[...] = jnp.zeros_like(acc_scr)

    if halo_is_gate:
        g_scr[0:halo, :] = halo_ref[...]
        g_scr[halo:halo + tm, :] = _dot(xb_scr[...], wg_ref[...])
    else:
        g_scr[...] = _dot(xb_scr[...], wg_ref[...])
    val = _dot(xb_scr[xoff:xoff + tm, :], wv_ref[...])
    conv = (cb_ref[...]
            + g_scr[halo - 2 * shift:halo - 2 * shift + tm, :] * cw_ref[0:1, :]
            + g_scr[halo - shift:halo - shift + tm, :] * cw_ref[1:2, :]
            + g_scr[halo:halo + tm, :] * cw_ref[2:3, :])
    h = 0.5 * conv * (1.0 + lax.erf(conv * math.sqrt(0.5))) * val
    acc_scr[...] += _dot(h.astype(BF16), wo_ref[...])
    tail_ref[...] = g_scr[halo + tm - tail_rows:halo + tm, :]

    @pl.when(j == pl.num_programs(1) - 1)
    def _():
        y = ALPHA * x_ref[...] + acc_scr[...]
        out_ref[...] = _layer_norm(y, lng_ref[...], lnb_ref[...])


def _prev_rows_spec(tm, rows, cols, grid_rank):
    per_tile = tm // rows
    if grid_rank == 1:
        return pl.BlockSpec((rows, cols), lambda i: (jnp.maximum(i * per_tile - 1, 0), 0))
    return pl.BlockSpec((rows, cols), lambda i, j: (jnp.maximum(i * per_tile - 1, 0), 0))


def _ffn_call(x, gate_prev, weights, conv, ln, layer, *, tm, tf, shift, tiles_per_seq):
    cw, cb = conv
    lng, lnb = ln
    n = x.shape[0]
    nt = n // tm
    nf = D_FF // tf
    halo_is_gate = gate_prev is not None
    if halo_is_gate:
        halo_rows = tail_rows = gate_prev.shape[0]
        halo, halo_spec = gate_prev, pl.BlockSpec((halo_rows, tf), lambda i, j: (0, j))
        x_rows = tm
        tail_shape = (tail_rows, D_FF)
        tail_spec = pl.BlockSpec((tail_rows, tf), lambda i, j: (0, j))
        w_in, w_out = weights
        w_args = (w_in, w_in, w_out)
        w_specs = [pl.BlockSpec((None, D_MODEL, tf), lambda i, j: (layer, 0, j)),
                   pl.BlockSpec((None, D_MODEL, tf), lambda i, j: (layer, 0, j + nf)),
                   pl.BlockSpec((None, tf, D_MODEL), lambda i, j: (layer, j, 0))]
        extra_specs = [pl.BlockSpec((D_MODEL, tf), lambda i, j: (0, j)),
                       pl.BlockSpec((D_MODEL, tf), lambda i, j: (0, j)),
                       pl.BlockSpec((tf, D_MODEL), lambda i, j: (j, 0))]
        extra_shapes = [jax.ShapeDtypeStruct((D_MODEL, D_FF), BF16),
                        jax.ShapeDtypeStruct((D_MODEL, D_FF), BF16),
                        jax.ShapeDtypeStruct((D_FF, D_MODEL), BF16)]
    else:
        halo_rows, tail_rows = BF16_ROWS, SUBLANES
        halo, halo_spec = x, _prev_rows_spec(tm, halo_rows, D_MODEL, 2)
        x_rows = tm + halo_rows
        tail_shape = (nt, tail_rows, D_FF)
        tail_spec = pl.BlockSpec((None, tail_rows, tf), lambda i, j: (i, 0, j))
        w_args = weights
        w_specs = [pl.BlockSpec((D_MODEL, tf), lambda i, j: (0, j)),
                   pl.BlockSpec((D_MODEL, tf), lambda i, j: (0, j)),
                   pl.BlockSpec((tf, D_MODEL), lambda i, j: (j, 0))]
        extra_specs, extra_shapes = [], []
    body = functools.partial(_ffn_body, tm=tm, halo=halo_rows, shift=shift,
                             halo_is_gate=halo_is_gate, tail_rows=tail_rows,
                             tiles_per_seq=tiles_per_seq, emit_bf16=halo_is_gate)
    ln_row = 2 * layer + 1
    outs = pl.pallas_call(
        body,
        grid=(nt, nf),
        in_specs=[
            pl.BlockSpec((tm, D_MODEL), lambda i, j: (i, 0)),
            halo_spec,
            *w_specs,
            pl.BlockSpec((None, CONV_W, tf), lambda i, j: (layer, 0, j)),
            pl.BlockSpec((None, 1, tf), lambda i, j: (layer, 0, j)),
            pl.BlockSpec((None, 1, D_MODEL), lambda i, j: (ln_row, 0, 0)),
            pl.BlockSpec((None, 1, D_MODEL), lambda i, j: (ln_row, 0, 0)),
        ],
        out_specs=[pl.BlockSpec((tm, D_MODEL), lambda i, j: (i, 0)), tail_spec, *extra_specs],
        out_shape=[jax.ShapeDtypeStruct((n, D_MODEL), F32),
                   jax.ShapeDtypeStruct(tail_shape, F32), *extra_shapes],
        scratch_shapes=[pltpu.VMEM((x_rows, D_MODEL), BF16),
                        pltpu.VMEM((tm + halo_rows, tf), F32),
                        pltpu.VMEM((tm, D_MODEL), F32)],
        compiler_params=_params("parallel", "arbitrary"),
        name="conv_ffn_ln",
    )(x, halo, *w_args, cw, cb, lng, lnb)
    return outs[0], outs[1], tuple(outs[2:])


def _proj_ln_body(a_ref, x_ref, w_ref, bias_ref, lng_ref, lnb_ref, out_ref):
    h = _dot(a_ref[...], w_ref[...]) + bias_ref[...]
    out_ref[...] = _layer_norm(ALPHA * x_ref[...] + h, lng_ref[...], lnb_ref[...])


def _proj_ln_call(a, x, w, bias, lng, lnb, tm):
    n, k = a.shape
    return pl.pallas_call(
        _proj_ln_body,
        grid=(n // tm,),
        in_specs=[_row_spec(tm, k), _row_spec(tm, D_MODEL), _const_spec(w.shape),
                  _const_spec((1, D_MODEL)), _const_spec((1, D_MODEL)), _const_spec((1, D_MODEL))],
        out_specs=_row_spec(tm, D_MODEL),
        out_shape=jax.ShapeDtypeStruct((n, D_MODEL), F32),
        compiler_params=_params("parallel"),
        name="proj_res_ln",
    )(a, x, w, bias, lng, lnb)


POOL_HALO = 4 * SUBLANES


def _pool_prompt_body(x_ref, halo_ref, w_ref, scale_ref, lng_ref, lnb_ref, out_ref, ext_scr,
                      tmp_scr, *, tm, tiles_per_seq):
    i = pl.program_id(0)
    hb = POOL_HALO
    end = hb + tm
    ext_scr[0:hb, :] = halo_ref[...] * (i % tiles_per_seq != 0).astype(F32)
    ext_scr[hb:end, :] = x_ref[...]
    pos = (i % tiles_per_seq) * tm + lax.broadcasted_iota(jnp.int32, (tm, 1), 0)
    for g, wnd in enumerate(POOL_WINDOWS):
        c0 = g * POOL_GC
        stages = wnd.bit_length() - 1
        src, cols = ext_scr, slice(c0, c0 + POOL_GC)
        for k in range(1, stages + 1):
            w = 2 ** (k - 1)
            start = hb if k == stages else SUBLANES * k
            win = src[start:end, cols] + src[start - w:end - w, cols]
            if k < stages:
                dst = tmp_scr.at[k % 2]
                dst[start:end, :] = win
                src, cols = dst, slice(0, POOL_GC)
        xg = x_ref[:, c0:c0 + POOL_GC]
        inv_cnt = 1.0 / jnp.minimum(pos + 1, wnd).astype(F32)
        pooled = win * inv_cnt - xg
        y = _dot(pooled.astype(BF16), w_ref[g]) * scale_ref[:, c0:c0 + POOL_GC]
        ext_scr[hb:end, c0:c0 + POOL_GC] = ALPHA * xg + y
    out_ref[...] = _layer_norm(ext_scr[hb:end, :], lng_ref[...], lnb_ref[...])


def _pool_prompt_call(x, w, scale, lng, lnb, tm, tiles_per_seq):
    n = x.shape[0]
    hb = POOL_HALO
    body = functools.partial(_pool_prompt_body, tm=tm, tiles_per_seq=tiles_per_seq)
    return pl.pallas_call(
        body,
        grid=(n // tm,),
        in_specs=[_row_spec(tm, D_MODEL),
                  _prev_rows_spec(tm, hb, D_MODEL, 1),
                  _const_spec(w.shape), _const_spec((1, D_MODEL)),
                  _const_spec((1, D_MODEL)), _const_spec((1, D_MODEL))],
        out_specs=_row_spec(tm, D_MODEL),
        out_shape=jax.ShapeDtypeStruct((n, D_MODEL), F32),
        scratch_shapes=[pltpu.VMEM((tm + hb, D_MODEL), F32),
                        pltpu.VMEM((2, tm + hb, POOL_GC), F32)],
        compiler_params=_params("parallel"),
        name="pool_mix_ln_prompt",
    )(x, x, w, scale, lng, lnb)


def _pool_sample_body(prev_ref, x_ref, w_ref, scale_ref, lng_ref, lnb_ref, out_ref, state_ref,
                      y_scr, *, t_new, bt):
    def row(r, cols):
        return prev_ref[r, :, cols] if r < POOL_BUF else x_ref[r - POOL_BUF, :, cols]

    for r in range(POOL_BUF):
        state_ref[r] = row(r + t_new, slice(None))

    for g, wnd in enumerate(POOL_WINDOWS):
        cols = slice(g * POOL_GC, (g + 1) * POOL_GC)
        pooled = []
        for t in range(t_new):
            win = x_ref[t, :, cols]
            for k in range(1, wnd):
                win = win + row(POOL_BUF + t - k, cols)
            pooled.append(win / float(wnd) - x_ref[t, :, cols])
        pooled = jnp.concatenate(pooled, axis=0)
        y = _dot(pooled.astype(BF16), w_ref[g]) * scale_ref[:, cols]
        for t in range(t_new):
            y_scr[t, :, cols] = ALPHA * x_ref[t, :, cols] + y[t * bt:(t + 1) * bt]
    for t in range(t_new):
        out_ref[t] = _layer_norm(y_scr[t], lng_ref[...], lnb_ref[...])


def _pool_sample_call(prev_all, layer, x, w, scale, lng, lnb, bt):
    t_new, nb, _ = x.shape
    body = functools.partial(_pool_sample_body, t_new=t_new, bt=bt)
    return pl.pallas_call(
        body,
        grid=(nb // bt,),
        in_specs=[pl.BlockSpec((None, POOL_BUF, bt, D_MODEL), lambda i: (layer, 0, i, 0)),
                  pl.BlockSpec((t_new, bt, D_MODEL), lambda i: (0, i, 0)),
                  _const_spec(w.shape), _const_spec((1, D_MODEL)),
                  _const_spec((1, D_MODEL)), _const_spec((1, D_MODEL))],
        out_specs=[pl.BlockSpec((t_new, bt, D_MODEL), lambda i: (0, i, 0)),
                   pl.BlockSpec((POOL_BUF, bt, D_MODEL), lambda i: (0, i, 0))],
        out_shape=[jax.ShapeDtypeStruct((t_new, nb, D_MODEL), F32),
                   jax.ShapeDtypeStruct((POOL_BUF, nb, D_MODEL), F32)],
        scratch_shapes=[pltpu.VMEM((t_new, bt, D_MODEL), F32)],
        compiler_params=_params("parallel"),
        name="pool_mix_ln_sample",
    )(prev_all, x, w, scale, lng, lnb)


def _rope_pair(pair, cs):
    t = pair * cs
    return t + pltpu.roll(t, MLA_ROPE, 1)


def _mla_a_body(x_ref, cs_ref, w_ref, gq_ref, gkv_ref, cq_ref, ckv_ref, kpe_ref, kpad_ref):
    a = _dot(x_ref[...].astype(BF16), w_ref[...])
    cq_ref[...] = _rms_norm(a[:, :MLA_Q_RANK], gq_ref[...]).astype(BF16)
    ckv_ref[...] = _rms_norm(a[:, MLA_Q_RANK:MLA_Q_RANK + MLA_KV_RANK], gkv_ref[...])
    r = _rope_pair(a[:, MLA_Q_RANK + MLA_KV_RANK:], cs_ref[...])
    kpe_ref[...] = r[:, :MLA_ROPE]
    lane = lax.broadcasted_iota(jnp.int32, r.shape, 1)
    kpad_ref[...] = jnp.where(lane < MLA_ROPE, r, 0.0).astype(BF16)


def _mla_a_call(x, cs, w, gq, gkv, tm, tiles_per_seq):
    n = x.shape[0]
    return pl.pallas_call(
        _mla_a_body,
        grid=(n // tm,),
        in_specs=[_row_spec(tm, D_MODEL),
                  pl.BlockSpec((tm, LANES), lambda i: (i % tiles_per_seq, 0)),
                  _const_spec(w.shape), _const_spec((1, MLA_Q_RANK)),
                  _const_spec((1, MLA_KV_RANK))],
        out_specs=[_row_spec(tm, MLA_Q_RANK), _row_spec(tm, MLA_KV_RANK),
                   _row_spec(tm, MLA_ROPE), _row_spec(tm, LANES)],
        out_shape=[jax.ShapeDtypeStruct((n, MLA_Q_RANK), BF16),
                   jax.ShapeDtypeStruct((n, MLA_KV_RANK), F32),
                   jax.ShapeDtypeStruct((n, MLA_ROPE), F32),
                   jax.ShapeDtypeStruct((n, LANES), BF16)],
        compiler_params=_params("parallel"),
        name="mla_down_proj",
    )(x, cs, w, gq, gkv)


MLA_QW = MLA_NOPE + 2 * MLA_ROPE


def _mla_q_body(cq_ref, cs_ref, w_ref, q_ref):
    cs = cs_ref[...]
    for h in range(MLA_HEADS):
        c0 = h * MLA_QW
        qh = _dot(cq_ref[...], w_ref[:, c0:c0 + MLA_QW])
        q_ref[:, c0:c0 + MLA_NOPE] = qh[:, :MLA_NOPE].astype(BF16)
        q_ref[:, c0 + MLA_NOPE:c0 + MLA_QW] = _rope_pair(qh[:, MLA_NOPE:], cs).astype(BF16)


def _mla_q_call(cq, cs, w, tm, tiles_per_seq):
    n = cq.shape[0]
    return pl.pallas_call(
        _mla_q_body,
        grid=(n // tm,),
        in_specs=[_row_spec(tm, MLA_Q_RANK),
                  pl.BlockSpec((tm, LANES), lambda i: (i % tiles_per_seq, 0)),
                  _const_spec(w.shape)],
        out_specs=_row_spec(tm, MLA_HEADS * MLA_QW),
        out_shape=jax.ShapeDtypeStruct((n, MLA_HEADS * MLA_QW), BF16),
        compiler_params=_params("parallel"),
        name="mla_q_proj",
    )(cq, cs, w)


def _mla_kv_up_body(ckv_ref, kpad_ref, wuk_ref, wuv_t_ref, k_ref, vt_ref):
    ckv = ckv_ref[...].astype(BF16)
    vt_ref[...] = _dot_nt(wuv_t_ref[...], ckv).astype(BF16)
    k_nope = _dot(ckv, wuk_ref[...]).astype(BF16)
    for h in range(MLA_HEADS):
        c0 = h * MLA_QW
        k_ref[:, c0:c0 + MLA_NOPE] = k_nope[:, h * MLA_NOPE:(h + 1) * MLA_NOPE]
        k_ref[:, c0 + MLA_NOPE:c0 + MLA_QW] = kpad_ref[...]


def _mla_kv_up_call(ckv, kpad, wuk, wuv_t, tm):
    n = ckv.shape[0]
    return pl.pallas_call(
        _mla_kv_up_body,
        grid=(n // tm,),
        in_specs=[_row_spec(tm, MLA_KV_RANK), _row_spec(tm, LANES),
                  _const_spec(wuk.shape), _const_spec(wuv_t.shape)],
        out_specs=[_row_spec(tm, MLA_HEADS * MLA_QW),
                   pl.BlockSpec((MLA_HEADS * MLA_V, tm), lambda i: (0, i))],
        out_shape=[jax.ShapeDtypeStruct((n, MLA_HEADS * MLA_QW), BF16),
                   jax.ShapeDtypeStruct((MLA_HEADS * MLA_V, n), BF16)],
        compiler_params=_params("parallel"),
        name="mla_kv_up_proj",
    )(ckv, kpad, wuk, wuv_t)


def _softmax_init(m_scr, l_scr, acc_scr):
    m_scr[...] = jnp.full(m_scr.shape, NEG_INF, F32)
    l_scr[...] = jnp.zeros(l_scr.shape, F32)
    acc_scr[...] = jnp.zeros(acc_scr.shape, F32)


LOG2E = math.log2(math.e)
MLA_EXP2_SCALE = MLA_SCALE * LOG2E


def _flash_update(s, v, m_ref, l_ref, acc_ref):
    m_prev = m_ref[...]
    m_new = jnp.maximum(m_prev, jnp.max(s, axis=-1, keepdims=True))
    corr = jnp.exp2((m_prev - m_new) * MLA_EXP2_SCALE)
    p = jnp.exp2(s * MLA_EXP2_SCALE - m_new * MLA_EXP2_SCALE)
    l_ref[...] = l_ref[...] * corr + jnp.sum(p, axis=-1, keepdims=True)
    acc_ref[...] = acc_ref[...] * corr + _dot(p.astype(BF16), v)
    m_ref[...] = m_new


def _flash_update_t(s_t, v_t, m_ref, l_ref, acc_ref):
    m_prev = m_ref[...]
    m_new = jnp.maximum(m_prev, jnp.max(s_t, axis=0, keepdims=True))
    corr = jnp.exp2((m_prev - m_new) * MLA_EXP2_SCALE)
    p = jnp.exp2(s_t * MLA_EXP2_SCALE - m_new * MLA_EXP2_SCALE)
    l_ref[...] = l_ref[...] * corr + jnp.sum(p, axis=0, keepdims=True)
    acc_ref[...] = acc_ref[...] * corr + _dot(v_t, p.astype(BF16))
    m_ref[...] = m_new


def _mla_flash_body(q_ref, k_ref, vt_ref, o_ref, s_scr, m_scr, l_scr, acc_scr, *, t, heads):
    qi = pl.program_id(2)

    def scores(hh, ki):
        k = k_ref[pl.ds(pl.multiple_of(ki * t, t), t), hh * MLA_QW:(hh + 1) * MLA_QW]
        return _dot_nt(k, q_ref[:, hh * MLA_QW:(hh + 1) * MLA_QW])

    def values(hh, ki):
        return vt_ref[hh * MLA_V:(hh + 1) * MLA_V, pl.ds(pl.multiple_of(ki * t, t), t)]

    def advance(ki, slot):
        for hh in range(heads):
            s_scr[hh, 1 - slot] = scores(hh, ki + 1)
        for hh in range(heads):
            _flash_update_t(s_scr[hh, slot], values(hh, ki), m_scr.at[hh], l_scr.at[hh],
                            acc_scr.at[hh])

    def finish(slot):
        key = lax.broadcasted_iota(jnp.int32, (t, t), 0)
        query = lax.broadcasted_iota(jnp.int32, (t, t), 1)
        for hh in range(heads):
            s = jnp.where(key <= query, s_scr[hh, slot], NEG_INF)
            _flash_update_t(s, values(hh, qi), m_scr.at[hh], l_scr.at[hh], acc_scr.at[hh])
            o_t = acc_scr[hh] / l_scr[hh]
            o_ref[:, hh * MLA_V:(hh + 1) * MLA_V] = o_t.T.astype(BF16)

    for hh in range(heads):
        _softmax_init(m_scr.at[hh], l_scr.at[hh], acc_scr.at[hh])
        s_scr[hh, 0] = scores(hh, 0)

    def pair_step(kk, carry):
        advance(2 * kk, 0)
        advance(2 * kk + 1, 1)
        return carry

    lax.fori_loop(0, qi // 2, pair_step, 0)

    @pl.when(qi % 2 == 0)
    def _():
        finish(0)

    @pl.when(qi % 2 == 1)
    def _():
        advance(qi - 1, 0)
        finish(1)


def _mla_flash_call(q, k, v_t, batch, seq):
    t, heads = FLASH_T, FLASH_HEADS
    nq = seq // t
    body = functools.partial(_mla_flash_body, t=t, heads=heads)
    return pl.pallas_call(
        body,
        grid=(batch, MLA_HEADS // heads, nq),
        in_specs=[pl.BlockSpec((t, heads * MLA_QW), lambda b, h, i: (b * nq + i, h)),
                  pl.BlockSpec((seq, heads * MLA_QW), lambda b, h, i: (b, h)),
                  pl.BlockSpec((heads * MLA_V, seq), lambda b, h, i: (h, b))],
        out_specs=pl.BlockSpec((t, heads * MLA_V), lambda b, h, i: (b * nq + i, h)),
        out_shape=jax.ShapeDtypeStruct((batch * seq, MLA_HEADS * MLA_V), BF16),
        scratch_shapes=[pltpu.VMEM((heads, 2, t, t), F32),
                        pltpu.VMEM((heads, 1, t), F32), pltpu.VMEM((heads, 1, t), F32),
                        pltpu.VMEM((heads, MLA_V, t), F32)],
        compiler_params=_params("parallel", "parallel", "arbitrary"),
        name="mla_prompt_attention",
    )(q, k, v_t)


MLA_LATQ = MLA_KV_RANK + 2 * MLA_ROPE


def _mla_absorb_q_body(q_ref, wuk_ref, out_ref):
    q = q_ref[...]
    out_ref[:, :MLA_KV_RANK] = _dot_nt(q[:, :MLA_NOPE], wuk_ref[...]).astype(BF16)
    out_ref[:, MLA_KV_RANK:] = q[:, MLA_NOPE:]


def _mla_absorb_q_call(q, wuk):
    n = q.shape[0]
    return pl.pallas_call(
        _mla_absorb_q_body,
        grid=(MLA_HEADS,),
        in_specs=[pl.BlockSpec((n, MLA_QW), lambda h: (0, h)),
                  pl.BlockSpec((MLA_KV_RANK, MLA_NOPE), lambda h: (0, h))],
        out_specs=pl.BlockSpec((n, MLA_LATQ), lambda h: (0, h)),
        out_shape=jax.ShapeDtypeStruct((n, MLA_HEADS * MLA_LATQ), BF16),
        compiler_params=_params("parallel"),
        name="mla_absorb_q",
    )(q, wuk)


def _mla_decode_body(pt_ref, q_ref, nck_ref, nkp_ref, ck_hbm, kp_hbm, o_ref, ck_buf, kp_buf, sem,
                     ck_scr, kp_scr, m_scr, l_scr, acc_scr, *, pages, chains, t_new, layer):
    b = pl.program_id(0)
    step = pl.program_id(1)
    n_steps = pl.num_programs(1)
    idx = b * n_steps + step
    slot = idx % 2
    per_chain = pages // chains
    span = per_chain * PAGE_SIZE

    def page_copies(page_of, sl):
        copies = []
        for g in range(pages):
            page = page_of(g)
            copies.append(pltpu.make_async_copy(ck_hbm.at[layer, page], ck_buf.at[sl, g],
                                                sem.at[0, sl]))
            copies.append(pltpu.make_async_copy(kp_hbm.at[layer, page], kp_buf.at[sl, g],
                                                sem.at[1, sl]))
        return copies

    @pl.when(idx == 0)
    def _():
        for cp in page_copies(lambda g: pt_ref[0, g], 0):
            cp.start()

    @pl.when(idx + 1 < pl.num_programs(0) * n_steps)
    def _():
        wrap = step == n_steps - 1
        nb = jnp.where(wrap, b + 1, b)
        ns = jnp.where(wrap, 0, step + 1)
        for cp in page_copies(lambda g: pt_ref[nb, ns * pages + g], 1 - slot):
            cp.start()

    for cp in page_copies(lambda g: 0, slot):
        cp.wait()

    @pl.when(step == 0)
    def _():
        _softmax_init(m_scr, l_scr, acc_scr)

    q_lat = q_ref[0, :, :MLA_KV_RANK]
    q_pe = q_ref[0, :, MLA_KV_RANK:MLA_KV_RANK + MLA_ROPE]
    scores = []
    for c in range(chains):
        for g in range(c * per_chain, (c + 1) * per_chain):
            ck_scr[g * PAGE_SIZE:(g + 1) * PAGE_SIZE, :] = ck_buf[slot, g].astype(BF16)
            kp_scr[:, g * PAGE_SIZE:(g + 1) * PAGE_SIZE] = kp_buf[slot, g].astype(BF16)
        scores.append(_dot_nt(q_lat, ck_scr[c * span:(c + 1) * span, :])
                      + _dot(q_pe, kp_scr[:, c * span:(c + 1) * span]))
    for c in range(chains):
        _flash_update(scores[c], ck_scr[c * span:(c + 1) * span, :], m_scr.at[c], l_scr.at[c],
                      acc_scr.at[c])

    @pl.when(step == pl.num_programs(1) - 1)
    def _():
        for c in range(1, chains):
            m = jnp.maximum(m_scr[0], m_scr[c])
            w0 = jnp.exp2((m_scr[0] - m) * MLA_EXP2_SCALE)
            wc = jnp.exp2((m_scr[c] - m) * MLA_EXP2_SCALE)
            l_scr[0] = l_scr[0] * w0 + l_scr[c] * wc
            acc_scr[0] = acc_scr[0] * w0 + acc_scr[c] * wc
            m_scr[0] = m
        pad = PAGE_SIZE - nck_ref.shape[1]
        nck = jnp.concatenate([nck_ref[0], jnp.zeros((pad, MLA_KV_RANK), F32)],
                              axis=0).astype(BF16)
        nkp = jnp.concatenate([nkp_ref[0], jnp.zeros((pad, MLA_ROPE), F32)],
                              axis=0).astype(BF16)
        s2 = _dot_nt(q_lat, nck) + _dot_nt(q_pe, nkp)
        tq = lax.broadcasted_iota(jnp.int32, s2.shape, 0) // MLA_HEADS
        tk = lax.broadcasted_iota(jnp.int32, s2.shape, 1)
        s2 = jnp.where((tk <= tq) & (tk < t_new), s2, NEG_INF)
        _flash_update(s2, nck, m_scr.at[0], l_scr.at[0], acc_scr.at[0])
        o_ref[0] = (acc_scr[0] / l_scr[0]).astype(BF16)


def _mla_decode_call(page_table, q, new_ckv, new_kpe, cache_ckv, cache_kpe_t, layer, t_new):
    nb, rows, _ = q.shape
    n_pages = page_table.shape[1]
    pages, chains = DECODE_PAGES, DECODE_CHAINS
    pad_rows = new_ckv.shape[1]

    body = functools.partial(_mla_decode_body, pages=pages, chains=chains, t_new=t_new,
                             layer=layer)
    grid_spec = pltpu.PrefetchScalarGridSpec(
        num_scalar_prefetch=1,
        grid=(nb, n_pages // pages),
        in_specs=[pl.BlockSpec((1, rows, MLA_LATQ), lambda b, s, pt: (b, 0, 0)),
                  pl.BlockSpec((1, pad_rows, MLA_KV_RANK), lambda b, s, pt: (b, 0, 0)),
                  pl.BlockSpec((1, pad_rows, MLA_ROPE), lambda b, s, pt: (b, 0, 0)),
                  pl.BlockSpec(memory_space=pl.ANY),
                  pl.BlockSpec(memory_space=pl.ANY)],
        out_specs=pl.BlockSpec((1, rows, MLA_KV_RANK), lambda b, s, pt: (b, 0, 0)),
        scratch_shapes=[pltpu.VMEM((2, pages, PAGE_SIZE, MLA_KV_RANK), F32),
                        pltpu.VMEM((2, pages, MLA_ROPE, PAGE_SIZE), F32),
                        pltpu.SemaphoreType.DMA((2, 2)),
                        pltpu.VMEM((pages * PAGE_SIZE, MLA_KV_RANK), BF16),
                        pltpu.VMEM((MLA_ROPE, pages * PAGE_SIZE), BF16),
                        pltpu.VMEM((chains, rows, 1), F32), pltpu.VMEM((chains, rows, 1), F32),
                        pltpu.VMEM((chains, rows, MLA_KV_RANK), F32)],
    )
    return pl.pallas_call(
        body,
        grid_spec=grid_spec,
        out_shape=jax.ShapeDtypeStruct((nb, rows, MLA_KV_RANK), BF16),
        compiler_params=_params("arbitrary", "arbitrary"),
        name="mla_paged_decode",
    )(page_table, q, new_ckv, new_kpe, cache_ckv, cache_kpe_t)


def _mla_absorb_o_body(o_ref, wuv_ref, out_ref):
    out_ref[...] = _dot(o_ref[0], wuv_ref[...]).astype(BF16)


def _mla_absorb_o_call(o_lat, wuv):
    _, n, _ = o_lat.shape
    return pl.pallas_call(
        _mla_absorb_o_body,
        grid=(MLA_HEADS,),
        in_specs=[pl.BlockSpec((1, n, MLA_KV_RANK), lambda h: (h, 0, 0)),
                  pl.BlockSpec((MLA_KV_RANK, MLA_V), lambda h: (0, h))],
        out_specs=pl.BlockSpec((n, MLA_V), lambda h: (0, h)),
        out_shape=jax.ShapeDtypeStruct((n, MLA_HEADS * MLA_V), BF16),
        compiler_params=_params("parallel"),
        name="mla_absorb_o",
    )(o_lat, wuv)


SWA_NQ = SWA_HEADS * SWA_HEAD_DIM
SWA_NKV = SWA_KV_HEADS * SWA_HEAD_DIM
SWA_PAIRS = SWA_KV_HEADS // 2


def _swa_regroup(a, axis):
    shape = a.shape
    split = shape[:axis] + (SWA_PAIRS, 2, SWA_GROUP, SWA_HEAD_DIM) + shape[axis + 1:]
    return jnp.swapaxes(a.reshape(split), axis + 1, axis + 2).reshape(shape)


def _swa_slope(head):
    return 2.0 ** (-8.0 * (head + 1) / SWA_HEADS)


def _swa_qkv_body(x_ref, w_ref, b_ref, q_ref, k_ref, v_ref):
    qkv = _dot(x_ref[...].astype(BF16), w_ref[...]) + b_ref[...]
    q_ref[...] = qkv[:, :SWA_NQ].astype(BF16)
    k_ref[...] = qkv[:, SWA_NQ:SWA_NQ + SWA_NKV]
    v_ref[...] = qkv[:, SWA_NQ + SWA_NKV:]


def _swa_qkv_call(x, w, b, tm):
    n = x.shape[0]
    return pl.pallas_call(
        _swa_qkv_body,
        grid=(n // tm,),
        in_specs=[_row_spec(tm, D_MODEL), _const_spec(w.shape), _const_spec(b.shape)],
        out_specs=[_row_spec(tm, SWA_NQ), _row_spec(tm, SWA_NKV), _row_spec(tm, SWA_NKV)],
        out_shape=[jax.ShapeDtypeStruct((n, SWA_NQ), BF16),
                   jax.ShapeDtypeStruct((n, SWA_NKV), F32),
                   jax.ShapeDtypeStruct((n, SWA_NKV), F32)],
        compiler_params=_params("parallel"),
        name="swa_qkv_proj",
    )(x, w, b)


def _swa_heads(q_blk, k_pair, v_pair, sink_ref, p, g, dist, valid):
    lane_q = lax.broadcasted_iota(jnp.int32, q_blk.shape, 1)
    outs = []
    for half in range(2):
        head = (2 * p + half) * SWA_GROUP + g
        in_half = (lane_q >= SWA_HEAD_DIM) if half else (lane_q < SWA_HEAD_DIM)
        qm = jnp.where(in_half, q_blk, jnp.zeros_like(q_blk))
        s = _dot_nt(qm, k_pair) * (SWA_SCALE * LOG2E) - (_swa_slope(head) * LOG2E) * dist
        s = jnp.where(valid, s, NEG_INF)
        sink = sink_ref[head] * LOG2E
        m = jnp.maximum(jnp.max(s, axis=-1, keepdims=True), sink)
        e = jnp.exp2(s - m)
        denom = jnp.sum(e, axis=-1, keepdims=True) + jnp.exp2(sink - m)
        outs.append(_dot(e.astype(BF16), v_pair) * (1.0 / denom))
    lane_o = lax.broadcasted_iota(jnp.int32, outs[0].shape, 1)
    return jnp.where(lane_o < SWA_HEAD_DIM, outs[0], outs[1])


def _swa_prompt_body(sink_ref, q_ref, kc_ref, kp_ref, vc_ref, vp_ref, o_ref):
    i = pl.program_id(1)
    k = jnp.concatenate([kp_ref[...].astype(BF16), kc_ref[...].astype(BF16)], axis=0)
    v = jnp.concatenate([vp_ref[...].astype(BF16), vc_ref[...].astype(BF16)], axis=0)
    row = lax.broadcasted_iota(jnp.int32, (WINDOW, 2 * WINDOW), 0)
    col = lax.broadcasted_iota(jnp.int32, (WINDOW, 2 * WINDOW), 1)
    dist = row + WINDOW - col
    valid = (dist >= 0) & (dist < WINDOW) & ((col >= WINDOW) | (i > 0))
    distf = dist.astype(F32)
    for p in range(SWA_PAIRS):
        kpair = k[:, p * LANES:(p + 1) * LANES]
        vpair = v[:, p * LANES:(p + 1) * LANES]
        for g in range(SWA_GROUP):
            c0 = (p * SWA_GROUP + g) * LANES
            o = _swa_heads(q_ref[:, c0:c0 + LANES], kpair, vpair, sink_ref, p, g, distf, valid)
            o_ref[:, c0:c0 + LANES] = o.astype(BF16)


def _swa_prompt_call(sinks, q, k, v, batch, seq):
    nb = seq // WINDOW
    cur = lambda b, i: (b * nb + i, 0)
    prev = lambda b, i: (b * nb + jnp.maximum(i - 1, 0), 0)
    return pl.pallas_call(
        _swa_prompt_body,
        grid=(batch, nb),
        in_specs=[pl.BlockSpec(memory_space=pltpu.SMEM),
                  pl.BlockSpec((WINDOW, SWA_NQ), cur),
                  pl.BlockSpec((WINDOW, SWA_NKV), cur), pl.BlockSpec((WINDOW, SWA_NKV), prev),
                  pl.BlockSpec((WINDOW, SWA_NKV), cur), pl.BlockSpec((WINDOW, SWA_NKV), prev)],
        out_specs=pl.BlockSpec((WINDOW, SWA_NQ), cur),
        out_shape=jax.ShapeDtypeStruct((batch * seq, SWA_NQ), BF16),
        compiler_params=_params("parallel", "arbitrary"),
        name="swa_prompt_attention",
    )(sinks, q, k, k, v, v)


def _swa_sample_body(sink_ref, q_ref, kc_ref, vc_ref, kn_ref, vn_ref, o_ref, ko_ref, vo_ref,
                     *, bt, t_pad, t_new):
    n_keys = 2 * WINDOW
    fill = jnp.zeros((WINDOW - t_pad, LANES), F32)
    rows = 2 * SWA_GROUP * t_pad
    ridx = lax.broadcasted_iota(jnp.int32, (rows, 1), 0) // t_pad
    tq = lax.broadcasted_iota(jnp.int32, (rows, n_keys), 0) % t_pad
    col = lax.broadcasted_iota(jnp.int32, (rows, n_keys), 1)
    dist = tq + WINDOW - col
    valid = (dist >= 0) & (dist < WINDOW)
    distf = dist.astype(F32)
    lane = lax.broadcasted_iota(jnp.int32, (t_pad, LANES), 1)
    bias, sink_cols = [], []
    for p in range(SWA_PAIRS):
        slope = jnp.zeros((rows, 1), F32)
        sink = jnp.zeros((rows, 1), F32)
        for half in range(2):
            for g in range(SWA_GROUP):
                head = (2 * p + half) * SWA_GROUP + g
                here = ridx == half * SWA_GROUP + g
                slope = jnp.where(here, _swa_slope(head), slope)
                sink = jnp.where(here, sink_ref[head], sink)
        bias.append(slope * distf)
        sink_cols.append(sink)

    def one_seq(bb, carry):
        scores = []
        for p in range(SWA_PAIRS):
            pair = slice(p * LANES, (p + 1) * LANES)
            k_old = kc_ref[bb, pair, :].astype(BF16)
            k_new = jnp.concatenate([kn_ref[bb, :, pair], fill], axis=0).astype(BF16)
            blocks = [q_ref[bb, :, (p * SWA_GROUP + g) * LANES:(p * SWA_GROUP + g + 1) * LANES]
                      .astype(F32) for g in range(SWA_GROUP)]
            stack = ([jnp.where(lane < SWA_HEAD_DIM, blk, 0.0) for blk in blocks]
                     + [jnp.where(lane >= SWA_HEAD_DIM, blk, 0.0) for blk in blocks])
            qs = jnp.concatenate(stack, axis=0).astype(BF16)
            scores.append(jnp.concatenate([_dot(qs, k_old), _dot_nt(qs, k_new)], axis=1))
        probs = []
        for p in range(SWA_PAIRS):
            s = scores[p] * SWA_SCALE - bias[p]
            s = jnp.where(valid, s, NEG_INF)
            m = jnp.maximum(jnp.max(s, axis=-1, keepdims=True), sink_cols[p])
            e = jnp.exp(s - m)
            denom = jnp.sum(e, axis=-1, keepdims=True) + jnp.exp(sink_cols[p] - m)
            probs.append((e / denom).astype(BF16))
        for p in range(SWA_PAIRS):
            pair = slice(p * LANES, (p + 1) * LANES)
            v_old = vc_ref[bb, pair, :].astype(BF16)
            v_new = jnp.concatenate([vn_ref[bb, :, pair], fill], axis=0).astype(BF16)
            pr = probs[p]
            o = _dot_nt(pr[:, :WINDOW], v_old) + _dot(pr[:, WINDOW:], v_new)
            for g in range(SWA_GROUP):
                lo = o[g * t_pad:(g + 1) * t_pad]
                hi = o[(SWA_GROUP + g) * t_pad:(SWA_GROUP + g + 1) * t_pad]
                c0 = (p * SWA_GROUP + g) * LANES
                o_ref[bb, :, c0:c0 + LANES] = jnp.where(lane < SWA_HEAD_DIM, lo, hi).astype(BF16)
        tcol = lax.broadcasted_iota(jnp.int32, (LANES, WINDOW), 1)
        for old_ref, new_ref, out_ref in ((kc_ref, kn_ref, ko_ref), (vc_ref, vn_ref, vo_ref)):
            for p in range(SWA_PAIRS):
                pair = slice(p * LANES, (p + 1) * LANES)
                new_t = jnp.concatenate([new_ref[bb, :, pair], fill], axis=0).T
                merged = jnp.where(tcol < t_new, new_t, old_ref[bb, pair, :])
                out_ref[bb, pair, :] = pltpu.roll(merged, WINDOW - t_new, 1)
        return carry

    lax.fori_loop(0, bt, one_seq, 0)


def _swa_sample_call(sinks, q, kc, vc, kn, vn, t_new):
    nb, t_pad, _ = q.shape
    bt = SWA_SEQ_TILE
    body = functools.partial(_swa_sample_body, bt=bt, t_pad=t_pad, t_new=t_new)
    seq3 = lambda rows, cols: pl.BlockSpec((bt, rows, cols), lambda i: (i, 0, 0))
    return pl.pallas_call(
        body,
        grid=(nb // bt,),
        in_specs=[pl.BlockSpec(memory_space=pltpu.SMEM),
                  seq3(t_pad, SWA_NQ), seq3(SWA_NKV, WINDOW), seq3(SWA_NKV, WINDOW),
                  seq3(t_pad, SWA_NKV), seq3(t_pad, SWA_NKV)],
        out_specs=[seq3(t_pad, SWA_NQ), seq3(SWA_NKV, WINDOW), seq3(SWA_NKV, WINDOW)],
        out_shape=[jax.ShapeDtypeStruct((nb, t_pad, SWA_NQ), BF16),
                   jax.ShapeDtypeStruct((nb, SWA_NKV, WINDOW), F32),
                   jax.ShapeDtypeStruct((nb, SWA_NKV, WINDOW), F32)],
        compiler_params=_params("parallel"),
        name="swa_sample_attention",
    )(sinks, q, kc, vc, kn, vn)


def _rot_cols(w):
    half = w.shape[-1] // 2
    return jnp.concatenate([-w[..., half:], w[..., :half]], axis=-1)


def _rope_table(pos):
    half = MLA_ROPE // 2
    inv = ROPE_THETA ** (-jnp.arange(half, dtype=F32) / half)
    ang = pos.astype(F32)[:, None] * inv[None, :]
    cos, sin = jnp.cos(ang), jnp.sin(ang)
    return jnp.concatenate([cos, cos, sin, sin], axis=-1)


def _time_major(a, nb, t):
    return jnp.swapaxes(a, 0, 1).reshape((t * nb,) + a.shape[2:])


def _batch_major(a, nb, t):
    return jnp.swapaxes(a.reshape((t, nb) + a.shape[1:]), 0, 1)


def kernel(x_prompt, x_sample, state_pool, cache_mla_ckv, cache_mla_kpe, page_table, cache_swa_k, cache_swa_v, state_ffn_conv, ln_g, ln_b, pool_w, pool_scale, mla_w_a, mla_g_q, mla_g_kv, mla_w_uq, mla_w_uk, mla_w_uv, mla_w_o, swa_w_qkv, swa_b_qkv, swa_sinks, swa_w_o, swa_b_o, ffn_w_in, ffn_conv_w, ffn_conv_b, ffn_w_out):
    B, S, D = x_prompt.shape
    NB, T, _ = x_sample.shape
    NS = NB * T
    tm = ROW_TILE
    tps = S // tm
    assert S % tm == 0 and NS == tm and S % FLASH_T == 0 and T <= SUBLANES

    xp = x_prompt.reshape(B * S, D)
    xs = _time_major(x_sample, NB, T)
    cs_p = _rope_table(jnp.arange(S))
    cs_s = jnp.repeat(_rope_table(PAST_LEN + jnp.arange(T)), NB, axis=0)
    zero_bias = jnp.zeros((1, D), F32)
    ffn_conv = (ffn_conv_w, ffn_conv_b.reshape(DEPTH, 1, D_FF))
    ln_rows = (ln_g.reshape(2 * DEPTH, 1, D), ln_b.reshape(2 * DEPTH, 1, D))

    pool_p, pool_s, ckv_p, ckv_s, kpe_p, kpe_s = [], [], [], [], [], []
    swk_p, swk_s, swv_p, swv_s, conv_p, conv_s = [], [], [], [], [], []

    for i in range(DEPTH):
        kind, j = i % N_MIXERS, i // N_MIXERS
        g1, b1 = ln_g[i, 0][None, :], ln_b[i, 0][None, :]
        if kind == 0:
            w = pool_w[j].astype(BF16)
            sc = pool_scale[j][None, :]
            pool_p.append(xp.reshape(B, S, D)[:, S - POOL_BUF:])
            prev_tm = jnp.swapaxes(state_pool, 1, 2)
            xs, state = _pool_sample_call(prev_tm, j, xs.reshape(T, NB, D), w, sc, g1, b1, 32)
            xs = xs.reshape(NS, D)
            pool_s.append(jnp.swapaxes(state, 0, 1))
            xp = _pool_prompt_call(xp, w, sc, g1, b1, tm, tps)
        elif kind == 1:
            w_a = mla_w_a[j]
            n_lat = MLA_Q_RANK + MLA_KV_RANK
            w_a_ext = jnp.concatenate([w_a, _rot_cols(w_a[:, n_lat:])], axis=1).astype(BF16)
            w_uq = mla_w_uq[j]
            w_uq_ext = jnp.concatenate([w_uq, _rot_cols(w_uq[..., MLA_NOPE:])], axis=-1)
            w_uq_ext = w_uq_ext.reshape(MLA_Q_RANK, MLA_HEADS * MLA_QW).astype(BF16)
            w_uk = mla_w_uk[j].reshape(MLA_KV_RANK, MLA_HEADS * MLA_NOPE).astype(BF16)
            w_uv = mla_w_uv[j].reshape(MLA_KV_RANK, MLA_HEADS * MLA_V).astype(BF16)
            w_o = mla_w_o[j].reshape(MLA_HEADS * MLA_V, D).astype(BF16)
            gq, gkv = mla_g_q[j][None, :], mla_g_kv[j][None, :]

            cq, ckv, kpe, kpad = _mla_a_call(xp, cs_p, w_a_ext, gq, gkv, tm, tps)
            q = _mla_q_call(cq, cs_p, w_uq_ext, tm, tps)
            k_full, v_full = _mla_kv_up_call(ckv, kpad, w_uk, w_uv.T, tm)
            o = _mla_flash_call(q, k_full, v_full, B, S)
            ckv_p.append(ckv.reshape(B, S, MLA_KV_RANK))
            kpe_p.append(kpe.reshape(B, S, MLA_ROPE))
            xp = _proj_ln_call(o, xp, w_o, zero_bias, g1, b1, tm)

            cq, ckv, kpe, _ = _mla_a_call(xs, cs_s, w_a_ext, gq, gkv, tm, 1)
            q = _mla_q_call(cq, cs_s, w_uq_ext, tm, 1)
            q_abs = _mla_absorb_q_call(q, w_uk)
            q_abs = _batch_major(q_abs.reshape(NS, MLA_HEADS, MLA_LATQ), NB, T)
            q_abs = q_abs.reshape(NB, T * MLA_HEADS, MLA_LATQ)
            ckv_bm = _batch_major(ckv, NB, T)
            kpe_bm = _batch_major(kpe, NB, T)
            pad = ((0, 0), (0, SUBLANES - T), (0, 0))
            o_lat = _mla_decode_call(page_table, q_abs, jnp.pad(ckv_bm, pad), jnp.pad(kpe_bm, pad),
                                     cache_mla_ckv, jnp.swapaxes(cache_mla_kpe, 2, 3), j, T)
            o_lat = o_lat.reshape(NB, T, MLA_HEADS, MLA_KV_RANK).transpose(2, 1, 0, 3)
            o = _mla_absorb_o_call(o_lat.reshape(MLA_HEADS, NS, MLA_KV_RANK), w_uv)
            ckv_s.append(ckv_bm)
            kpe_s.append(kpe_bm)
            xs = _proj_ln_call(o, xs, w_o, zero_bias, g1, b1, tm)
        else:
            w_qkv, b_qkv = swa_w_qkv[j], swa_b_qkv[j]
            w_qkv = jnp.concatenate([_swa_regroup(w_qkv[:, :SWA_NQ], 1), w_qkv[:, SWA_NQ:]],
                                    axis=1)
            b_qkv = jnp.concatenate([_swa_regroup(b_qkv[:SWA_NQ], 0), b_qkv[SWA_NQ:]])[None, :]
            w_qkv = w_qkv.astype(BF16)
            w_o = _swa_regroup(swa_w_o[j], 0).astype(BF16)
            b_o = swa_b_o[j][None, :]
            sinks = swa_sinks[j]

            q, k, v = _swa_qkv_call(xp, w_qkv, b_qkv, tm // 2)
            o = _swa_prompt_call(sinks, q, k, v, B, S)
            kv_shape = (B, WINDOW, SWA_KV_HEADS, SWA_HEAD_DIM)
            swk_p.append(k.reshape(B, S, SWA_NKV)[:, S - WINDOW:].reshape(kv_shape))
            swv_p.append(v.reshape(B, S, SWA_NKV)[:, S - WINDOW:].reshape(kv_shape))
            xp = _proj_ln_call(o, xp, w_o, b_o, g1, b1, tm)

            q, k, v = _swa_qkv_call(xs, w_qkv, b_qkv, tm // 2)
            pad = ((0, 0), (0, SUBLANES - T), (0, 0))
            q_bm = jnp.pad(_batch_major(q, NB, T), pad)
            k_bm, v_bm = _batch_major(k, NB, T), _batch_major(v, NB, T)
            kc = jnp.transpose(cache_swa_k[j], (0, 2, 3, 1)).reshape(NB, SWA_NKV, WINDOW)
            vc = jnp.transpose(cache_swa_v[j], (0, 2, 3, 1)).reshape(NB, SWA_NKV, WINDOW)
            o, k_upd, v_upd = _swa_sample_call(sinks, q_bm, kc, vc, jnp.pad(k_bm, pad),
                                               jnp.pad(v_bm, pad), T)
            o = _time_major(o[:, :T], NB, T)
            kv_shape = (NB, SWA_KV_HEADS, SWA_HEAD_DIM, WINDOW)
            swk_s.append(jnp.transpose(k_upd.reshape(kv_shape), (0, 3, 1, 2)))
            swv_s.append(jnp.transpose(v_upd.reshape(kv_shape), (0, 3, 1, 2)))
            xs = _proj_ln_call(o, xs, w_o, b_o, g1, b1, tm)

        prev = _time_major(state_ffn_conv[i], NB, CONV_W - 1)
        xs, tails, w_bf16 = _ffn_call(xs, prev, (ffn_w_in, ffn_w_out), ffn_conv, ln_rows, i,
                                      tm=tm, tf=FFN_SAMPLE_TF, shift=NB, tiles_per_seq=1)
        conv_s.append(_batch_major(tails, NB, CONV_W - 1))
        xp, tails, _ = _ffn_call(xp, None, w_bf16, ffn_conv, ln_rows, i, tm=tm, tf=FFN_TF,
                                 shift=1, tiles_per_seq=tps)
        conv_p.append(tails.reshape(B, tps, SUBLANES, D_FF)[:, -1, SUBLANES - (CONV_W - 1):])

    y_p = xp.reshape(B, S, D)
    y_s = _batch_major(xs, NB, T)
    return (y_p, y_s, jnp.stack(pool_p), jnp.stack(pool_s), jnp.stack(ckv_p), jnp.stack(ckv_s),
            jnp.stack(kpe_p), jnp.stack(kpe_s), jnp.stack(swk_p), jnp.stack(swk_s),
            jnp.stack(swv_p), jnp.stack(swv_s), jnp.stack(conv_p), jnp.stack(conv_s))
```

```python
import functools
import math

import jax
import jax.numpy as jnp
import numpy as np
from jax import lax
from jax.experimental import pallas as pl
from jax.experimental.pallas import tpu as pltpu

D_MODEL = 2048
DEPTH = 4
N_MIXERS = 3
PAST_LEN = 8192
PAGE_SIZE = 128

POOL_WINDOWS = (2, 4, 8, 16)
POOL_GROUPS = len(POOL_WINDOWS)
POOL_GC = D_MODEL // POOL_GROUPS
POOL_BUF = max(POOL_WINDOWS) - 1

MLA_HEADS = 16
MLA_Q_RANK = 512
MLA_KV_RANK = 512
MLA_NOPE = 128
MLA_ROPE = 64
MLA_V = 128
MLA_SCALE = 1.0 / math.sqrt(MLA_NOPE + MLA_ROPE)
ROPE_THETA = 10000.0

SWA_HEADS = 32
SWA_KV_HEADS = 8
SWA_GROUP = SWA_HEADS // SWA_KV_HEADS
SWA_HEAD_DIM = 64
SWA_SCALE = 1.0 / math.sqrt(SWA_HEAD_DIM)
WINDOW = 128

D_FF = 5632
CONV_W = 3

ALPHA = (2.0 * DEPTH) ** 0.25
LN_EPS = 1e-5
RMS_EPS = 1e-6
NEG_INF = -1e30

LANES = 128
SUBLANES = 8
BF16_ROWS = 16
VMEM_LIMIT = 56 * 1024 * 1024

ROW_TILE = 512
ROW_CHUNK = 128
FFN_TF = 512
FFN_SAMPLE_TF = 256
FLASH_T = 512
FLASH_HEADS = 2
DECODE_PAGES = 32
DECODE_CHAINS = 2
SWA_SEQ_TILE = 8

F32 = jnp.float32
BF16 = jnp.bfloat16


def _dot(a, b):
    return jnp.dot(a, b, preferred_element_type=F32)


def _dot_nt(a, b):
    return lax.dot_general(a, b, (((1,), (1,)), ((), ())), preferred_element_type=F32)


def _layer_norm(y, g, b):
    mu = jnp.mean(y, axis=-1, keepdims=True)
    d = y - mu
    var = jnp.mean(d * d, axis=-1, keepdims=True)
    return d * lax.rsqrt(var + LN_EPS) * g + b


def _rms_norm(a, g):
    return a * lax.rsqrt(jnp.mean(a * a, axis=-1, keepdims=True) + RMS_EPS) * g


def _params(*sem):
    return pltpu.CompilerParams(dimension_semantics=sem, vmem_limit_bytes=VMEM_LIMIT)


def _row_spec(tm, cols):
    return pl.BlockSpec((tm, cols), lambda i: (i, 0))


def _const_spec(shape):
    nd = len(shape)
    return pl.BlockSpec(shape, lambda i: (0,) * nd)


def _ffn_body(x_ref, halo_ref, wg_ref, wv_ref, wo_ref, cw_ref, cb_ref, lng_ref, lnb_ref,
              out_ref, tail_ref, *rest, tm, halo, shift, halo_is_gate, tail_rows,
              tiles_per_seq, emit_bf16):
    i = pl.program_id(0)
    j = pl.program_id(1)
    xoff = 0 if halo_is_gate else halo
    if emit_bf16:
        wg_out, wv_out, wo_out, xb_scr, g_scr, acc_scr = rest
        wg_out[...] = wg_ref[...].astype(BF16)
        wv_out[...] = wv_ref[...].astype(BF16)
        wo_out[...] = wo_ref[...].astype(BF16)
        wg_ref, wv_ref, wo_ref = wg_out, wv_out, wo_out
    else:
        xb_scr, g_scr, acc_scr = rest

    @pl.when(j == 0)
    def _():
        if not halo_is_gate:
            keep = (i % tiles_per_seq != 0).astype(F32)
            xb_scr[0:halo, :] = (halo_ref[...] * keep).astype(BF16)
        xb_scr[xoff:xoff + tm, :] = x_ref[...].astype(BF16)
        acc_scr[...] = jnp.zeros_like(acc_scr)

    if halo_is_gate:
        g_scr[0:halo, :] = halo_ref[...]
        g_scr[halo:halo + tm, :] = _dot(xb_scr[...], wg_ref[...])
    else:
        g_scr[...] = _dot(xb_scr[...], wg_ref[...])
    val = _dot(xb_scr[xoff:xoff + tm, :], wv_ref[...])
    conv = (cb_ref[...]
            + g_scr[halo - 2 * shift:halo - 2 * shift + tm, :] * cw_ref[0:1, :]
            + g_scr[halo - shift:halo - shift + tm, :] * cw_ref[1:2, :]
            + g_scr[halo:halo + tm, :] * cw_ref[2:3, :])
    h = 0.5 * conv * (1.0 + lax.erf(conv * math.sqrt(0.5))) * val
    acc_scr[...] += _dot(h.astype(BF16), wo_ref[...])
    tail_ref[...] = g_scr[halo + tm - tail_rows:halo + tm, :]

    @pl.when(j == pl.num_programs(1) - 1)
    def _():
        y = ALPHA * x_ref[...] + acc_scr[...]
        out_ref[...] = _layer_norm(y, lng_ref[...], lnb_ref[...])


def _prev_rows_spec(tm, rows, cols, grid_rank):
    per_tile = tm // rows
    if grid_rank == 1:
        return pl.BlockSpec((rows, cols), lambda i: (jnp.maximum(i * per_tile - 1, 0), 0))
    return pl.BlockSpec((rows, cols), lambda i, j: (jnp.maximum(i * per_tile - 1, 0), 0))


def _ffn_call(x, gate_prev, weights, conv, ln, layer, *, tm, tf, shift, tiles_per_seq):
    cw, cb = conv
    lng, lnb = ln
    n = x.shape[0]
    nt = n // tm
    nf = D_FF // tf
    halo_is_gate = gate_prev is not None
    if halo_is_gate:
        halo_rows = tail_rows = gate_prev.shape[0]
        halo, halo_spec = gate_prev, pl.BlockSpec((halo_rows, tf), lambda i, j: (0, j))
        x_rows = tm
        tail_shape = (tail_rows, D_FF)
        tail_spec = pl.BlockSpec((tail_rows, tf), lambda i, j: (0, j))
        w_in, w_out = weights
        w_args = (w_in, w_in, w_out)
        w_specs = [pl.BlockSpec((None, D_MODEL, tf), lambda i, j: (layer, 0, j)),
                   pl.BlockSpec((None, D_MODEL, tf), lambda i, j: (layer, 0, j + nf)),
                   pl.BlockSpec((None, tf, D_MODEL), lambda i, j: (layer, j, 0))]
        extra_specs = [pl.BlockSpec((D_MODEL, tf), lambda i, j: (0, j)),
                       pl.BlockSpec((D_MODEL, tf), lambda i, j: (0, j)),
                       pl.BlockSpec((tf, D_MODEL), lambda i, j: (j, 0))]
        extra_shapes = [jax.ShapeDtypeStruct((D_MODEL, D_FF), BF16),
                        jax.ShapeDtypeStruct((D_MODEL, D_FF), BF16),
                        jax.ShapeDtypeStruct((D_FF, D_MODEL), BF16)]
    else:
        halo_rows, tail_rows = BF16_ROWS, SUBLANES
        halo, halo_spec = x, _prev_rows_spec(tm, halo_rows, D_MODEL, 2)
        x_rows = tm + halo_rows
        tail_shape = (nt, tail_rows, D_FF)
        tail_spec = pl.BlockSpec((None, tail_rows, tf), lambda i, j: (i, 0, j))
        w_args = weights
        w_specs = [pl.BlockSpec((D_MODEL, tf), lambda i, j: (0, j)),
                   pl.BlockSpec((D_MODEL, tf), lambda i, j: (0, j)),
                   pl.BlockSpec((tf, D_MODEL), lambda i, j: (j, 0))]
        extra_specs, extra_shapes = [], []
    body = functools.partial(_ffn_body, tm=tm, halo=halo_rows, shift=shift,
                             halo_is_gate=halo_is_gate, tail_rows=tail_rows,
                             tiles_per_seq=tiles_per_seq, emit_bf16=halo_is_gate)
    ln_row = 2 * layer + 1
    outs = pl.pallas_call(
        body,
        grid=(nt, nf),
        in_specs=[
            pl.BlockSpec((tm, D_MODEL), lambda i, j: (i, 0)),
            halo_spec,
            *w_specs,
            pl.BlockSpec((None, CONV_W, tf), lambda i, j: (layer, 0, j)),
            pl.BlockSpec((None, 1, tf), lambda i, j: (layer, 0, j)),
            pl.BlockSpec((None, 1, D_MODEL), lambda i, j: (ln_row, 0, 0)),
            pl.BlockSpec((None, 1, D_MODEL), lambda i, j: (ln_row, 0, 0)),
        ],
        out_specs=[pl.BlockSpec((tm, D_MODEL), lambda i, j: (i, 0)), tail_spec, *extra_specs],
        out_shape=[jax.ShapeDtypeStruct((n, D_MODEL), F32),
                   jax.ShapeDtypeStruct(tail_shape, F32), *extra_shapes],
        scratch_shapes=[pltpu.VMEM((x_rows, D_MODEL), BF16),
                        pltpu.VMEM((tm + halo_rows, tf), F32),
                        pltpu.VMEM((tm, D_MODEL), F32)],
        compiler_params=_params("parallel", "arbitrary"),
        name="conv_ffn_ln",
    )(x, halo, *w_args, cw, cb, lng, lnb)
    return outs[0], outs[1], tuple(outs[2:])


def _row_chunks(ref):
    rows = ref.shape[0]
    step = min(rows, ROW_CHUNK)
    return [slice(r, r + step) for r in range(0, rows, step)]


def _proj_ln_body(a_ref, x_ref, w_ref, bias_ref, lng_ref, lnb_ref, out_ref):
    for rows in _row_chunks(a_ref):
        h = _dot(a_ref[rows, :], w_ref[...]) + bias_ref[...]
        out_ref[rows, :] = _layer_norm(ALPHA * x_ref[rows, :] + h, lng_ref[...], lnb_ref[...])


def _proj_ln_call(a, x, w, bias, lng, lnb, tm):
    n, k = a.shape
    return pl.pallas_call(
        _proj_ln_body,
        grid=(n // tm,),
        in_specs=[_row_spec(tm, k), _row_spec(tm, D_MODEL), _const_spec(w.shape),
                  _const_spec((1, D_MODEL)), _const_spec((1, D_MODEL)), _const_spec((1, D_MODEL))],
        out_specs=_row_spec(tm, D_MODEL),
        out_shape=jax.ShapeDtypeStruct((n, D_MODEL), F32),
        compiler_params=_params("parallel"),
        name="proj_res_ln",
    )(a, x, w, bias, lng, lnb)


POOL_HALO = 4 * SUBLANES


def _pool_prompt_body(x_ref, halo_ref, w_ref, scale_ref, lng_ref, lnb_ref, out_ref, ext_scr,
                      tmp_scr, *, tm, tiles_per_seq):
    i = pl.program_id(0)
    hb = POOL_HALO
    end = hb + tm
    ext_scr[0:hb, :] = halo_ref[...] * (i % tiles_per_seq != 0).astype(F32)
    ext_scr[hb:end, :] = x_ref[...]
    pos = (i % tiles_per_seq) * tm + lax.broadcasted_iota(jnp.int32, (tm, 1), 0)
    for g, wnd in enumerate(POOL_WINDOWS):
        c0 = g * POOL_GC
        stages = wnd.bit_length() - 1
        src, cols = ext_scr, slice(c0, c0 + POOL_GC)
        for k in range(1, stages + 1):
            w = 2 ** (k - 1)
            start = hb if k == stages else SUBLANES * k
            win = src[start:end, cols] + src[start - w:end - w, cols]
            if k < stages:
                dst = tmp_scr.at[k % 2]
                dst[start:end, :] = win
                src, cols = dst, slice(0, POOL_GC)
        xg = x_ref[:, c0:c0 + POOL_GC]
        inv_cnt = 1.0 / jnp.minimum(pos + 1, wnd).astype(F32)
        pooled = win * inv_cnt - xg
        y = _dot(pooled.astype(BF16), w_ref[g]) * scale_ref[:, c0:c0 + POOL_GC]
        ext_scr[hb:end, c0:c0 + POOL_GC] = ALPHA * xg + y
    out_ref[...] = _layer_norm(ext_scr[hb:end, :], lng_ref[...], lnb_ref[...])


def _pool_prompt_call(x, w, scale, lng, lnb, tm, tiles_per_seq):
    n = x.shape[0]
    hb = POOL_HALO
    body = functools.partial(_pool_prompt_body, tm=tm, tiles_per_seq=tiles_per_seq)
    return pl.pallas_call(
        body,
        grid=(n // tm,),
        in_specs=[_row_spec(tm, D_MODEL),
                  _prev_rows_spec(tm, hb, D_MODEL, 1),
                  _const_spec(w.shape), _const_spec((1, D_MODEL)),
                  _const_spec((1, D_MODEL)), _const_spec((1, D_MODEL))],
        out_specs=_row_spec(tm, D_MODEL),
        out_shape=jax.ShapeDtypeStruct((n, D_MODEL), F32),
        scratch_shapes=[pltpu.VMEM((tm + hb, D_MODEL), F32),
                        pltpu.VMEM((2, tm + hb, POOL_GC), F32)],
        compiler_params=_params("parallel"),
        name="pool_mix_ln_prompt",
    )(x, x, w, scale, lng, lnb)


def _pool_sample_body(prev_ref, x_ref, w_ref, scale_ref, lng_ref, lnb_ref, out_ref, state_ref,
                      y_scr, *, t_new, bt):
    def row(r, cols):
        return prev_ref[r, :, cols] if r < POOL_BUF else x_ref[r - POOL_BUF, :, cols]

    for r in range(POOL_BUF):
        state_ref[r] = row(r + t_new, slice(None))

    for g, wnd in enumerate(POOL_WINDOWS):
        cols = slice(g * POOL_GC, (g + 1) * POOL_GC)
        pooled = []
        for t in range(t_new):
            win = x_ref[t, :, cols]
            for k in range(1, wnd):
                win = win + row(POOL_BUF + t - k, cols)
            pooled.append(win / float(wnd) - x_ref[t, :, cols])
        pooled = jnp.concatenate(pooled, axis=0)
        y = _dot(pooled.astype(BF16), w_ref[g]) * scale_ref[:, cols]
        for t in range(t_new):
            y_scr[t, :, cols] = ALPHA * x_ref[t, :, cols] + y[t * bt:(t + 1) * bt]
    for t in range(t_new):
        out_ref[t] = _layer_norm(y_scr[t], lng_ref[...], lnb_ref[...])


def _pool_sample_call(prev_all, layer, x, w, scale, lng, lnb, bt):
    t_new, nb, _ = x.shape
    body = functools.partial(_pool_sample_body, t_new=t_new, bt=bt)
    return pl.pallas_call(
        body,
        grid=(nb // bt,),
        in_specs=[pl.BlockSpec((None, POOL_BUF, bt, D_MODEL), lambda i: (layer, 0, i, 0)),
                  pl.BlockSpec((t_new, bt, D_MODEL), lambda i: (0, i, 0)),
                  _const_spec(w.shape), _const_spec((1, D_MODEL)),
                  _const_spec((1, D_MODEL)), _const_spec((1, D_MODEL))],
        out_specs=[pl.BlockSpec((t_new, bt, D_MODEL), lambda i: (0, i, 0)),
                   pl.BlockSpec((POOL_BUF, bt, D_MODEL), lambda i: (0, i, 0))],
        out_shape=[jax.ShapeDtypeStruct((t_new, nb, D_MODEL), F32),
                   jax.ShapeDtypeStruct((POOL_BUF, nb, D_MODEL), F32)],
        scratch_shapes=[pltpu.VMEM((t_new, bt, D_MODEL), F32)],
        compiler_params=_params("parallel"),
        name="pool_mix_ln_sample",
    )(prev_all, x, w, scale, lng, lnb)


def _rope_pair(pair, cs):
    t = pair * cs
    return t + pltpu.roll(t, MLA_ROPE, 1)


def _mla_a_body(x_ref, cs_ref, w_ref, gq_ref, gkv_ref, cq_ref, ckv_ref, kpe_ref, kpad_ref):
    for rows in _row_chunks(x_ref):
        a = _dot(x_ref[rows, :].astype(BF16), w_ref[...])
        cq_ref[rows, :] = _rms_norm(a[:, :MLA_Q_RANK], gq_ref[...]).astype(BF16)
        ckv_ref[rows, :] = _rms_norm(a[:, MLA_Q_RANK:MLA_Q_RANK + MLA_KV_RANK], gkv_ref[...])
        r = _rope_pair(a[:, MLA_Q_RANK + MLA_KV_RANK:], cs_ref[rows, :])
        kpe_ref[rows, :] = r[:, :MLA_ROPE]
        lane = lax.broadcasted_iota(jnp.int32, r.shape, 1)
        kpad_ref[rows, :] = jnp.where(lane < MLA_ROPE, r, 0.0).astype(BF16)


def _mla_a_call(x, cs, w, gq, gkv, tm, tiles_per_seq):
    n = x.shape[0]
    return pl.pallas_call(
        _mla_a_body,
        grid=(n // tm,),
        in_specs=[_row_spec(tm, D_MODEL),
                  pl.BlockSpec((tm, LANES), lambda i: (i % tiles_per_seq, 0)),
                  _const_spec(w.shape), _const_spec((1, MLA_Q_RANK)),
                  _const_spec((1, MLA_KV_RANK))],
        out_specs=[_row_spec(tm, MLA_Q_RANK), _row_spec(tm, MLA_KV_RANK),
                   _row_spec(tm, MLA_ROPE), _row_spec(tm, LANES)],
        out_shape=[jax.ShapeDtypeStruct((n, MLA_Q_RANK), BF16),
                   jax.ShapeDtypeStruct((n, MLA_KV_RANK), F32),
                   jax.ShapeDtypeStruct((n, MLA_ROPE), F32),
                   jax.ShapeDtypeStruct((n, LANES), BF16)],
        compiler_params=_params("parallel"),
        name="mla_down_proj",
    )(x, cs, w, gq, gkv)


MLA_QW = MLA_NOPE + 2 * MLA_ROPE


def _mla_q_body(cq_ref, cs_ref, w_ref, q_ref):
    cs = cs_ref[...]
    for h in range(MLA_HEADS):
        c0 = h * MLA_QW
        qh = _dot(cq_ref[...], w_ref[:, c0:c0 + MLA_QW]) * MLA_EXP2_SCALE
        q_ref[:, c0:c0 + MLA_NOPE] = qh[:, :MLA_NOPE].astype(BF16)
        q_ref[:, c0 + MLA_NOPE:c0 + MLA_QW] = _rope_pair(qh[:, MLA_NOPE:], cs).astype(BF16)


def _mla_q_call(cq, cs, w, tm, tiles_per_seq):
    n = cq.shape[0]
    return pl.pallas_call(
        _mla_q_body,
        grid=(n // tm,),
        in_specs=[_row_spec(tm, MLA_Q_RANK),
                  pl.BlockSpec((tm, LANES), lambda i: (i % tiles_per_seq, 0)),
                  _const_spec(w.shape)],
        out_specs=_row_spec(tm, MLA_HEADS * MLA_QW),
        out_shape=jax.ShapeDtypeStruct((n, MLA_HEADS * MLA_QW), BF16),
        compiler_params=_params("parallel"),
        name="mla_q_proj",
    )(cq, cs, w)


def _mla_kv_up_body(ckv_ref, kpad_ref, wuk_ref, wuv_t_ref, k_ref, vt_ref):
    ckv = ckv_ref[...].astype(BF16)
    vt_ref[...] = _dot_nt(wuv_t_ref[...], ckv).astype(BF16)
    k_nope = _dot(ckv, wuk_ref[...]).astype(BF16)
    for h in range(MLA_HEADS):
        c0 = h * MLA_QW
        k_ref[:, c0:c0 + MLA_NOPE] = k_nope[:, h * MLA_NOPE:(h + 1) * MLA_NOPE]
        k_ref[:, c0 + MLA_NOPE:c0 + MLA_QW] = kpad_ref[...]


def _mla_kv_up_call(ckv, kpad, wuk, wuv_t, tm):
    n = ckv.shape[0]
    return pl.pallas_call(
        _mla_kv_up_body,
        grid=(n // tm,),
        in_specs=[_row_spec(tm, MLA_KV_RANK), _row_spec(tm, LANES),
                  _const_spec(wuk.shape), _const_spec(wuv_t.shape)],
        out_specs=[_row_spec(tm, MLA_HEADS * MLA_QW),
                   pl.BlockSpec((MLA_HEADS * MLA_V, tm), lambda i: (0, i))],
        out_shape=[jax.ShapeDtypeStruct((n, MLA_HEADS * MLA_QW), BF16),
                   jax.ShapeDtypeStruct((MLA_HEADS * MLA_V, n), BF16)],
        compiler_params=_params("parallel"),
        name="mla_kv_up_proj",
    )(ckv, kpad, wuk, wuv_t)


def _softmax_init(m_scr, l_scr, acc_scr):
    m_scr[...] = jnp.full(m_scr.shape, NEG_INF, F32)
    l_scr[...] = jnp.zeros(l_scr.shape, F32)
    acc_scr[...] = jnp.zeros(acc_scr.shape, F32)


LOG2E = math.log2(math.e)
MLA_EXP2_SCALE = MLA_SCALE * LOG2E


def _flash_update(s, v, m_ref, l_ref, acc_ref):
    m_prev = m_ref[...]
    m_new = jnp.maximum(m_prev, jnp.max(s, axis=-1, keepdims=True))
    corr = jnp.exp2(m_prev - m_new)
    p = jnp.exp2(s - m_new)
    l_ref[...] = l_ref[...] * corr + jnp.sum(p, axis=-1, keepdims=True)
    acc_ref[...] = acc_ref[...] * corr + _dot(p.astype(BF16), v)
    m_ref[...] = m_new


def _flash_update_t(s_t, v_t, m_ref, l_ref, acc_ref):
    m_prev = m_ref[...]
    m_new = jnp.maximum(m_prev, jnp.max(s_t, axis=0, keepdims=True))
    corr = jnp.exp2(m_prev - m_new)
    p = jnp.exp2(s_t - m_new)
    l_ref[...] = l_ref[...] * corr + jnp.sum(p, axis=0, keepdims=True)
    acc_ref[...] = acc_ref[...] * corr + _dot(v_t, p.astype(BF16))
    m_ref[...] = m_new


def _mla_flash_body(q_ref, k_ref, vt_ref, o_ref, s_scr, m_scr, l_scr, acc_scr, *, t, heads):
    qi = pl.program_id(2)

    def scores(hh, ki):
        k = k_ref[pl.ds(pl.multiple_of(ki * t, t), t), hh * MLA_QW:(hh + 1) * MLA_QW]
        return _dot_nt(k, q_ref[:, hh * MLA_QW:(hh + 1) * MLA_QW])

    def values(hh, ki):
        return vt_ref[hh * MLA_V:(hh + 1) * MLA_V, pl.ds(pl.multiple_of(ki * t, t), t)]

    def advance(ki, slot):
        for hh in range(heads):
            s_scr[hh, 1 - slot] = scores(hh, ki + 1)
        for hh in range(heads):
            _flash_update_t(s_scr[hh, slot], values(hh, ki), m_scr.at[hh], l_scr.at[hh],
                            acc_scr.at[hh])

    def finish(slot):
        key = lax.broadcasted_iota(jnp.int32, (t, t), 0)
        query = lax.broadcasted_iota(jnp.int32, (t, t), 1)
        for hh in range(heads):
            s = jnp.where(key <= query, s_scr[hh, slot], NEG_INF)
            _flash_update_t(s, values(hh, qi), m_scr.at[hh], l_scr.at[hh], acc_scr.at[hh])
            o_t = acc_scr[hh] / l_scr[hh]
            o_ref[:, hh * MLA_V:(hh + 1) * MLA_V] = o_t.T.astype(BF16)

    for hh in range(heads):
        _softmax_init(m_scr.at[hh], l_scr.at[hh], acc_scr.at[hh])
        s_scr[hh, 0] = scores(hh, 0)

    def pair_step(kk, carry):
        advance(2 * kk, 0)
        advance(2 * kk + 1, 1)
        return carry

    lax.fori_loop(0, qi // 2, pair_step, 0)

    @pl.when(qi % 2 == 0)
    def _():
        finish(0)

    @pl.when(qi % 2 == 1)
    def _():
        advance(qi - 1, 0)
        finish(1)


def _mla_flash_call(q, k, v_t, batch, seq):
    t, heads = FLASH_T, FLASH_HEADS
    nq = seq // t
    body = functools.partial(_mla_flash_body, t=t, heads=heads)
    return pl.pallas_call(
        body,
        grid=(batch, MLA_HEADS // heads, nq),
        in_specs=[pl.BlockSpec((t, heads * MLA_QW), lambda b, h, i: (b * nq + i, h)),
                  pl.BlockSpec((seq, heads * MLA_QW), lambda b, h, i: (b, h)),
                  pl.BlockSpec((heads * MLA_V, seq), lambda b, h, i: (h, b))],
        out_specs=pl.BlockSpec((t, heads * MLA_V), lambda b, h, i: (b * nq + i, h)),
        out_shape=jax.ShapeDtypeStruct((batch * seq, MLA_HEADS * MLA_V), BF16),
        scratch_shapes=[pltpu.VMEM((heads, 2, t, t), F32),
                        pltpu.VMEM((heads, 1, t), F32), pltpu.VMEM((heads, 1, t), F32),
                        pltpu.VMEM((heads, MLA_V, t), F32)],
        compiler_params=_params("parallel", "parallel", "arbitrary"),
        name="mla_prompt_attention",
    )(q, k, v_t)


MLA_LATQ = MLA_KV_RANK + 2 * MLA_ROPE


def _mla_absorb_q_body(q_ref, wuk_ref, out_ref):
    q = q_ref[...]
    out_ref[:, :MLA_KV_RANK] = _dot_nt(q[:, :MLA_NOPE], wuk_ref[...]).astype(BF16)
    out_ref[:, MLA_KV_RANK:] = q[:, MLA_NOPE:]


def _mla_absorb_q_call(q, wuk):
    n = q.shape[0]
    return pl.pallas_call(
        _mla_absorb_q_body,
        grid=(MLA_HEADS,),
        in_specs=[pl.BlockSpec((n, MLA_QW), lambda h: (0, h)),
                  pl.BlockSpec((MLA_KV_RANK, MLA_NOPE), lambda h: (0, h))],
        out_specs=pl.BlockSpec((n, MLA_LATQ), lambda h: (0, h)),
        out_shape=jax.ShapeDtypeStruct((n, MLA_HEADS * MLA_LATQ), BF16),
        compiler_params=_params("parallel"),
        name="mla_absorb_q",
    )(q, wuk)


def _mla_decode_body(pt_ref, q_ref, nck_ref, nkp_ref, ck_hbm, kp_hbm, o_ref, ck_buf, kp_buf, sem,
                     ck_scr, kp_scr, m_scr, l_scr, acc_scr, *, pages, chains, t_new, layer):
    b = pl.program_id(0)
    step = pl.program_id(1)
    n_steps = pl.num_programs(1)
    idx = b * n_steps + step
    slot = idx % 2
    per_chain = pages // chains
    span = per_chain * PAGE_SIZE

    def page_copies(page_of, sl):
        copies = []
        for g in range(pages):
            page = page_of(g)
            copies.append(pltpu.make_async_copy(ck_hbm.at[layer, page], ck_buf.at[sl, g],
                                                sem.at[0, sl]))
            copies.append(pltpu.make_async_copy(kp_hbm.at[layer, page], kp_buf.at[sl, g],
                                                sem.at[1, sl]))
        return copies

    @pl.when(idx == 0)
    def _():
        for cp in page_copies(lambda g: pt_ref[0, g], 0):
            cp.start()

    @pl.when(idx + 1 < pl.num_programs(0) * n_steps)
    def _():
        wrap = step == n_steps - 1
        nb = jnp.where(wrap, b + 1, b)
        ns = jnp.where(wrap, 0, step + 1)
        for cp in page_copies(lambda g: pt_ref[nb, ns * pages + g], 1 - slot):
            cp.start()

    for cp in page_copies(lambda g: 0, slot):
        cp.wait()

    @pl.when(step == 0)
    def _():
        _softmax_init(m_scr, l_scr, acc_scr)

    q_lat = q_ref[0, :, :MLA_KV_RANK]
    q_pe = q_ref[0, :, MLA_KV_RANK:MLA_KV_RANK + MLA_ROPE]
    scores = []
    for c in range(chains):
        for g in range(c * per_chain, (c + 1) * per_chain):
            ck_scr[g * PAGE_SIZE:(g + 1) * PAGE_SIZE, :] = ck_buf[slot, g].astype(BF16)
            kp_scr[:, g * PAGE_SIZE:(g + 1) * PAGE_SIZE] = kp_buf[slot, g].astype(BF16)
        scores.append(_dot_nt(q_lat, ck_scr[c * span:(c + 1) * span, :])
                      + _dot(q_pe, kp_scr[:, c * span:(c + 1) * span]))
    for c in range(chains):
        _flash_update(scores[c], ck_scr[c * span:(c + 1) * span, :], m_scr.at[c], l_scr.at[c],
                      acc_scr.at[c])

    @pl.when(step == pl.num_programs(1) - 1)
    def _():
        for c in range(1, chains):
            m = jnp.maximum(m_scr[0], m_scr[c])
            w0 = jnp.exp2(m_scr[0] - m)
            wc = jnp.exp2(m_scr[c] - m)
            l_scr[0] = l_scr[0] * w0 + l_scr[c] * wc
            acc_scr[0] = acc_scr[0] * w0 + acc_scr[c] * wc
            m_scr[0] = m
        pad = PAGE_SIZE - nck_ref.shape[1]
        nck = jnp.concatenate([nck_ref[0], jnp.zeros((pad, MLA_KV_RANK), F32)],
                              axis=0).astype(BF16)
        nkp = jnp.concatenate([nkp_ref[0], jnp.zeros((pad, MLA_ROPE), F32)],
                              axis=0).astype(BF16)
        s2 = _dot_nt(q_lat, nck) + _dot_nt(q_pe, nkp)
        tq = lax.broadcasted_iota(jnp.int32, s2.shape, 0) // MLA_HEADS
        tk = lax.broadcasted_iota(jnp.int32, s2.shape, 1)
        s2 = jnp.where((tk <= tq) & (tk < t_new), s2, NEG_INF)
        _flash_update(s2, nck, m_scr.at[0], l_scr.at[0], acc_scr.at[0])
        o_ref[0] = (acc_scr[0] / l_scr[0]).astype(BF16)


def _mla_decode_call(page_table, q, new_ckv, new_kpe, cache_ckv, cache_kpe_t, layer, t_new):
    nb, rows, _ = q.shape
    n_pages = page_table.shape[1]
    pages, chains = DECODE_PAGES, DECODE_CHAINS
    pad_rows = new_ckv.shape[1]

    body = functools.partial(_mla_decode_body, pages=pages, chains=chains, t_new=t_new,
                             layer=layer)
    grid_spec = pltpu.PrefetchScalarGridSpec(
        num_scalar_prefetch=1,
        grid=(nb, n_pages // pages),
        in_specs=[pl.BlockSpec((1, rows, MLA_LATQ), lambda b, s, pt: (b, 0, 0)),
                  pl.BlockSpec((1, pad_rows, MLA_KV_RANK), lambda b, s, pt: (b, 0, 0)),
                  pl.BlockSpec((1, pad_rows, MLA_ROPE), lambda b, s, pt: (b, 0, 0)),
                  pl.BlockSpec(memory_space=pl.ANY),
                  pl.BlockSpec(memory_space=pl.ANY)],
        out_specs=pl.BlockSpec((1, rows, MLA_KV_RANK), lambda b, s, pt: (b, 0, 0)),
        scratch_shapes=[pltpu.VMEM((2, pages, PAGE_SIZE, MLA_KV_RANK), F32),
                        pltpu.VMEM((2, pages, MLA_ROPE, PAGE_SIZE), F32),
                        pltpu.SemaphoreType.DMA((2, 2)),
                        pltpu.VMEM((pages * PAGE_SIZE, MLA_KV_RANK), BF16),
                        pltpu.VMEM((MLA_ROPE, pages * PAGE_SIZE), BF16),
                        pltpu.VMEM((chains, rows, 1), F32), pltpu.VMEM((chains, rows, 1), F32),
                        pltpu.VMEM((chains, rows, MLA_KV_RANK), F32)],
    )
    return pl.pallas_call(
        body,
        grid_spec=grid_spec,
        out_shape=jax.ShapeDtypeStruct((nb, rows, MLA_KV_RANK), BF16),
        compiler_params=_params("arbitrary", "arbitrary"),
        name="mla_paged_decode",
    )(page_table, q, new_ckv, new_kpe, cache_ckv, cache_kpe_t)


def _mla_absorb_o_body(o_ref, wuv_ref, out_ref):
    out_ref[...] = _dot(o_ref[0], wuv_ref[...]).astype(BF16)


def _mla_absorb_o_call(o_lat, wuv):
    _, n, _ = o_lat.shape
    return pl.pallas_call(
        _mla_absorb_o_body,
        grid=(MLA_HEADS,),
        in_specs=[pl.BlockSpec((1, n, MLA_KV_RANK), lambda h: (h, 0, 0)),
                  pl.BlockSpec((MLA_KV_RANK, MLA_V), lambda h: (0, h))],
        out_specs=pl.BlockSpec((n, MLA_V), lambda h: (0, h)),
        out_shape=jax.ShapeDtypeStruct((n, MLA_HEADS * MLA_V), BF16),
        compiler_params=_params("parallel"),
        name="mla_absorb_o",
    )(o_lat, wuv)


SWA_NQ = SWA_HEADS * SWA_HEAD_DIM
SWA_NKV = SWA_KV_HEADS * SWA_HEAD_DIM
SWA_PAIRS = SWA_KV_HEADS // 2


def _swa_regroup(a, axis):
    shape = a.shape
    split = shape[:axis] + (SWA_PAIRS, 2, SWA_GROUP, SWA_HEAD_DIM) + shape[axis + 1:]
    return jnp.swapaxes(a.reshape(split), axis + 1, axis + 2).reshape(shape)


def _swa_slope(head):
    return 2.0 ** (-8.0 * (head + 1) / SWA_HEADS)


def _swa_qkv_body(x_ref, w_ref, b_ref, q_ref, k_ref, v_ref):
    for rows in _row_chunks(x_ref):
        qkv = _dot(x_ref[rows, :].astype(BF16), w_ref[...]) + b_ref[...]
        q_ref[rows, :] = qkv[:, :SWA_NQ].astype(BF16)
        k_ref[rows, :] = qkv[:, SWA_NQ:SWA_NQ + SWA_NKV]
        v_ref[rows, :] = qkv[:, SWA_NQ + SWA_NKV:]


def _swa_qkv_call(x, w, b, tm):
    n = x.shape[0]
    return pl.pallas_call(
        _swa_qkv_body,
        grid=(n // tm,),
        in_specs=[_row_spec(tm, D_MODEL), _const_spec(w.shape), _const_spec(b.shape)],
        out_specs=[_row_spec(tm, SWA_NQ), _row_spec(tm, SWA_NKV), _row_spec(tm, SWA_NKV)],
        out_shape=[jax.ShapeDtypeStruct((n, SWA_NQ), BF16),
                   jax.ShapeDtypeStruct((n, SWA_NKV), F32),
                   jax.ShapeDtypeStruct((n, SWA_NKV), F32)],
        compiler_params=_params("parallel"),
        name="swa_qkv_proj",
    )(x, w, b)


def _swa_heads(q_blk, k_pair, v_pair, sink_ref, p, g, lower, dist, valid):
    lane_q = lax.broadcasted_iota(jnp.int32, q_blk.shape, 1)
    outs = []
    for half in range(2):
        head = (2 * p + half) * SWA_GROUP + g
        in_half = (lane_q >= SWA_HEAD_DIM) if half else (lane_q < SWA_HEAD_DIM)
        qm = jnp.where(in_half, q_blk, jnp.zeros_like(q_blk))
        raw = _dot_nt(qm, k_pair)
        s = jnp.where(lower, raw[:, WINDOW:], raw[:, :WINDOW])
        s = s * (SWA_SCALE * LOG2E) - (_swa_slope(head) * LOG2E) * dist
        s = jnp.where(valid, s, NEG_INF)
        sink = sink_ref[head] * LOG2E
        m = jnp.maximum(jnp.max(s, axis=-1, keepdims=True), sink)
        e = jnp.exp2(s - m)
        denom = jnp.sum(e, axis=-1, keepdims=True) + jnp.exp2(sink - m)
        zero = jnp.zeros_like(e)
        pe = jnp.concatenate([jnp.where(lower, zero, e), jnp.where(lower, e, zero)], axis=1)
        outs.append(_dot(pe.astype(BF16), v_pair) * (1.0 / denom))
    lane_o = lax.broadcasted_iota(jnp.int32, outs[0].shape, 1)
    return jnp.where(lane_o < SWA_HEAD_DIM, outs[0], outs[1])


def _swa_prompt_body(sink_ref, q_ref, kc_ref, kp_ref, vc_ref, vp_ref, o_ref):
    i = pl.program_id(1)
    k = jnp.concatenate([kp_ref[...].astype(BF16), kc_ref[...].astype(BF16)], axis=0)
    v = jnp.concatenate([vp_ref[...].astype(BF16), vc_ref[...].astype(BF16)], axis=0)
    row = lax.broadcasted_iota(jnp.int32, (WINDOW, WINDOW), 0)
    col = lax.broadcasted_iota(jnp.int32, (WINDOW, WINDOW), 1)
    lower = col <= row
    distf = jnp.where(lower, row - col, row - col + WINDOW).astype(F32)
    valid = lower | (i > 0)
    for p in range(SWA_PAIRS):
        kpair = k[:, p * LANES:(p + 1) * LANES]
        vpair = v[:, p * LANES:(p + 1) * LANES]
        for g in range(SWA_GROUP):
            c0 = (p * SWA_GROUP + g) * LANES
            o = _swa_heads(q_ref[:, c0:c0 + LANES], kpair, vpair, sink_ref, p, g, lower, distf,
                           valid)
            o_ref[:, c0:c0 + LANES] = o.astype(BF16)


def _swa_prompt_call(sinks, q, k, v, batch, seq):
    nb = seq // WINDOW
    cur = lambda b, i: (b * nb + i, 0)
    prev = lambda b, i: (b * nb + jnp.maximum(i - 1, 0), 0)
    return pl.pallas_call(
        _swa_prompt_body,
        grid=(batch, nb),
        in_specs=[pl.BlockSpec(memory_space=pltpu.SMEM),
                  pl.BlockSpec((WINDOW, SWA_NQ), cur),
                  pl.BlockSpec((WINDOW, SWA_NKV), cur), pl.BlockSpec((WINDOW, SWA_NKV), prev),
                  pl.BlockSpec((WINDOW, SWA_NKV), cur), pl.BlockSpec((WINDOW, SWA_NKV), prev)],
        out_specs=pl.BlockSpec((WINDOW, SWA_NQ), cur),
        out_shape=jax.ShapeDtypeStruct((batch * seq, SWA_NQ), BF16),
        compiler_params=_params("parallel", "arbitrary"),
        name="swa_prompt_attention",
    )(sinks, q, k, k, v, v)


def _swa_sample_body(sink_ref, q_ref, kc_ref, vc_ref, kn_ref, vn_ref, o_ref, ko_ref, vo_ref,
                     *, bt, t_pad, t_new):
    n_keys = 2 * WINDOW
    fill = jnp.zeros((WINDOW - t_pad, LANES), F32)
    rows = 2 * SWA_GROUP * t_pad
    ridx = lax.broadcasted_iota(jnp.int32, (rows, 1), 0) // t_pad
    tq = lax.broadcasted_iota(jnp.int32, (rows, n_keys), 0) % t_pad
    col = lax.broadcasted_iota(jnp.int32, (rows, n_keys), 1)
    dist = tq + WINDOW - col
    valid = (dist >= 0) & (dist < WINDOW)
    distf = dist.astype(F32)
    lane = lax.broadcasted_iota(jnp.int32, (t_pad, LANES), 1)
    bias, sink_cols = [], []
    for p in range(SWA_PAIRS):
        slope = jnp.zeros((rows, 1), F32)
        sink = jnp.zeros((rows, 1), F32)
        for half in range(2):
            for g in range(SWA_GROUP):
                head = (2 * p + half) * SWA_GROUP + g
                here = ridx == half * SWA_GROUP + g
                slope = jnp.where(here, _swa_slope(head), slope)
                sink = jnp.where(here, sink_ref[head], sink)
        bias.append(slope * distf)
        sink_cols.append(sink)

    def one_seq(bb, carry):
        scores = []
        for p in range(SWA_PAIRS):
            pair = slice(p * LANES, (p + 1) * LANES)
            k_old = kc_ref[bb, pair, :].astype(BF16)
            k_new = jnp.concatenate([kn_ref[bb, :, pair], fill], axis=0).astype(BF16)
            blocks = [q_ref[bb, :, (p * SWA_GROUP + g) * LANES:(p * SWA_GROUP + g + 1) * LANES]
                      .astype(F32) for g in range(SWA_GROUP)]
            stack = ([jnp.where(lane < SWA_HEAD_DIM, blk, 0.0) for blk in blocks]
                     + [jnp.where(lane >= SWA_HEAD_DIM, blk, 0.0) for blk in blocks])
            qs = jnp.concatenate(stack, axis=0).astype(BF16)
            scores.append(jnp.concatenate([_dot(qs, k_old), _dot_nt(qs, k_new)], axis=1))
        probs = []
        for p in range(SWA_PAIRS):
            s = scores[p] * SWA_SCALE - bias[p]
            s = jnp.where(valid, s, NEG_INF)
            m = jnp.maximum(jnp.max(s, axis=-1, keepdims=True), sink_cols[p])
            e = jnp.exp(s - m)
            denom = jnp.sum(e, axis=-1, keepdims=True) + jnp.exp(sink_cols[p] - m)
            probs.append((e / denom).astype(BF16))
        for p in range(SWA_PAIRS):
            pair = slice(p * LANES, (p + 1) * LANES)
            v_old = vc_ref[bb, pair, :].astype(BF16)
            v_new = jnp.concatenate([vn_ref[bb, :, pair], fill], axis=0).astype(BF16)
            pr = probs[p]
            o = _dot_nt(pr[:, :WINDOW], v_old) + _dot(pr[:, WINDOW:], v_new)
            for g in range(SWA_GROUP):
                lo = o[g * t_pad:(g + 1) * t_pad]
                hi = o[(SWA_GROUP + g) * t_pad:(SWA_GROUP + g + 1) * t_pad]
                c0 = (p * SWA_GROUP + g) * LANES
                o_ref[bb, :, c0:c0 + LANES] = jnp.where(lane < SWA_HEAD_DIM, lo, hi).astype(BF16)
        tcol = lax.broadcasted_iota(jnp.int32, (LANES, WINDOW), 1)
        for old_ref, new_ref, out_ref in ((kc_ref, kn_ref, ko_ref), (vc_ref, vn_ref, vo_ref)):
            for p in range(SWA_PAIRS):
                pair = slice(p * LANES, (p + 1) * LANES)
                new_t = jnp.concatenate([new_ref[bb, :, pair], fill], axis=0).T
                merged = jnp.where(tcol < t_new, new_t, old_ref[bb, pair, :])
                out_ref[bb, pair, :] = pltpu.roll(merged, WINDOW - t_new, 1)
        return carry

    lax.fori_loop(0, bt, one_seq, 0)


def _swa_sample_call(sinks, q, kc, vc, kn, vn, t_new):
    nb, t_pad, _ = q.shape
    bt = SWA_SEQ_TILE
    body = functools.partial(_swa_sample_body, bt=bt, t_pad=t_pad, t_new=t_new)
    seq3 = lambda rows, cols: pl.BlockSpec((bt, rows, cols), lambda i: (i, 0, 0))
    return pl.pallas_call(
        body,
        grid=(nb // bt,),
        in_specs=[pl.BlockSpec(memory_space=pltpu.SMEM),
                  seq3(t_pad, SWA_NQ), seq3(SWA_NKV, WINDOW), seq3(SWA_NKV, WINDOW),
                  seq3(t_pad, SWA_NKV), seq3(t_pad, SWA_NKV)],
        out_specs=[seq3(t_pad, SWA_NQ), seq3(SWA_NKV, WINDOW), seq3(SWA_NKV, WINDOW)],
        out_shape=[jax.ShapeDtypeStruct((nb, t_pad, SWA_NQ), BF16),
                   jax.ShapeDtypeStruct((nb, SWA_NKV, WINDOW), F32),
                   jax.ShapeDtypeStruct((nb, SWA_NKV, WINDOW), F32)],
        compiler_params=_params("parallel"),
        name="swa_sample_attention",
    )(sinks, q, kc, vc, kn, vn)


def _rot_cols(w):
    half = w.shape[-1] // 2
    return jnp.concatenate([-w[..., half:], w[..., :half]], axis=-1)


def _rope_table(pos):
    half = MLA_ROPE // 2
    inv = ROPE_THETA ** (-jnp.arange(half, dtype=F32) / half)
    ang = pos.astype(F32)[:, None] * inv[None, :]
    cos, sin = jnp.cos(ang), jnp.sin(ang)
    return jnp.concatenate([cos, cos, sin, sin], axis=-1)


def _time_major(a, nb, t):
    return jnp.swapaxes(a, 0, 1).reshape((t * nb,) + a.shape[2:])


def _batch_major(a, nb, t):
    return jnp.swapaxes(a.reshape((t, nb) + a.shape[1:]), 0, 1)


def kernel(x_prompt, x_sample, state_pool, cache_mla_ckv, cache_mla_kpe, page_table, cache_swa_k, cache_swa_v, state_ffn_conv, ln_g, ln_b, pool_w, pool_scale, mla_w_a, mla_g_q, mla_g_kv, mla_w_uq, mla_w_uk, mla_w_uv, mla_w_o, swa_w_qkv, swa_b_qkv, swa_sinks, swa_w_o, swa_b_o, ffn_w_in, ffn_conv_w, ffn_conv_b, ffn_w_out):
    B, S, D = x_prompt.shape
    NB, T, _ = x_sample.shape
    NS = NB * T
    tm = ROW_TILE
    tps = S // tm
    assert S % tm == 0 and NS == tm and S % FLASH_T == 0 and T <= SUBLANES

    xp = x_prompt.reshape(B * S, D)
    xs = _time_major(x_sample, NB, T)
    cs_p = _rope_table(jnp.arange(S))
    cs_s = jnp.repeat(_rope_table(PAST_LEN + jnp.arange(T)), NB, axis=0)
    zero_bias = jnp.zeros((1, D), F32)
    ffn_conv = (ffn_conv_w, ffn_conv_b.reshape(DEPTH, 1, D_FF))
    ln_rows = (ln_g.reshape(2 * DEPTH, 1, D), ln_b.reshape(2 * DEPTH, 1, D))

    pool_p, pool_s, ckv_p, ckv_s, kpe_p, kpe_s = [], [], [], [], [], []
    swk_p, swk_s, swv_p, swv_s, conv_p, conv_s = [], [], [], [], [], []

    for i in range(DEPTH):
        kind, j = i % N_MIXERS, i // N_MIXERS
        g1, b1 = ln_g[i, 0][None, :], ln_b[i, 0][None, :]
        if kind == 0:
            w = pool_w[j].astype(BF16)
            sc = pool_scale[j][None, :]
            pool_p.append(xp.reshape(B, S, D)[:, S - POOL_BUF:])
            prev_tm = jnp.swapaxes(state_pool, 1, 2)
            xs, state = _pool_sample_call(prev_tm, j, xs.reshape(T, NB, D), w, sc, g1, b1, 32)
            xs = xs.reshape(NS, D)
            pool_s.append(jnp.swapaxes(state, 0, 1))
            xp = _pool_prompt_call(xp, w, sc, g1, b1, tm, tps)
        elif kind == 1:
            w_a = mla_w_a[j]
            n_lat = MLA_Q_RANK + MLA_KV_RANK
            w_a_ext = jnp.concatenate([w_a, _rot_cols(w_a[:, n_lat:])], axis=1).astype(BF16)
            w_uq = mla_w_uq[j]
            w_uq_ext = jnp.concatenate([w_uq, _rot_cols(w_uq[..., MLA_NOPE:])], axis=-1)
            w_uq_ext = w_uq_ext.reshape(MLA_Q_RANK, MLA_HEADS * MLA_QW).astype(BF16)
            w_uk = mla_w_uk[j].reshape(MLA_KV_RANK, MLA_HEADS * MLA_NOPE).astype(BF16)
            w_uv = mla_w_uv[j].reshape(MLA_KV_RANK, MLA_HEADS * MLA_V).astype(BF16)
            w_o = mla_w_o[j].reshape(MLA_HEADS * MLA_V, D).astype(BF16)
            gq, gkv = mla_g_q[j][None, :], mla_g_kv[j][None, :]

            cq, ckv, kpe, kpad = _mla_a_call(xp, cs_p, w_a_ext, gq, gkv, tm, tps)
            q = _mla_q_call(cq, cs_p, w_uq_ext, tm, tps)
            k_full, v_full = _mla_kv_up_call(ckv, kpad, w_uk, w_uv.T, tm)
            o = _mla_flash_call(q, k_full, v_full, B, S)
            ckv_p.append(ckv.reshape(B, S, MLA_KV_RANK))
            kpe_p.append(kpe.reshape(B, S, MLA_ROPE))
            xp = _proj_ln_call(o, xp, w_o, zero_bias, g1, b1, tm)

            cq, ckv, kpe, _ = _mla_a_call(xs, cs_s, w_a_ext, gq, gkv, tm, 1)
            q = _mla_q_call(cq, cs_s, w_uq_ext, tm, 1)
            q_abs = _mla_absorb_q_call(q, w_uk)
            q_abs = _batch_major(q_abs.reshape(NS, MLA_HEADS, MLA_LATQ), NB, T)
            q_abs = q_abs.reshape(NB, T * MLA_HEADS, MLA_LATQ)
            ckv_bm = _batch_major(ckv, NB, T)
            kpe_bm = _batch_major(kpe, NB, T)
            pad = ((0, 0), (0, SUBLANES - T), (0, 0))
            o_lat = _mla_decode_call(page_table, q_abs, jnp.pad(ckv_bm, pad), jnp.pad(kpe_bm, pad),
                                     cache_mla_ckv, jnp.swapaxes(cache_mla_kpe, 2, 3), j, T)
            o_lat = o_lat.reshape(NB, T, MLA_HEADS, MLA_KV_RANK).transpose(2, 1, 0, 3)
            o = _mla_absorb_o_call(o_lat.reshape(MLA_HEADS, NS, MLA_KV_RANK), w_uv)
            ckv_s.append(ckv_bm)
            kpe_s.append(kpe_bm)
            xs = _proj_ln_call(o, xs, w_o, zero_bias, g1, b1, tm)
        else:
            w_qkv, b_qkv = swa_w_qkv[j], swa_b_qkv[j]
            w_qkv = jnp.concatenate([_swa_regroup(w_qkv[:, :SWA_NQ], 1), w_qkv[:, SWA_NQ:]],
                                    axis=1)
            b_qkv = jnp.concatenate([_swa_regroup(b_qkv[:SWA_NQ], 0), b_qkv[SWA_NQ:]])[None, :]
            w_qkv = w_qkv.astype(BF16)
            w_o = _swa_regroup(swa_w_o[j], 0).astype(BF16)
            b_o = swa_b_o[j][None, :]
            sinks = swa_sinks[j]

            q, k, v = _swa_qkv_call(xp, w_qkv, b_qkv, tm // 2)
            o = _swa_prompt_call(sinks, q, k, v, B, S)
            kv_shape = (B, WINDOW, SWA_KV_HEADS, SWA_HEAD_DIM)
            swk_p.append(k.reshape(B, S, SWA_NKV)[:, S - WINDOW:].reshape(kv_shape))
            swv_p.append(v.reshape(B, S, SWA_NKV)[:, S - WINDOW:].reshape(kv_shape))
            xp = _proj_ln_call(o, xp, w_o, b_o, g1, b1, tm)

            q, k, v = _swa_qkv_call(xs, w_qkv, b_qkv, tm // 2)
            pad = ((0, 0), (0, SUBLANES - T), (0, 0))
            q_bm = jnp.pad(_batch_major(q, NB, T), pad)
            k_bm, v_bm = _batch_major(k, NB, T), _batch_major(v, NB, T)
            kc = jnp.transpose(cache_swa_k[j], (0, 2, 3, 1)).reshape(NB, SWA_NKV, WINDOW)
            vc = jnp.transpose(cache_swa_v[j], (0, 2, 3, 1)).reshape(NB, SWA_NKV, WINDOW)
            o, k_upd, v_upd = _swa_sample_call(sinks, q_bm, kc, vc, jnp.pad(k_bm, pad),
                                               jnp.pad(v_bm, pad), T)
            o = _time_major(o[:, :T], NB, T)
            kv_shape = (NB, SWA_KV_HEADS, SWA_HEAD_DIM, WINDOW)
            swk_s.append(jnp.transpose(k_upd.reshape(kv_shape), (0, 3, 1, 2)))
            swv_s.append(jnp.transpose(v_upd.reshape(kv_shape), (0, 3, 1, 2)))
            xs = _proj_ln_call(o, xs, w_o, b_o, g1, b1, tm)

        prev = _time_major(state_ffn_conv[i], NB, CONV_W - 1)
        xs, tails, w_bf16 = _ffn_call(xs, prev, (ffn_w_in, ffn_w_out), ffn_conv, ln_rows, i,
                                      tm=tm, tf=FFN_SAMPLE_TF, shift=NB, tiles_per_seq=1)
        conv_s.append(_batch_major(tails, NB, CONV_W - 1))
        xp, tails, _ = _ffn_call(xp, None, w_bf16, ffn_conv, ln_rows, i, tm=tm, tf=FFN_TF,
                                 shift=1, tiles_per_seq=tps)
        conv_p.append(tails.reshape(B, tps, SUBLANES, D_FF)[:, -1, SUBLANES - (CONV_W - 1):])

    y_p = xp.reshape(B, S, D)
    y_s = _batch_major(xs, NB, T)
    return (y_p, y_s, jnp.stack(pool_p), jnp.stack(pool_s), jnp.stack(ckv_p), jnp.stack(ckv_s),
            jnp.stack(kpe_p), jnp.stack(kpe_s), jnp.stack(swk_p), jnp.stack(swk_s),
            jnp.stack(swv_p), jnp.stack(swv_s), jnp.stack(conv_p), jnp.stack(conv_s))
```

```python
import functools
import math

import jax
import jax.numpy as jnp
import numpy as np
from jax import lax
from jax.experimental import pallas as pl
from jax.experimental.pallas import tpu as pltpu

D_MODEL = 2048
DEPTH = 4
N_MIXERS = 3
PAST_LEN = 8192
PAGE_SIZE = 128

POOL_WINDOWS = (2, 4, 8, 16)
POOL_GROUPS = len(POOL_WINDOWS)
POOL_GC = D_MODEL // POOL_GROUPS
POOL_BUF = max(POOL_WINDOWS) - 1

MLA_HEADS = 16
MLA_Q_RANK = 512
MLA_KV_RANK = 512
MLA_NOPE = 128
MLA_ROPE = 64
MLA_V = 128
MLA_SCALE = 1.0 / math.sqrt(MLA_NOPE + MLA_ROPE)
ROPE_THETA = 10000.0

SWA_HEADS = 32
SWA_KV_HEADS = 8
SWA_GROUP = SWA_HEADS // SWA_KV_HEADS
SWA_HEAD_DIM = 64
SWA_SCALE = 1.0 / math.sqrt(SWA_HEAD_DIM)
WINDOW = 128

D_FF = 5632
CONV_W = 3

ALPHA = (2.0 * DEPTH) ** 0.25
LN_EPS = 1e-5
RMS_EPS = 1e-6
NEG_INF = -1e30

LANES = 128
SUBLANES = 8
BF16_ROWS = 16
VMEM_LIMIT = 56 * 1024 * 1024

ROW_TILE = 512
ROW_CHUNK = 128
FFN_TF = 512
FFN_SAMPLE_TF = 256
FLASH_T = 512
FLASH_HEADS = 2
DECODE_PAGES = 32
DECODE_CHAINS = 2
SWA_SEQ_TILE = 8

F32 = jnp.float32
BF16 = jnp.bfloat16


def _dot(a, b):
    return jnp.dot(a, b, preferred_element_type=F32)


def _dot_nt(a, b):
    return lax.dot_general(a, b, (((1,), (1,)), ((), ())), preferred_element_type=F32)


def _layer_norm(y, g, b):
    mu = jnp.mean(y, axis=-1, keepdims=True)
    d = y - mu
    var = jnp.mean(d * d, axis=-1, keepdims=True)
    return d * lax.rsqrt(var + LN_EPS) * g + b


def _rms_norm(a, g):
    return a * lax.rsqrt(jnp.mean(a * a, axis=-1, keepdims=True) + RMS_EPS) * g


def _params(*sem):
    return pltpu.CompilerParams(dimension_semantics=sem, vmem_limit_bytes=VMEM_LIMIT)


def _row_spec(tm, cols):
    return pl.BlockSpec((tm, cols), lambda i: (i, 0))


def _const_spec(shape):
    nd = len(shape)
    return pl.BlockSpec(shape, lambda i: (0,) * nd)


def _ffn_body(x_ref, halo_ref, wg_ref, wv_ref, wo_ref, cw_ref, cb_ref, lng_ref, lnb_ref,
              *rest, tm, halo, shift, halo_is_gate, tail_rows, tiles_per_seq, own_f32,
              cast_next):
    i = pl.program_id(0)
    j = pl.program_id(1)
    xoff = 0 if halo_is_gate else halo
    rest = list(rest)
    next_in = [rest.pop(0) for _ in range(3)] if cast_next else []
    out_ref, tail_ref = rest.pop(0), rest.pop(0)
    if own_f32:
        wg_out, wv_out, wo_out = rest.pop(0), rest.pop(0), rest.pop(0)
        wg_out[...] = wg_ref[...].astype(BF16)
        wv_out[...] = wv_ref[...].astype(BF16)
        wo_out[...] = wo_ref[...].astype(BF16)
        wg_ref, wv_ref, wo_ref = wg_out, wv_out, wo_out
    next_out = [rest.pop(0) for _ in next_in]
    xb_scr, g_scr, acc_scr = rest

    @pl.when(j == 0)
    def _():
        if not halo_is_gate:
            keep = (i % tiles_per_seq != 0).astype(F32)
            xb_scr[0:halo, :] = (halo_ref[...] * keep).astype(BF16)
        xb_scr[xoff:xoff + tm, :] = x_ref[...].astype(BF16)
        acc_scr[...] = jnp.zeros_like(acc_scr)

    if halo_is_gate:
        g_scr[0:halo, :] = halo_ref[...]
        g_scr[halo:halo + tm, :] = _dot(xb_scr[...], wg_ref[...])
    else:
        g_scr[...] = _dot(xb_scr[...], wg_ref[...])
    val = _dot(xb_scr[xoff:xoff + tm, :], wv_ref[...])
    conv = (cb_ref[...]
            + g_scr[halo - 2 * shift:halo - 2 * shift + tm, :] * cw_ref[0:1, :]
            + g_scr[halo - shift:halo - shift + tm, :] * cw_ref[1:2, :]
            + g_scr[halo:halo + tm, :] * cw_ref[2:3, :])
    h = 0.5 * conv * (1.0 + lax.erf(conv * math.sqrt(0.5))) * val
    for src, dst in zip(next_in, next_out):
        dst[...] = src[...].astype(BF16)
    acc_scr[...] += _dot(h.astype(BF16), wo_ref[...])
    tail_ref[...] = g_scr[halo + tm - tail_rows:halo + tm, :]

    @pl.when(j == pl.num_programs(1) - 1)
    def _():
        y = ALPHA * x_ref[...] + acc_scr[...]
        out_ref[...] = _layer_norm(y, lng_ref[...], lnb_ref[...])


def _prev_rows_spec(tm, rows, cols, grid_rank):
    per_tile = tm // rows
    if grid_rank == 1:
        return pl.BlockSpec((rows, cols), lambda i: (jnp.maximum(i * per_tile - 1, 0), 0))
    return pl.BlockSpec((rows, cols), lambda i, j: (jnp.maximum(i * per_tile - 1, 0), 0))


def _ffn_call(x, gate_prev, weights, conv, ln, layer, *, tm, tf, shift, tiles_per_seq,
              own_f32=False, cast_next=None):
    cw, cb = conv
    lng, lnb = ln
    n = x.shape[0]
    nt = n // tm
    nf = D_FF // tf
    halo_is_gate = gate_prev is not None
    bf16_shapes = [jax.ShapeDtypeStruct((D_MODEL, D_FF), BF16),
                   jax.ShapeDtypeStruct((D_MODEL, D_FF), BF16),
                   jax.ShapeDtypeStruct((D_FF, D_MODEL), BF16)]
    tile_specs = [pl.BlockSpec((D_MODEL, tf), lambda i, j: (0, j)),
                  pl.BlockSpec((D_MODEL, tf), lambda i, j: (0, j)),
                  pl.BlockSpec((tf, D_MODEL), lambda i, j: (j, 0))]
    if halo_is_gate:
        halo_rows = tail_rows = gate_prev.shape[0]
        halo, halo_spec = gate_prev, pl.BlockSpec((halo_rows, tf), lambda i, j: (0, j))
        x_rows = tm
        tail_shape = (tail_rows, D_FF)
        tail_spec = pl.BlockSpec((tail_rows, tf), lambda i, j: (0, j))
    else:
        halo_rows, tail_rows = BF16_ROWS, SUBLANES
        halo, halo_spec = x, _prev_rows_spec(tm, halo_rows, D_MODEL, 2)
        x_rows = tm + halo_rows
        tail_shape = (nt, tail_rows, D_FF)
        tail_spec = pl.BlockSpec((None, tail_rows, tf), lambda i, j: (i, 0, j))
    extra_in, extra_in_specs, extra_specs, extra_shapes = [], [], [], []
    if own_f32:
        w_in, w_out = weights
        w_args = (w_in, w_in, w_out)
        w_specs = [pl.BlockSpec((None, D_MODEL, tf), lambda i, j: (layer, 0, j)),
                   pl.BlockSpec((None, D_MODEL, tf), lambda i, j: (layer, 0, j + nf)),
                   pl.BlockSpec((None, tf, D_MODEL), lambda i, j: (layer, j, 0))]
        extra_specs += tile_specs
        extra_shapes += bf16_shapes
    else:
        w_args, w_specs = weights, tile_specs
    if cast_next is not None:
        w_in, w_out = cast_next
        dr, fr = D_MODEL // nt, tf // nt
        assert dr * nt == D_MODEL and fr * nt == tf and fr % BF16_ROWS == 0
        nxt = layer + 1
        extra_in += [w_in, w_in, w_out]
        extra_in_specs += [pl.BlockSpec((None, dr, tf), lambda i, j: (nxt, i, j)),
                           pl.BlockSpec((None, dr, tf), lambda i, j: (nxt, i, j + nf)),
                           pl.BlockSpec((None, fr, D_MODEL), lambda i, j: (nxt, j * nt + i, 0))]
        extra_specs += [pl.BlockSpec((dr, tf), lambda i, j: (i, j)),
                        pl.BlockSpec((dr, tf), lambda i, j: (i, j)),
                        pl.BlockSpec((fr, D_MODEL), lambda i, j: (j * nt + i, 0))]
        extra_shapes += bf16_shapes
    body = functools.partial(_ffn_body, tm=tm, halo=halo_rows, shift=shift,
                             halo_is_gate=halo_is_gate, tail_rows=tail_rows,
                             tiles_per_seq=tiles_per_seq, own_f32=own_f32,
                             cast_next=cast_next is not None)
    ln_row = 2 * layer + 1
    outs = pl.pallas_call(
        body,
        grid=(nt, nf),
        in_specs=[
            pl.BlockSpec((tm, D_MODEL), lambda i, j: (i, 0)),
            halo_spec,
            *w_specs,
            pl.BlockSpec((None, CONV_W, tf), lambda i, j: (layer, 0, j)),
            pl.BlockSpec((None, 1, tf), lambda i, j: (layer, 0, j)),
            pl.BlockSpec((None, 1, D_MODEL), lambda i, j: (ln_row, 0, 0)),
            pl.BlockSpec((None, 1, D_MODEL), lambda i, j: (ln_row, 0, 0)),
            *extra_in_specs,
        ],
        out_specs=[pl.BlockSpec((tm, D_MODEL), lambda i, j: (i, 0)), tail_spec, *extra_specs],
        out_shape=[jax.ShapeDtypeStruct((n, D_MODEL), F32),
                   jax.ShapeDtypeStruct(tail_shape, F32), *extra_shapes],
        scratch_shapes=[pltpu.VMEM((x_rows, D_MODEL), BF16),
                        pltpu.VMEM((tm + halo_rows, tf), F32),
                        pltpu.VMEM((tm, D_MODEL), F32)],
        compiler_params=_params("parallel", "arbitrary"),
        name="conv_ffn_ln",
    )(x, halo, *w_args, cw, cb, lng, lnb, *extra_in)
    return outs[0], outs[1], tuple(outs[2:])


def _row_chunks(ref):
    rows = ref.shape[0]
    step = min(rows, ROW_CHUNK)
    return [slice(r, r + step) for r in range(0, rows, step)]


def _proj_ln_body(a_ref, x_ref, w_ref, bias_ref, lng_ref, lnb_ref, out_ref):
    for rows in _row_chunks(a_ref):
        h = _dot(a_ref[rows, :], w_ref[...]) + bias_ref[...]
        out_ref[rows, :] = _layer_norm(ALPHA * x_ref[rows, :] + h, lng_ref[...], lnb_ref[...])


def _proj_ln_call(a, x, w, bias, lng, lnb, tm):
    n, k = a.shape
    return pl.pallas_call(
        _proj_ln_body,
        grid=(n // tm,),
        in_specs=[_row_spec(tm, k), _row_spec(tm, D_MODEL), _const_spec(w.shape),
                  _const_spec((1, D_MODEL)), _const_spec((1, D_MODEL)), _const_spec((1, D_MODEL))],
        out_specs=_row_spec(tm, D_MODEL),
        out_shape=jax.ShapeDtypeStruct((n, D_MODEL), F32),
        compiler_params=_params("parallel"),
        name="proj_res_ln",
    )(a, x, w, bias, lng, lnb)


POOL_HALO = 4 * SUBLANES


def _pool_prompt_body(x_ref, halo_ref, w_ref, scale_ref, lng_ref, lnb_ref, out_ref, ext_scr,
                      tmp_scr, *, tm, tiles_per_seq):
    i = pl.program_id(0)
    hb = POOL_HALO
    end = hb + tm
    ext_scr[0:hb, :] = halo_ref[...] * (i % tiles_per_seq != 0).astype(F32)
    ext_scr[hb:end, :] = x_ref[...]
    pos = (i % tiles_per_seq) * tm + lax.broadcasted_iota(jnp.int32, (tm, 1), 0)
    for g, wnd in enumerate(POOL_WINDOWS):
        c0 = g * POOL_GC
        stages = wnd.bit_length() - 1
        src, cols = ext_scr, slice(c0, c0 + POOL_GC)
        for k in range(1, stages + 1):
            w = 2 ** (k - 1)
            start = hb if k == stages else SUBLANES * k
            win = src[start:end, cols] + src[start - w:end - w, cols]
            if k < stages:
                dst = tmp_scr.at[k % 2]
                dst[start:end, :] = win
                src, cols = dst, slice(0, POOL_GC)
        xg = x_ref[:, c0:c0 + POOL_GC]
        inv_cnt = 1.0 / jnp.minimum(pos + 1, wnd).astype(F32)
        pooled = win * inv_cnt - xg
        y = _dot(pooled.astype(BF16), w_ref[g]) * scale_ref[:, c0:c0 + POOL_GC]
        ext_scr[hb:end, c0:c0 + POOL_GC] = ALPHA * xg + y
    out_ref[...] = _layer_norm(ext_scr[hb:end, :], lng_ref[...], lnb_ref[...])


def _pool_prompt_call(x, w, scale, lng, lnb, tm, tiles_per_seq):
    n = x.shape[0]
    hb = POOL_HALO
    body = functools.partial(_pool_prompt_body, tm=tm, tiles_per_seq=tiles_per_seq)
    return pl.pallas_call(
        body,
        grid=(n // tm,),
        in_specs=[_row_spec(tm, D_MODEL),
                  _prev_rows_spec(tm, hb, D_MODEL, 1),
                  _const_spec(w.shape), _const_spec((1, D_MODEL)),
                  _const_spec((1, D_MODEL)), _const_spec((1, D_MODEL))],
        out_specs=_row_spec(tm, D_MODEL),
        out_shape=jax.ShapeDtypeStruct((n, D_MODEL), F32),
        scratch_shapes=[pltpu.VMEM((tm + hb, D_MODEL), F32),
                        pltpu.VMEM((2, tm + hb, POOL_GC), F32)],
        compiler_params=_params("parallel"),
        name="pool_mix_ln_prompt",
    )(x, x, w, scale, lng, lnb)


def _pool_sample_body(prev_ref, x_ref, w_ref, scale_ref, lng_ref, lnb_ref, out_ref, state_ref,
                      y_scr, *, t_new, bt):
    def row(r, cols):
        return prev_ref[r, :, cols] if r < POOL_BUF else x_ref[r - POOL_BUF, :, cols]

    for r in range(POOL_BUF):
        state_ref[r] = row(r + t_new, slice(None))

    for g, wnd in enumerate(POOL_WINDOWS):
        cols = slice(g * POOL_GC, (g + 1) * POOL_GC)
        pooled = []
        for t in range(t_new):
            win = x_ref[t, :, cols]
            for k in range(1, wnd):
                win = win + row(POOL_BUF + t - k, cols)
            pooled.append(win / float(wnd) - x_ref[t, :, cols])
        pooled = jnp.concatenate(pooled, axis=0)
        y = _dot(pooled.astype(BF16), w_ref[g]) * scale_ref[:, cols]
        for t in range(t_new):
            y_scr[t, :, cols] = ALPHA * x_ref[t, :, cols] + y[t * bt:(t + 1) * bt]
    for t in range(t_new):
        out_ref[t] = _layer_norm(y_scr[t], lng_ref[...], lnb_ref[...])


def _pool_sample_call(prev_all, layer, x, w, scale, lng, lnb, bt):
    t_new, nb, _ = x.shape
    body = functools.partial(_pool_sample_body, t_new=t_new, bt=bt)
    return pl.pallas_call(
        body,
        grid=(nb // bt,),
        in_specs=[pl.BlockSpec((None, POOL_BUF, bt, D_MODEL), lambda i: (layer, 0, i, 0)),
                  pl.BlockSpec((t_new, bt, D_MODEL), lambda i: (0, i, 0)),
                  _const_spec(w.shape), _const_spec((1, D_MODEL)),
                  _const_spec((1, D_MODEL)), _const_spec((1, D_MODEL))],
        out_specs=[pl.BlockSpec((t_new, bt, D_MODEL), lambda i: (0, i, 0)),
                   pl.BlockSpec((POOL_BUF, bt, D_MODEL), lambda i: (0, i, 0))],
        out_shape=[jax.ShapeDtypeStruct((t_new, nb, D_MODEL), F32),
                   jax.ShapeDtypeStruct((POOL_BUF, nb, D_MODEL), F32)],
        scratch_shapes=[pltpu.VMEM((t_new, bt, D_MODEL), F32)],
        compiler_params=_params("parallel"),
        name="pool_mix_ln_sample",
    )(prev_all, x, w, scale, lng, lnb)


def _rope_pair(pair, cs):
    t = pair * cs
    return t + pltpu.roll(t, MLA_ROPE, 1)


def _mla_a_body(x_ref, cs_ref, w_ref, gq_ref, gkv_ref, cq_ref, ckv_ref, kpe_ref, kpad_ref):
    for rows in _row_chunks(x_ref):
        a = _dot(x_ref[rows, :].astype(BF16), w_ref[...])
        cq_ref[rows, :] = _rms_norm(a[:, :MLA_Q_RANK], gq_ref[...]).astype(BF16)
        ckv_ref[rows, :] = _rms_norm(a[:, MLA_Q_RANK:MLA_Q_RANK + MLA_KV_RANK], gkv_ref[...])
        r = _rope_pair(a[:, MLA_Q_RANK + MLA_KV_RANK:], cs_ref[rows, :])
        kpe_ref[rows, :] = r[:, :MLA_ROPE]
        lane = lax.broadcasted_iota(jnp.int32, r.shape, 1)
        kpad_ref[rows, :] = jnp.where(lane < MLA_ROPE, r, 0.0).astype(BF16)


def _mla_a_call(x, cs, w, gq, gkv, tm, tiles_per_seq):
    n = x.shape[0]
    return pl.pallas_call(
        _mla_a_body,
        grid=(n // tm,),
        in_specs=[_row_spec(tm, D_MODEL),
                  pl.BlockSpec((tm, LANES), lambda i: (i % tiles_per_seq, 0)),
                  _const_spec(w.shape), _const_spec((1, MLA_Q_RANK)),
                  _const_spec((1, MLA_KV_RANK))],
        out_specs=[_row_spec(tm, MLA_Q_RANK), _row_spec(tm, MLA_KV_RANK),
                   _row_spec(tm, MLA_ROPE), _row_spec(tm, LANES)],
        out_shape=[jax.ShapeDtypeStruct((n, MLA_Q_RANK), BF16),
                   jax.ShapeDtypeStruct((n, MLA_KV_RANK), F32),
                   jax.ShapeDtypeStruct((n, MLA_ROPE), F32),
                   jax.ShapeDtypeStruct((n, LANES), BF16)],
        compiler_params=_params("parallel"),
        name="mla_down_proj",
    )(x, cs, w, gq, gkv)


MLA_QW = MLA_NOPE + 2 * MLA_ROPE


def _mla_q_body(cq_ref, cs_ref, w_ref, q_ref):
    cs = cs_ref[...]
    for h in range(MLA_HEADS):
        c0 = h * MLA_QW
        qh = _dot(cq_ref[...], w_ref[:, c0:c0 + MLA_QW]) * MLA_EXP2_SCALE
        q_ref[:, c0:c0 + MLA_NOPE] = qh[:, :MLA_NOPE].astype(BF16)
        q_ref[:, c0 + MLA_NOPE:c0 + MLA_QW] = _rope_pair(qh[:, MLA_NOPE:], cs).astype(BF16)


def _mla_q_call(cq, cs, w, tm, tiles_per_seq):
    n = cq.shape[0]
    return pl.pallas_call(
        _mla_q_body,
        grid=(n // tm,),
        in_specs=[_row_spec(tm, MLA_Q_RANK),
                  pl.BlockSpec((tm, LANES), lambda i: (i % tiles_per_seq, 0)),
                  _const_spec(w.shape)],
        out_specs=_row_spec(tm, MLA_HEADS * MLA_QW),
        out_shape=jax.ShapeDtypeStruct((n, MLA_HEADS * MLA_QW), BF16),
        compiler_params=_params("parallel"),
        name="mla_q_proj",
    )(cq, cs, w)


def _mla_kv_up_body(ckv_ref, kpad_ref, wuk_ref, wuv_t_ref, k_ref, vt_ref):
    ckv = ckv_ref[...].astype(BF16)
    vt_ref[...] = _dot_nt(wuv_t_ref[...], ckv).astype(BF16)
    k_nope = _dot(ckv, wuk_ref[...]).astype(BF16)
    for h in range(MLA_HEADS):
        c0 = h * MLA_QW
        k_ref[:, c0:c0 + MLA_NOPE] = k_nope[:, h * MLA_NOPE:(h + 1) * MLA_NOPE]
        k_ref[:, c0 + MLA_NOPE:c0 + MLA_QW] = kpad_ref[...]


def _mla_kv_up_call(ckv, kpad, wuk, wuv_t, tm):
    n = ckv.shape[0]
    return pl.pallas_call(
        _mla_kv_up_body,
        grid=(n // tm,),
        in_specs=[_row_spec(tm, MLA_KV_RANK), _row_spec(tm, LANES),
                  _const_spec(wuk.shape), _const_spec(wuv_t.shape)],
        out_specs=[_row_spec(tm, MLA_HEADS * MLA_QW),
                   pl.BlockSpec((MLA_HEADS * MLA_V, tm), lambda i: (0, i))],
        out_shape=[jax.ShapeDtypeStruct((n, MLA_HEADS * MLA_QW), BF16),
                   jax.ShapeDtypeStruct((MLA_HEADS * MLA_V, n), BF16)],
        compiler_params=_params("parallel"),
        name="mla_kv_up_proj",
    )(ckv, kpad, wuk, wuv_t)


def _softmax_init(m_scr, l_scr, acc_scr):
    m_scr[...] = jnp.full(m_scr.shape, NEG_INF, F32)
    l_scr[...] = jnp.zeros(l_scr.shape, F32)
    acc_scr[...] = jnp.zeros(acc_scr.shape, F32)


LOG2E = math.log2(math.e)
MLA_EXP2_SCALE = MLA_SCALE * LOG2E


def _flash_update(s, v, m_ref, l_ref, acc_ref):
    m_prev = m_ref[...]
    m_new = jnp.maximum(m_prev, jnp.max(s, axis=-1, keepdims=True))
    corr = jnp.exp2(m_prev - m_new)
    p = jnp.exp2(s - m_new)
    l_ref[...] = l_ref[...] * corr + jnp.sum(p, axis=-1, keepdims=True)
    acc_ref[...] = acc_ref[...] * corr + _dot(p.astype(BF16), v)
    m_ref[...] = m_new


def _flash_update_t(s_t, v_t, m_ref, l_ref, acc_ref):
    m_prev = m_ref[...]
    m_new = jnp.maximum(m_prev, jnp.max(s_t, axis=0, keepdims=True))
    corr = jnp.exp2(m_prev - m_new)
    p = jnp.exp2(s_t - m_new)
    l_ref[...] = l_ref[...] * corr + jnp.sum(p, axis=0, keepdims=True)
    acc_ref[...] = acc_ref[...] * corr + _dot(v_t, p.astype(BF16))
    m_ref[...] = m_new


def _mla_flash_body(q_ref, k_ref, vt_ref, o_ref, s_scr, m_scr, l_scr, acc_scr, *, t, heads):
    qi = pl.program_id(2)

    def scores(hh, ki):
        k = k_ref[pl.ds(pl.multiple_of(ki * t, t), t), hh * MLA_QW:(hh + 1) * MLA_QW]
        return _dot_nt(k, q_ref[:, hh * MLA_QW:(hh + 1) * MLA_QW])

    def values(hh, ki):
        return vt_ref[hh * MLA_V:(hh + 1) * MLA_V, pl.ds(pl.multiple_of(ki * t, t), t)]

    def advance(ki, slot):
        for hh in range(heads):
            s_scr[hh, 1 - slot] = scores(hh, ki + 1)
        for hh in range(heads):
            _flash_update_t(s_scr[hh, slot], values(hh, ki), m_scr.at[hh], l_scr.at[hh],
                            acc_scr.at[hh])

    def finish(slot):
        key = lax.broadcasted_iota(jnp.int32, (t, t), 0)
        query = lax.broadcasted_iota(jnp.int32, (t, t), 1)
        for hh in range(heads):
            s = jnp.where(key <= query, s_scr[hh, slot], NEG_INF)
            _flash_update_t(s, values(hh, qi), m_scr.at[hh], l_scr.at[hh], acc_scr.at[hh])
            o_t = acc_scr[hh] / l_scr[hh]
            o_ref[:, hh * MLA_V:(hh + 1) * MLA_V] = o_t.T.astype(BF16)

    for hh in range(heads):
        _softmax_init(m_scr.at[hh], l_scr.at[hh], acc_scr.at[hh])
        s_scr[hh, 0] = scores(hh, 0)

    def pair_step(kk, carry):
        advance(2 * kk, 0)
        advance(2 * kk + 1, 1)
        return carry

    lax.fori_loop(0, qi // 2, pair_step, 0)

    @pl.when(qi % 2 == 0)
    def _():
        finish(0)

    @pl.when(qi % 2 == 1)
    def _():
        advance(qi - 1, 0)
        finish(1)


def _mla_flash_call(q, k, v_t, batch, seq):
    t, heads = FLASH_T, FLASH_HEADS
    nq = seq // t
    body = functools.partial(_mla_flash_body, t=t, heads=heads)
    return pl.pallas_call(
        body,
        grid=(batch, MLA_HEADS // heads, nq),
        in_specs=[pl.BlockSpec((t, heads * MLA_QW), lambda b, h, i: (b * nq + i, h)),
                  pl.BlockSpec((seq, heads * MLA_QW), lambda b, h, i: (b, h)),
                  pl.BlockSpec((heads * MLA_V, seq), lambda b, h, i: (h, b))],
        out_specs=pl.BlockSpec((t, heads * MLA_V), lambda b, h, i: (b * nq + i, h)),
        out_shape=jax.ShapeDtypeStruct((batch * seq, MLA_HEADS * MLA_V), BF16),
        scratch_shapes=[pltpu.VMEM((heads, 2, t, t), F32),
                        pltpu.VMEM((heads, 1, t), F32), pltpu.VMEM((heads, 1, t), F32),
                        pltpu.VMEM((heads, MLA_V, t), F32)],
        compiler_params=_params("parallel", "parallel", "arbitrary"),
        name="mla_prompt_attention",
    )(q, k, v_t)


MLA_LATQ = MLA_KV_RANK + 2 * MLA_ROPE


def _mla_absorb_q_body(q_ref, wuk_ref, out_ref):
    q = q_ref[...]
    out_ref[:, :MLA_KV_RANK] = _dot_nt(q[:, :MLA_NOPE], wuk_ref[...]).astype(BF16)
    out_ref[:, MLA_KV_RANK:] = q[:, MLA_NOPE:]


def _mla_absorb_q_call(q, wuk):
    n = q.shape[0]
    return pl.pallas_call(
        _mla_absorb_q_body,
        grid=(MLA_HEADS,),
        in_specs=[pl.BlockSpec((n, MLA_QW), lambda h: (0, h)),
                  pl.BlockSpec((MLA_KV_RANK, MLA_NOPE), lambda h: (0, h))],
        out_specs=pl.BlockSpec((n, MLA_LATQ), lambda h: (0, h)),
        out_shape=jax.ShapeDtypeStruct((n, MLA_HEADS * MLA_LATQ), BF16),
        compiler_params=_params("parallel"),
        name="mla_absorb_q",
    )(q, wuk)


def _mla_decode_body(pt_ref, q_ref, nck_ref, nkp_ref, ck_hbm, kp_hbm, o_ref, ck_buf, kp_buf, sem,
                     ck_scr, kp_scr, m_scr, l_scr, acc_scr, *, pages, chains, t_new, layer):
    b = pl.program_id(0)
    step = pl.program_id(1)
    n_steps = pl.num_programs(1)
    idx = b * n_steps + step
    slot = idx % 2
    per_chain = pages // chains
    span = per_chain * PAGE_SIZE

    def page_copies(page_of, sl):
        copies = []
        for g in range(pages):
            page = page_of(g)
            copies.append(pltpu.make_async_copy(ck_hbm.at[layer, page], ck_buf.at[sl, g],
                                                sem.at[0, sl]))
            copies.append(pltpu.make_async_copy(kp_hbm.at[layer, page], kp_buf.at[sl, g],
                                                sem.at[1, sl]))
        return copies

    @pl.when(idx == 0)
    def _():
        for cp in page_copies(lambda g: pt_ref[0, g], 0):
            cp.start()

    @pl.when(idx + 1 < pl.num_programs(0) * n_steps)
    def _():
        wrap = step == n_steps - 1
        nb = jnp.where(wrap, b + 1, b)
        ns = jnp.where(wrap, 0, step + 1)
        for cp in page_copies(lambda g: pt_ref[nb, ns * pages + g], 1 - slot):
            cp.start()

    for cp in page_copies(lambda g: 0, slot):
        cp.wait()

    @pl.when(step == 0)
    def _():
        _softmax_init(m_scr, l_scr, acc_scr)

    q_lat = q_ref[0, :, :MLA_KV_RANK]
    q_pe = q_ref[0, :, MLA_KV_RANK:MLA_KV_RANK + MLA_ROPE]
    scores = []
    for c in range(chains):
        for g in range(c * per_chain, (c + 1) * per_chain):
            ck_scr[g * PAGE_SIZE:(g + 1) * PAGE_SIZE, :] = ck_buf[slot, g].astype(BF16)
            kp_scr[:, g * PAGE_SIZE:(g + 1) * PAGE_SIZE] = kp_buf[slot, g].astype(BF16)
        scores.append(_dot_nt(q_lat, ck_scr[c * span:(c + 1) * span, :])
                      + _dot(q_pe, kp_scr[:, c * span:(c + 1) * span]))
    for c in range(chains):
        _flash_update(scores[c], ck_scr[c * span:(c + 1) * span, :], m_scr.at[c], l_scr.at[c],
                      acc_scr.at[c])

    @pl.when(step == pl.num_programs(1) - 1)
    def _():
        for c in range(1, chains):
            m = jnp.maximum(m_scr[0], m_scr[c])
            w0 = jnp.exp2(m_scr[0] - m)
            wc = jnp.exp2(m_scr[c] - m)
            l_scr[0] = l_scr[0] * w0 + l_scr[c] * wc
            acc_scr[0] = acc_scr[0] * w0 + acc_scr[c] * wc
            m_scr[0] = m
        pad = PAGE_SIZE - nck_ref.shape[1]
        nck = jnp.concatenate([nck_ref[0], jnp.zeros((pad, MLA_KV_RANK), F32)],
                              axis=0).astype(BF16)
        nkp = jnp.concatenate([nkp_ref[0], jnp.zeros((pad, MLA_ROPE), F32)],
                              axis=0).astype(BF16)
        s2 = _dot_nt(q_lat, nck) + _dot_nt(q_pe, nkp)
        tq = lax.broadcasted_iota(jnp.int32, s2.shape, 0) // MLA_HEADS
        tk = lax.broadcasted_iota(jnp.int32, s2.shape, 1)
        s2 = jnp.where((tk <= tq) & (tk < t_new), s2, NEG_INF)
        _flash_update(s2, nck, m_scr.at[0], l_scr.at[0], acc_scr.at[0])
        o_ref[0] = (acc_scr[0] / l_scr[0]).astype(BF16)


def _mla_decode_call(page_table, q, new_ckv, new_kpe, cache_ckv, cache_kpe_t, layer, t_new):
    nb, rows, _ = q.shape
    n_pages = page_table.shape[1]
    pages, chains = DECODE_PAGES, DECODE_CHAINS
    pad_rows = new_ckv.shape[1]

    body = functools.partial(_mla_decode_body, pages=pages, chains=chains, t_new=t_new,
                             layer=layer)
    grid_spec = pltpu.PrefetchScalarGridSpec(
        num_scalar_prefetch=1,
        grid=(nb, n_pages // pages),
        in_specs=[pl.BlockSpec((1, rows, MLA_LATQ), lambda b, s, pt: (b, 0, 0)),
                  pl.BlockSpec((1, pad_rows, MLA_KV_RANK), lambda b, s, pt: (b, 0, 0)),
                  pl.BlockSpec((1, pad_rows, MLA_ROPE), lambda b, s, pt: (b, 0, 0)),
                  pl.BlockSpec(memory_space=pl.ANY),
                  pl.BlockSpec(memory_space=pl.ANY)],
        out_specs=pl.BlockSpec((1, rows, MLA_KV_RANK), lambda b, s, pt: (b, 0, 0)),
        scratch_shapes=[pltpu.VMEM((2, pages, PAGE_SIZE, MLA_KV_RANK), F32),
                        pltpu.VMEM((2, pages, MLA_ROPE, PAGE_SIZE), F32),
                        pltpu.SemaphoreType.DMA((2, 2)),
                        pltpu.VMEM((pages * PAGE_SIZE, MLA_KV_RANK), BF16),
                        pltpu.VMEM((MLA_ROPE, pages * PAGE_SIZE), BF16),
                        pltpu.VMEM((chains, rows, 1), F32), pltpu.VMEM((chains, rows, 1), F32),
                        pltpu.VMEM((chains, rows, MLA_KV_RANK), F32)],
    )
    return pl.pallas_call(
        body,
        grid_spec=grid_spec,
        out_shape=jax.ShapeDtypeStruct((nb, rows, MLA_KV_RANK), BF16),
        compiler_params=_params("arbitrary", "arbitrary"),
        name="mla_paged_decode",
    )(page_table, q, new_ckv, new_kpe, cache_ckv, cache_kpe_t)


def _mla_absorb_o_body(o_ref, wuv_ref, out_ref):
    out_ref[...] = _dot(o_ref[0], wuv_ref[...]).astype(BF16)


def _mla_absorb_o_call(o_lat, wuv):
    _, n, _ = o_lat.shape
    return pl.pallas_call(
        _mla_absorb_o_body,
        grid=(MLA_HEADS,),
        in_specs=[pl.BlockSpec((1, n, MLA_KV_RANK), lambda h: (h, 0, 0)),
                  pl.BlockSpec((MLA_KV_RANK, MLA_V), lambda h: (0, h))],
        out_specs=pl.BlockSpec((n, MLA_V), lambda h: (0, h)),
        out_shape=jax.ShapeDtypeStruct((n, MLA_HEADS * MLA_V), BF16),
        compiler_params=_params("parallel"),
        name="mla_absorb_o",
    )(o_lat, wuv)


SWA_NQ = SWA_HEADS * SWA_HEAD_DIM
SWA_NKV = SWA_KV_HEADS * SWA_HEAD_DIM
SWA_PAIRS = SWA_KV_HEADS // 2


def _swa_regroup(a, axis):
    shape = a.shape
    split = shape[:axis] + (SWA_PAIRS, 2, SWA_GROUP, SWA_HEAD_DIM) + shape[axis + 1:]
    return jnp.swapaxes(a.reshape(split), axis + 1, axis + 2).reshape(shape)


def _swa_slope(head):
    return 2.0 ** (-8.0 * (head + 1) / SWA_HEADS)


def _swa_qkv_body(x_ref, w_ref, b_ref, q_ref, k_ref, v_ref):
    for rows in _row_chunks(x_ref):
        qkv = _dot(x_ref[rows, :].astype(BF16), w_ref[...]) + b_ref[...]
        q_ref[rows, :] = qkv[:, :SWA_NQ].astype(BF16)
        k_ref[rows, :] = qkv[:, SWA_NQ:SWA_NQ + SWA_NKV]
        v_ref[rows, :] = qkv[:, SWA_NQ + SWA_NKV:]


def _swa_qkv_call(x, w, b, tm):
    n = x.shape[0]
    return pl.pallas_call(
        _swa_qkv_body,
        grid=(n // tm,),
        in_specs=[_row_spec(tm, D_MODEL), _const_spec(w.shape), _const_spec(b.shape)],
        out_specs=[_row_spec(tm, SWA_NQ), _row_spec(tm, SWA_NKV), _row_spec(tm, SWA_NKV)],
        out_shape=[jax.ShapeDtypeStruct((n, SWA_NQ), BF16),
                   jax.ShapeDtypeStruct((n, SWA_NKV), F32),
                   jax.ShapeDtypeStruct((n, SWA_NKV), F32)],
        compiler_params=_params("parallel"),
        name="swa_qkv_proj",
    )(x, w, b)


def _swa_heads(q_blk, k_pair, v_pair, sink_ref, p, g, lower, dist, valid):
    lane_q = lax.broadcasted_iota(jnp.int32, q_blk.shape, 1)
    outs = []
    for half in range(2):
        head = (2 * p + half) * SWA_GROUP + g
        in_half = (lane_q >= SWA_HEAD_DIM) if half else (lane_q < SWA_HEAD_DIM)
        qm = jnp.where(in_half, q_blk, jnp.zeros_like(q_blk))
        raw = _dot_nt(qm, k_pair)
        s = jnp.where(lower, raw[:, WINDOW:], raw[:, :WINDOW])
        s = s * (SWA_SCALE * LOG2E) - (_swa_slope(head) * LOG2E) * dist
        s = jnp.where(valid, s, NEG_INF)
        sink = sink_ref[head] * LOG2E
        m = jnp.maximum(jnp.max(s, axis=-1, keepdims=True), sink)
        e = jnp.exp2(s - m)
        denom = jnp.sum(e, axis=-1, keepdims=True) + jnp.exp2(sink - m)
        zero = jnp.zeros_like(e)
        pe = jnp.concatenate([jnp.where(lower, zero, e), jnp.where(lower, e, zero)], axis=1)
        outs.append(_dot(pe.astype(BF16), v_pair) * (1.0 / denom))
    lane_o = lax.broadcasted_iota(jnp.int32, outs[0].shape, 1)
    return jnp.where(lane_o < SWA_HEAD_DIM, outs[0], outs[1])


def _swa_prompt_body(sink_ref, q_ref, kc_ref, kp_ref, vc_ref, vp_ref, o_ref):
    i = pl.program_id(1)
    k = jnp.concatenate([kp_ref[...].astype(BF16), kc_ref[...].astype(BF16)], axis=0)
    v = jnp.concatenate([vp_ref[...].astype(BF16), vc_ref[...].astype(BF16)], axis=0)
    row = lax.broadcasted_iota(jnp.int32, (WINDOW, WINDOW), 0)
    col = lax.broadcasted_iota(jnp.int32, (WINDOW, WINDOW), 1)
    lower = col <= row
    distf = jnp.where(lower, row - col, row - col + WINDOW).astype(F32)
    valid = lower | (i > 0)
    for p in range(SWA_PAIRS):
        kpair = k[:, p * LANES:(p + 1) * LANES]
        vpair = v[:, p * LANES:(p + 1) * LANES]
        for g in range(SWA_GROUP):
            c0 = (p * SWA_GROUP + g) * LANES
            o = _swa_heads(q_ref[:, c0:c0 + LANES], kpair, vpair, sink_ref, p, g, lower, distf,
                           valid)
            o_ref[:, c0:c0 + LANES] = o.astype(BF16)


def _swa_prompt_call(sinks, q, k, v, batch, seq):
    nb = seq // WINDOW
    cur = lambda b, i: (b * nb + i, 0)
    prev = lambda b, i: (b * nb + jnp.maximum(i - 1, 0), 0)
    return pl.pallas_call(
        _swa_prompt_body,
        grid=(batch, nb),
        in_specs=[pl.BlockSpec(memory_space=pltpu.SMEM),
                  pl.BlockSpec((WINDOW, SWA_NQ), cur),
                  pl.BlockSpec((WINDOW, SWA_NKV), cur), pl.BlockSpec((WINDOW, SWA_NKV), prev),
                  pl.BlockSpec((WINDOW, SWA_NKV), cur), pl.BlockSpec((WINDOW, SWA_NKV), prev)],
        out_specs=pl.BlockSpec((WINDOW, SWA_NQ), cur),
        out_shape=jax.ShapeDtypeStruct((batch * seq, SWA_NQ), BF16),
        compiler_params=_params("parallel", "arbitrary"),
        name="swa_prompt_attention",
    )(sinks, q, k, k, v, v)


def _swa_sample_body(sink_ref, q_ref, kc_ref, vc_ref, kn_ref, vn_ref, o_ref, ko_ref, vo_ref,
                     *, bt, t_pad, t_new):
    n_keys = 2 * WINDOW
    fill = jnp.zeros((WINDOW - t_pad, LANES), F32)
    rows = 2 * SWA_GROUP * t_pad
    ridx = lax.broadcasted_iota(jnp.int32, (rows, 1), 0) // t_pad
    tq = lax.broadcasted_iota(jnp.int32, (rows, n_keys), 0) % t_pad
    col = lax.broadcasted_iota(jnp.int32, (rows, n_keys), 1)
    dist = tq + WINDOW - col
    valid = (dist >= 0) & (dist < WINDOW)
    distf = dist.astype(F32)
    lane = lax.broadcasted_iota(jnp.int32, (t_pad, LANES), 1)
    bias, sink_cols = [], []
    for p in range(SWA_PAIRS):
        slope = jnp.zeros((rows, 1), F32)
        sink = jnp.zeros((rows, 1), F32)
        for half in range(2):
            for g in range(SWA_GROUP):
                head = (2 * p + half) * SWA_GROUP + g
                here = ridx == half * SWA_GROUP + g
                slope = jnp.where(here, _swa_slope(head), slope)
                sink = jnp.where(here, sink_ref[head], sink)
        bias.append(slope * distf)
        sink_cols.append(sink)

    def one_seq(bb, carry):
        scores = []
        for p in range(SWA_PAIRS):
            pair = slice(p * LANES, (p + 1) * LANES)
            k_old = kc_ref[bb, pair, :].astype(BF16)
            k_new = jnp.concatenate([kn_ref[bb, :, pair], fill], axis=0).astype(BF16)
            blocks = [q_ref[bb, :, (p * SWA_GROUP + g) * LANES:(p * SWA_GROUP + g + 1) * LANES]
                      .astype(F32) for g in range(SWA_GROUP)]
            stack = ([jnp.where(lane < SWA_HEAD_DIM, blk, 0.0) for blk in blocks]
                     + [jnp.where(lane >= SWA_HEAD_DIM, blk, 0.0) for blk in blocks])
            qs = jnp.concatenate(stack, axis=0).astype(BF16)
            scores.append(jnp.concatenate([_dot(qs, k_old), _dot_nt(qs, k_new)], axis=1))
        probs = []
        for p in range(SWA_PAIRS):
            s = scores[p] * SWA_SCALE - bias[p]
            s = jnp.where(valid, s, NEG_INF)
            m = jnp.maximum(jnp.max(s, axis=-1, keepdims=True), sink_cols[p])
            e = jnp.exp(s - m)
            denom = jnp.sum(e, axis=-1, keepdims=True) + jnp.exp(sink_cols[p] - m)
            probs.append((e / denom).astype(BF16))
        for p in range(SWA_PAIRS):
            pair = slice(p * LANES, (p + 1) * LANES)
            v_old = vc_ref[bb, pair, :].astype(BF16)
            v_new = jnp.concatenate([vn_ref[bb, :, pair], fill], axis=0).astype(BF16)
            pr = probs[p]
            o = _dot_nt(pr[:, :WINDOW], v_old) + _dot(pr[:, WINDOW:], v_new)
            for g in range(SWA_GROUP):
                lo = o[g * t_pad:(g + 1) * t_pad]
                hi = o[(SWA_GROUP + g) * t_pad:(SWA_GROUP + g + 1) * t_pad]
                c0 = (p * SWA_GROUP + g) * LANES
                o_ref[bb, :, c0:c0 + LANES] = jnp.where(lane < SWA_HEAD_DIM, lo, hi).astype(BF16)
        tcol = lax.broadcasted_iota(jnp.int32, (LANES, WINDOW), 1)
        for old_ref, new_ref, out_ref in ((kc_ref, kn_ref, ko_ref), (vc_ref, vn_ref, vo_ref)):
            for p in range(SWA_PAIRS):
                pair = slice(p * LANES, (p + 1) * LANES)
                new_t = jnp.concatenate([new_ref[bb, :, pair], fill], axis=0).T
                merged = jnp.where(tcol < t_new, new_t, old_ref[bb, pair, :])
                out_ref[bb, pair, :] = pltpu.roll(merged, WINDOW - t_new, 1)
        return carry

    lax.fori_loop(0, bt, one_seq, 0)


def _swa_sample_call(sinks, q, kc, vc, kn, vn, t_new):
    nb, t_pad, _ = q.shape
    bt = SWA_SEQ_TILE
    body = functools.partial(_swa_sample_body, bt=bt, t_pad=t_pad, t_new=t_new)
    seq3 = lambda rows, cols: pl.BlockSpec((bt, rows, cols), lambda i: (i, 0, 0))
    return pl.pallas_call(
        body,
        grid=(nb // bt,),
        in_specs=[pl.BlockSpec(memory_space=pltpu.SMEM),
                  seq3(t_pad, SWA_NQ), seq3(SWA_NKV, WINDOW), seq3(SWA_NKV, WINDOW),
                  seq3(t_pad, SWA_NKV), seq3(t_pad, SWA_NKV)],
        out_specs=[seq3(t_pad, SWA_NQ), seq3(SWA_NKV, WINDOW), seq3(SWA_NKV, WINDOW)],
        out_shape=[jax.ShapeDtypeStruct((nb, t_pad, SWA_NQ), BF16),
                   jax.ShapeDtypeStruct((nb, SWA_NKV, WINDOW), F32),
                   jax.ShapeDtypeStruct((nb, SWA_NKV, WINDOW), F32)],
        compiler_params=_params("parallel"),
        name="swa_sample_attention",
    )(sinks, q, kc, vc, kn, vn)


def _rot_cols(w):
    half = w.shape[-1] // 2
    return jnp.concatenate([-w[..., half:], w[..., :half]], axis=-1)


def _rope_table(pos):
    half = MLA_ROPE // 2
    inv = ROPE_THETA ** (-jnp.arange(half, dtype=F32) / half)
    ang = pos.astype(F32)[:, None] * inv[None, :]
    cos, sin = jnp.cos(ang), jnp.sin(ang)
    return jnp.concatenate([cos, cos, sin, sin], axis=-1)


def _time_major(a, nb, t):
    return jnp.swapaxes(a, 0, 1).reshape((t * nb,) + a.shape[2:])


def _batch_major(a, nb, t):
    return jnp.swapaxes(a.reshape((t, nb) + a.shape[1:]), 0, 1)


def kernel(x_prompt, x_sample, state_pool, cache_mla_ckv, cache_mla_kpe, page_table, cache_swa_k, cache_swa_v, state_ffn_conv, ln_g, ln_b, pool_w, pool_scale, mla_w_a, mla_g_q, mla_g_kv, mla_w_uq, mla_w_uk, mla_w_uv, mla_w_o, swa_w_qkv, swa_b_qkv, swa_sinks, swa_w_o, swa_b_o, ffn_w_in, ffn_conv_w, ffn_conv_b, ffn_w_out):
    B, S, D = x_prompt.shape
    NB, T, _ = x_sample.shape
    NS = NB * T
    tm = ROW_TILE
    tps = S // tm
    assert S % tm == 0 and NS == tm and S % FLASH_T == 0 and T <= SUBLANES

    xp = x_prompt.reshape(B * S, D)
    xs = _time_major(x_sample, NB, T)
    cs_p = _rope_table(jnp.arange(S))
    cs_s = jnp.repeat(_rope_table(PAST_LEN + jnp.arange(T)), NB, axis=0)
    zero_bias = jnp.zeros((1, D), F32)
    ffn_conv = (ffn_conv_w, ffn_conv_b.reshape(DEPTH, 1, D_FF))
    ffn_f32 = (ffn_w_in, ffn_w_out)
    ln_rows = (ln_g.reshape(2 * DEPTH, 1, D), ln_b.reshape(2 * DEPTH, 1, D))

    pool_p, pool_s, ckv_p, ckv_s, kpe_p, kpe_s = [], [], [], [], [], []
    swk_p, swk_s, swv_p, swv_s, conv_p, conv_s = [], [], [], [], [], []

    for i in range(DEPTH):
        kind, j = i % N_MIXERS, i // N_MIXERS
        g1, b1 = ln_g[i, 0][None, :], ln_b[i, 0][None, :]
        if kind == 0:
            w = pool_w[j].astype(BF16)
            sc = pool_scale[j][None, :]
            pool_p.append(xp.reshape(B, S, D)[:, S - POOL_BUF:])
            prev_tm = jnp.swapaxes(state_pool, 1, 2)
            xs, state = _pool_sample_call(prev_tm, j, xs.reshape(T, NB, D), w, sc, g1, b1, 32)
            xs = xs.reshape(NS, D)
            pool_s.append(jnp.swapaxes(state, 0, 1))
            xp = _pool_prompt_call(xp, w, sc, g1, b1, tm, tps)
        elif kind == 1:
            w_a = mla_w_a[j]
            n_lat = MLA_Q_RANK + MLA_KV_RANK
            w_a_ext = jnp.concatenate([w_a, _rot_cols(w_a[:, n_lat:])], axis=1).astype(BF16)
            w_uq = mla_w_uq[j]
            w_uq_ext = jnp.concatenate([w_uq, _rot_cols(w_uq[..., MLA_NOPE:])], axis=-1)
            w_uq_ext = w_uq_ext.reshape(MLA_Q_RANK, MLA_HEADS * MLA_QW).astype(BF16)
            w_uk = mla_w_uk[j].reshape(MLA_KV_RANK, MLA_HEADS * MLA_NOPE).astype(BF16)
            w_uv = mla_w_uv[j].reshape(MLA_KV_RANK, MLA_HEADS * MLA_V).astype(BF16)
            w_o = mla_w_o[j].reshape(MLA_HEADS * MLA_V, D).astype(BF16)
            gq, gkv = mla_g_q[j][None, :], mla_g_kv[j][None, :]

            cq, ckv, kpe, kpad = _mla_a_call(xp, cs_p, w_a_ext, gq, gkv, tm, tps)
            q = _mla_q_call(cq, cs_p, w_uq_ext, tm, tps)
            k_full, v_full = _mla_kv_up_call(ckv, kpad, w_uk, w_uv.T, tm)
            o = _mla_flash_call(q, k_full, v_full, B, S)
            ckv_p.append(ckv.reshape(B, S, MLA_KV_RANK))
            kpe_p.append(kpe.reshape(B, S, MLA_ROPE))
            xp = _proj_ln_call(o, xp, w_o, zero_bias, g1, b1, tm)

            cq, ckv, kpe, _ = _mla_a_call(xs, cs_s, w_a_ext, gq, gkv, tm, 1)
            q = _mla_q_call(cq, cs_s, w_uq_ext, tm, 1)
            q_abs = _mla_absorb_q_call(q, w_uk)
            q_abs = _batch_major(q_abs.reshape(NS, MLA_HEADS, MLA_LATQ), NB, T)
            q_abs = q_abs.reshape(NB, T * MLA_HEADS, MLA_LATQ)
            ckv_bm = _batch_major(ckv, NB, T)
            kpe_bm = _batch_major(kpe, NB, T)
            pad = ((0, 0), (0, SUBLANES - T), (0, 0))
            o_lat = _mla_decode_call(page_table, q_abs, jnp.pad(ckv_bm, pad), jnp.pad(kpe_bm, pad),
                                     cache_mla_ckv, jnp.swapaxes(cache_mla_kpe, 2, 3), j, T)
            o_lat = o_lat.reshape(NB, T, MLA_HEADS, MLA_KV_RANK).transpose(2, 1, 0, 3)
            o = _mla_absorb_o_call(o_lat.reshape(MLA_HEADS, NS, MLA_KV_RANK), w_uv)
            ckv_s.append(ckv_bm)
            kpe_s.append(kpe_bm)
            xs = _proj_ln_call(o, xs, w_o, zero_bias, g1, b1, tm)
        else:
            w_qkv, b_qkv = swa_w_qkv[j], swa_b_qkv[j]
            w_qkv = jnp.concatenate([_swa_regroup(w_qkv[:, :SWA_NQ], 1), w_qkv[:, SWA_NQ:]],
                                    axis=1)
            b_qkv = jnp.concatenate([_swa_regroup(b_qkv[:SWA_NQ], 0), b_qkv[SWA_NQ:]])[None, :]
            w_qkv = w_qkv.astype(BF16)
            w_o = _swa_regroup(swa_w_o[j], 0).astype(BF16)
            b_o = swa_b_o[j][None, :]
            sinks = swa_sinks[j]

            q, k, v = _swa_qkv_call(xp, w_qkv, b_qkv, tm // 2)
            o = _swa_prompt_call(sinks, q, k, v, B, S)
            kv_shape = (B, WINDOW, SWA_KV_HEADS, SWA_HEAD_DIM)
            swk_p.append(k.reshape(B, S, SWA_NKV)[:, S - WINDOW:].reshape(kv_shape))
            swv_p.append(v.reshape(B, S, SWA_NKV)[:, S - WINDOW:].reshape(kv_shape))
            xp = _proj_ln_call(o, xp, w_o, b_o, g1, b1, tm)

            q, k, v = _swa_qkv_call(xs, w_qkv, b_qkv, tm // 2)
            pad = ((0, 0), (0, SUBLANES - T), (0, 0))
            q_bm = jnp.pad(_batch_major(q, NB, T), pad)
            k_bm, v_bm = _batch_major(k, NB, T), _batch_major(v, NB, T)
            kc = jnp.transpose(cache_swa_k[j], (0, 2, 3, 1)).reshape(NB, SWA_NKV, WINDOW)
            vc = jnp.transpose(cache_swa_v[j], (0, 2, 3, 1)).reshape(NB, SWA_NKV, WINDOW)
            o, k_upd, v_upd = _swa_sample_call(sinks, q_bm, kc, vc, jnp.pad(k_bm, pad),
                                               jnp.pad(v_bm, pad), T)
            o = _time_major(o[:, :T], NB, T)
            kv_shape = (NB, SWA_KV_HEADS, SWA_HEAD_DIM, WINDOW)
            swk_s.append(jnp.transpose(k_upd.reshape(kv_shape), (0, 3, 1, 2)))
            swv_s.append(jnp.transpose(v_upd.reshape(kv_shape), (0, 3, 1, 2)))
            xs = _proj_ln_call(o, xs, w_o, b_o, g1, b1, tm)

        prev = _time_major(state_ffn_conv[i], NB, CONV_W - 1)
        if i == 0:
            xs, tails, w_bf16 = _ffn_call(xs, prev, ffn_f32, ffn_conv, ln_rows, i, tm=tm,
                                          tf=FFN_SAMPLE_TF, shift=NB, tiles_per_seq=1,
                                          own_f32=True)
        else:
            xs, tails, _ = _ffn_call(xs, prev, w_bf16, ffn_conv, ln_rows, i, tm=tm, tf=FFN_TF,
                                     shift=NB, tiles_per_seq=1)
        conv_s.append(_batch_major(tails, NB, CONV_W - 1))
        xp, tails, w_next = _ffn_call(xp, None, w_bf16, ffn_conv, ln_rows, i, tm=tm, tf=FFN_TF,
                                      shift=1, tiles_per_seq=tps,
                                      cast_next=ffn_f32 if i + 1 < DEPTH else None)
        w_bf16 = w_next
        conv_p.append(tails.reshape(B, tps, SUBLANES, D_FF)[:, -1, SUBLANES - (CONV_W - 1):])

    y_p = xp.reshape(B, S, D)
    y_s = _batch_major(xs, NB, T)
    return (y_p, y_s, jnp.stack(pool_p), jnp.stack(pool_s), jnp.stack(ckv_p), jnp.stack(ckv_s),
            jnp.stack(kpe_p), jnp.stack(kpe_s), jnp.stack(swk_p), jnp.stack(swk_s),
            jnp.stack(swv_p), jnp.stack(swv_s), jnp.stack(conv_p), jnp.stack(conv_s))
```

```python
import functools
import math

import jax
import jax.numpy as jnp
import numpy as np
from jax import lax
from jax.experimental import pallas as pl
from jax.experimental.pallas import tpu as pltpu

D_MODEL = 2048
DEPTH = 4
N_MIXERS = 3
PAST_LEN = 8192
PAGE_SIZE = 128

POOL_WINDOWS = (2, 4, 8, 16)
POOL_GROUPS = len(POOL_WINDOWS)
POOL_GC = D_MODEL // POOL_GROUPS
POOL_BUF = max(POOL_WINDOWS) - 1

MLA_HEADS = 16
MLA_Q_RANK = 512
MLA_KV_RANK = 512
MLA_NOPE = 128
MLA_ROPE = 64
MLA_V = 128
MLA_SCALE = 1.0 / math.sqrt(MLA_NOPE + MLA_ROPE)
ROPE_THETA = 10000.0

SWA_HEADS = 32
SWA_KV_HEADS = 8
SWA_GROUP = SWA_HEADS // SWA_KV_HEADS
SWA_HEAD_DIM = 64
SWA_SCALE = 1.0 / math.sqrt(SWA_HEAD_DIM)
WINDOW = 128

D_FF = 5632
CONV_W = 3

ALPHA = (2.0 * DEPTH) ** 0.25
LN_EPS = 1e-5
RMS_EPS = 1e-6
NEG_INF = -1e30

LANES = 128
SUBLANES = 8
BF16_ROWS = 16
VMEM_LIMIT = 56 * 1024 * 1024

ROW_TILE = 512
ROW_CHUNK = 128
FFN_TF = 512
FFN_SAMPLE_TF = 256
FLASH_T = 512
FLASH_HEADS = 2
DECODE_PAGES = 64
DECODE_CHAINS = 2
SWA_SEQ_TILE = 8

F32 = jnp.float32
BF16 = jnp.bfloat16


def _dot(a, b):
    return jnp.dot(a, b, preferred_element_type=F32)


def _dot_nt(a, b):
    return lax.dot_general(a, b, (((1,), (1,)), ((), ())), preferred_element_type=F32)


def _layer_norm(y, g, b):
    mu = jnp.mean(y, axis=-1, keepdims=True)
    d = y - mu
    var = jnp.mean(d * d, axis=-1, keepdims=True)
    return d * lax.rsqrt(var + LN_EPS) * g + b


def _rms_norm(a, g):
    return a * lax.rsqrt(jnp.mean(a * a, axis=-1, keepdims=True) + RMS_EPS) * g


def _params(*sem):
    return pltpu.CompilerParams(dimension_semantics=sem, vmem_limit_bytes=VMEM_LIMIT)


def _row_spec(tm, cols):
    return pl.BlockSpec((tm, cols), lambda i: (i, 0))


def _const_spec(shape):
    nd = len(shape)
    return pl.BlockSpec(shape, lambda i: (0,) * nd)


def _ffn_body(x_ref, halo_ref, wg_ref, wv_ref, wo_ref, cw_ref, cb_ref, lng_ref, lnb_ref,
              *rest, tm, halo, shift, halo_is_gate, tail_rows, tiles_per_seq, own_f32,
              cast_next):
    i = pl.program_id(0)
    j = pl.program_id(1)
    xoff = 0 if halo_is_gate else halo
    rest = list(rest)
    next_in = [rest.pop(0) for _ in range(3)] if cast_next else []
    out_ref, tail_ref = rest.pop(0), rest.pop(0)
    if own_f32:
        wg_out, wv_out, wo_out = rest.pop(0), rest.pop(0), rest.pop(0)
        wg_out[...] = wg_ref[...].astype(BF16)
        wv_out[...] = wv_ref[...].astype(BF16)
        wo_out[...] = wo_ref[...].astype(BF16)
        wg_ref, wv_ref, wo_ref = wg_out, wv_out, wo_out
    next_out = [rest.pop(0) for _ in next_in]
    xb_scr, g_scr, acc_scr = rest

    @pl.when(j == 0)
    def _():
        if not halo_is_gate:
            keep = (i % tiles_per_seq != 0).astype(F32)
            xb_scr[0:halo, :] = (halo_ref[...] * keep).astype(BF16)
        xb_scr[xoff:xoff + tm, :] = x_ref[...].astype(BF16)
        acc_scr[...] = jnp.zeros_like(acc_scr)

    if halo_is_gate:
        g_scr[0:halo, :] = halo_ref[...]
        g_scr[halo:halo + tm, :] = _dot(xb_scr[...], wg_ref[...])
    else:
        g_scr[...] = _dot(xb_scr[...], wg_ref[...])
    val = _dot(xb_scr[xoff:xoff + tm, :], wv_ref[...])
    conv = (cb_ref[...]
            + g_scr[halo - 2 * shift:halo - 2 * shift + tm, :] * cw_ref[0:1, :]
            + g_scr[halo - shift:halo - shift + tm, :] * cw_ref[1:2, :]
            + g_scr[halo:halo + tm, :] * cw_ref[2:3, :])
    h = 0.5 * conv * (1.0 + lax.erf(conv * math.sqrt(0.5))) * val
    for src, dst in zip(next_in, next_out):
        dst[...] = src[...].astype(BF16)
    acc_scr[...] += _dot(h.astype(BF16), wo_ref[...])
    tail_ref[...] = g_scr[halo + tm - tail_rows:halo + tm, :]

    @pl.when(j == pl.num_programs(1) - 1)
    def _():
        y = ALPHA * x_ref[...] + acc_scr[...]
        out_ref[...] = _layer_norm(y, lng_ref[...], lnb_ref[...])


def _prev_rows_spec(tm, rows, cols, grid_rank):
    per_tile = tm // rows
    if grid_rank == 1:
        return pl.BlockSpec((rows, cols), lambda i: (jnp.maximum(i * per_tile - 1, 0), 0))
    return pl.BlockSpec((rows, cols), lambda i, j: (jnp.maximum(i * per_tile - 1, 0), 0))


def _ffn_call(x, gate_prev, weights, conv, ln, layer, *, tm, tf, shift, tiles_per_seq,
              own_f32=False, cast_next=None):
    cw, cb = conv
    lng, lnb = ln
    n = x.shape[0]
    nt = n // tm
    nf = D_FF // tf
    halo_is_gate = gate_prev is not None
    bf16_shapes = [jax.ShapeDtypeStruct((D_MODEL, D_FF), BF16),
                   jax.ShapeDtypeStruct((D_MODEL, D_FF), BF16),
                   jax.ShapeDtypeStruct((D_FF, D_MODEL), BF16)]
    tile_specs = [pl.BlockSpec((D_MODEL, tf), lambda i, j: (0, j)),
                  pl.BlockSpec((D_MODEL, tf), lambda i, j: (0, j)),
                  pl.BlockSpec((tf, D_MODEL), lambda i, j: (j, 0))]
    if halo_is_gate:
        halo_rows = tail_rows = gate_prev.shape[0]
        halo, halo_spec = gate_prev, pl.BlockSpec((halo_rows, tf), lambda i, j: (0, j))
        x_rows = tm
        tail_shape = (tail_rows, D_FF)
        tail_spec = pl.BlockSpec((tail_rows, tf), lambda i, j: (0, j))
    else:
        halo_rows, tail_rows = BF16_ROWS, SUBLANES
        halo, halo_spec = x, _prev_rows_spec(tm, halo_rows, D_MODEL, 2)
        x_rows = tm + halo_rows
        tail_shape = (nt, tail_rows, D_FF)
        tail_spec = pl.BlockSpec((None, tail_rows, tf), lambda i, j: (i, 0, j))
    extra_in, extra_in_specs, extra_specs, extra_shapes = [], [], [], []
    if own_f32:
        w_in, w_out = weights
        w_args = (w_in, w_in, w_out)
        w_specs = [pl.BlockSpec((None, D_MODEL, tf), lambda i, j: (layer, 0, j)),
                   pl.BlockSpec((None, D_MODEL, tf), lambda i, j: (layer, 0, j + nf)),
                   pl.BlockSpec((None, tf, D_MODEL), lambda i, j: (layer, j, 0))]
        extra_specs += tile_specs
        extra_shapes += bf16_shapes
    else:
        w_args, w_specs = weights, tile_specs
    if cast_next is not None:
        w_in, w_out = cast_next
        dr, fr = D_MODEL // nt, tf // nt
        assert dr * nt == D_MODEL and fr * nt == tf and fr % BF16_ROWS == 0
        nxt = layer + 1
        extra_in += [w_in, w_in, w_out]
        extra_in_specs += [pl.BlockSpec((None, dr, tf), lambda i, j: (nxt, i, j)),
                           pl.BlockSpec((None, dr, tf), lambda i, j: (nxt, i, j + nf)),
                           pl.BlockSpec((None, fr, D_MODEL), lambda i, j: (nxt, j * nt + i, 0))]
        extra_specs += [pl.BlockSpec((dr, tf), lambda i, j: (i, j)),
                        pl.BlockSpec((dr, tf), lambda i, j: (i, j)),
                        pl.BlockSpec((fr, D_MODEL), lambda i, j: (j * nt + i, 0))]
        extra_shapes += bf16_shapes
    body = functools.partial(_ffn_body, tm=tm, halo=halo_rows, shift=shift,
                             halo_is_gate=halo_is_gate, tail_rows=tail_rows,
                             tiles_per_seq=tiles_per_seq, own_f32=own_f32,
                             cast_next=cast_next is not None)
    ln_row = 2 * layer + 1
    outs = pl.pallas_call(
        body,
        grid=(nt, nf),
        in_specs=[
            pl.BlockSpec((tm, D_MODEL), lambda i, j: (i, 0)),
            halo_spec,
            *w_specs,
            pl.BlockSpec((None, CONV_W, tf), lambda i, j: (layer, 0, j)),
            pl.BlockSpec((None, 1, tf), lambda i, j: (layer, 0, j)),
            pl.BlockSpec((None, 1, D_MODEL), lambda i, j: (ln_row, 0, 0)),
            pl.BlockSpec((None, 1, D_MODEL), lambda i, j: (ln_row, 0, 0)),
            *extra_in_specs,
        ],
        out_specs=[pl.BlockSpec((tm, D_MODEL), lambda i, j: (i, 0)), tail_spec, *extra_specs],
        out_shape=[jax.ShapeDtypeStruct((n, D_MODEL), F32),
                   jax.ShapeDtypeStruct(tail_shape, F32), *extra_shapes],
        scratch_shapes=[pltpu.VMEM((x_rows, D_MODEL), BF16),
                        pltpu.VMEM((tm + halo_rows, tf), F32),
                        pltpu.VMEM((tm, D_MODEL), F32)],
        compiler_params=_params("parallel", "arbitrary"),
        name="conv_ffn_ln",
    )(x, halo, *w_args, cw, cb, lng, lnb, *extra_in)
    return outs[0], outs[1], tuple(outs[2:])


def _row_chunks(ref):
    rows = ref.shape[0]
    step = min(rows, ROW_CHUNK)
    return [slice(r, r + step) for r in range(0, rows, step)]


def _proj_ln_body(a_ref, x_ref, w_ref, bias_ref, lng_ref, lnb_ref, out_ref):
    for rows in _row_chunks(a_ref):
        h = _dot(a_ref[rows, :], w_ref[...]) + bias_ref[...]
        out_ref[rows, :] = _layer_norm(ALPHA * x_ref[rows, :] + h, lng_ref[...], lnb_ref[...])


def _proj_ln_call(a, x, w, bias, lng, lnb, tm):
    n, k = a.shape
    return pl.pallas_call(
        _proj_ln_body,
        grid=(n // tm,),
        in_specs=[_row_spec(tm, k), _row_spec(tm, D_MODEL), _const_spec(w.shape),
                  _const_spec((1, D_MODEL)), _const_spec((1, D_MODEL)), _const_spec((1, D_MODEL))],
        out_specs=_row_spec(tm, D_MODEL),
        out_shape=jax.ShapeDtypeStruct((n, D_MODEL), F32),
        compiler_params=_params("parallel"),
        name="proj_res_ln",
    )(a, x, w, bias, lng, lnb)


POOL_HALO = 4 * SUBLANES


def _pool_prompt_body(x_ref, halo_ref, w_ref, scale_ref, lng_ref, lnb_ref, out_ref, ext_scr,
                      tmp_scr, *, tm, tiles_per_seq):
    i = pl.program_id(0)
    hb = POOL_HALO
    end = hb + tm
    ext_scr[0:hb, :] = halo_ref[...] * (i % tiles_per_seq != 0).astype(F32)
    ext_scr[hb:end, :] = x_ref[...]
    pos = (i % tiles_per_seq) * tm + lax.broadcasted_iota(jnp.int32, (tm, 1), 0)
    for g, wnd in enumerate(POOL_WINDOWS):
        c0 = g * POOL_GC
        stages = wnd.bit_length() - 1
        src, cols = ext_scr, slice(c0, c0 + POOL_GC)
        for k in range(1, stages + 1):
            w = 2 ** (k - 1)
            start = hb if k == stages else SUBLANES * k
            win = src[start:end, cols] + src[start - w:end - w, cols]
            if k < stages:
                dst = tmp_scr.at[k % 2]
                dst[start:end, :] = win
                src, cols = dst, slice(0, POOL_GC)
        xg = x_ref[:, c0:c0 + POOL_GC]
        inv_cnt = 1.0 / jnp.minimum(pos + 1, wnd).astype(F32)
        pooled = win * inv_cnt - xg
        y = _dot(pooled.astype(BF16), w_ref[g]) * scale_ref[:, c0:c0 + POOL_GC]
        ext_scr[hb:end, c0:c0 + POOL_GC] = ALPHA * xg + y
    out_ref[...] = _layer_norm(ext_scr[hb:end, :], lng_ref[...], lnb_ref[...])


def _pool_prompt_call(x, w, scale, lng, lnb, tm, tiles_per_seq):
    n = x.shape[0]
    hb = POOL_HALO
    body = functools.partial(_pool_prompt_body, tm=tm, tiles_per_seq=tiles_per_seq)
    return pl.pallas_call(
        body,
        grid=(n // tm,),
        in_specs=[_row_spec(tm, D_MODEL),
                  _prev_rows_spec(tm, hb, D_MODEL, 1),
                  _const_spec(w.shape), _const_spec((1, D_MODEL)),
                  _const_spec((1, D_MODEL)), _const_spec((1, D_MODEL))],
        out_specs=_row_spec(tm, D_MODEL),
        out_shape=jax.ShapeDtypeStruct((n, D_MODEL), F32),
        scratch_shapes=[pltpu.VMEM((tm + hb, D_MODEL), F32),
                        pltpu.VMEM((2, tm + hb, POOL_GC), F32)],
        compiler_params=_params("parallel"),
        name="pool_mix_ln_prompt",
    )(x, x, w, scale, lng, lnb)


def _pool_sample_body(prev_ref, x_ref, w_ref, scale_ref, lng_ref, lnb_ref, out_ref, state_ref,
                      y_scr, *, t_new, bt):
    def row(r, cols):
        return prev_ref[r, :, cols] if r < POOL_BUF else x_ref[r - POOL_BUF, :, cols]

    for r in range(POOL_BUF):
        state_ref[r] = row(r + t_new, slice(None))

    for g, wnd in enumerate(POOL_WINDOWS):
        cols = slice(g * POOL_GC, (g + 1) * POOL_GC)
        pooled = []
        for t in range(t_new):
            win = x_ref[t, :, cols]
            for k in range(1, wnd):
                win = win + row(POOL_BUF + t - k, cols)
            pooled.append(win / float(wnd) - x_ref[t, :, cols])
        pooled = jnp.concatenate(pooled, axis=0)
        y = _dot(pooled.astype(BF16), w_ref[g]) * scale_ref[:, cols]
        for t in range(t_new):
            y_scr[t, :, cols] = ALPHA * x_ref[t, :, cols] + y[t * bt:(t + 1) * bt]
    for t in range(t_new):
        out_ref[t] = _layer_norm(y_scr[t], lng_ref[...], lnb_ref[...])


def _pool_sample_call(prev_all, layer, x, w, scale, lng, lnb, bt):
    t_new, nb, _ = x.shape
    body = functools.partial(_pool_sample_body, t_new=t_new, bt=bt)
    return pl.pallas_call(
        body,
        grid=(nb // bt,),
        in_specs=[pl.BlockSpec((None, POOL_BUF, bt, D_MODEL), lambda i: (layer, 0, i, 0)),
                  pl.BlockSpec((t_new, bt, D_MODEL), lambda i: (0, i, 0)),
                  _const_spec(w.shape), _const_spec((1, D_MODEL)),
                  _const_spec((1, D_MODEL)), _const_spec((1, D_MODEL))],
        out_specs=[pl.BlockSpec((t_new, bt, D_MODEL), lambda i: (0, i, 0)),
                   pl.BlockSpec((POOL_BUF, bt, D_MODEL), lambda i: (0, i, 0))],
        out_shape=[jax.ShapeDtypeStruct((t_new, nb, D_MODEL), F32),
                   jax.ShapeDtypeStruct((POOL_BUF, nb, D_MODEL), F32)],
        scratch_shapes=[pltpu.VMEM((t_new, bt, D_MODEL), F32)],
        compiler_params=_params("parallel"),
        name="pool_mix_ln_sample",
    )(prev_all, x, w, scale, lng, lnb)


def _rope_pair(pair, cs):
    t = pair * cs
    return t + pltpu.roll(t, MLA_ROPE, 1)


def _mla_a_body(x_ref, cs_ref, w_ref, gq_ref, gkv_ref, cq_ref, ckv_ref, kpe_ref, kpad_ref):
    for rows in _row_chunks(x_ref):
        a = _dot(x_ref[rows, :].astype(BF16), w_ref[...])
        cq_ref[rows, :] = _rms_norm(a[:, :MLA_Q_RANK], gq_ref[...]).astype(BF16)
        ckv_ref[rows, :] = _rms_norm(a[:, MLA_Q_RANK:MLA_Q_RANK + MLA_KV_RANK], gkv_ref[...])
        r = _rope_pair(a[:, MLA_Q_RANK + MLA_KV_RANK:], cs_ref[rows, :])
        kpe_ref[rows, :] = r[:, :MLA_ROPE]
        lane = lax.broadcasted_iota(jnp.int32, r.shape, 1)
        kpad_ref[rows, :] = jnp.where(lane < MLA_ROPE, r, 0.0).astype(BF16)


def _mla_a_call(x, cs, w, gq, gkv, tm, tiles_per_seq):
    n = x.shape[0]
    return pl.pallas_call(
        _mla_a_body,
        grid=(n // tm,),
        in_specs=[_row_spec(tm, D_MODEL),
                  pl.BlockSpec((tm, LANES), lambda i: (i % tiles_per_seq, 0)),
                  _const_spec(w.shape), _const_spec((1, MLA_Q_RANK)),
                  _const_spec((1, MLA_KV_RANK))],
        out_specs=[_row_spec(tm, MLA_Q_RANK), _row_spec(tm, MLA_KV_RANK),
                   _row_spec(tm, MLA_ROPE), _row_spec(tm, LANES)],
        out_shape=[jax.ShapeDtypeStruct((n, MLA_Q_RANK), BF16),
                   jax.ShapeDtypeStruct((n, MLA_KV_RANK), F32),
                   jax.ShapeDtypeStruct((n, MLA_ROPE), F32),
                   jax.ShapeDtypeStruct((n, LANES), BF16)],
        compiler_params=_params("parallel"),
        name="mla_down_proj",
    )(x, cs, w, gq, gkv)


MLA_QW = MLA_NOPE + 2 * MLA_ROPE


def _mla_q_body(cq_ref, cs_ref, w_ref, q_ref):
    cs = cs_ref[...]
    for h in range(MLA_HEADS):
        c0 = h * MLA_QW
        qh = _dot(cq_ref[...], w_ref[:, c0:c0 + MLA_QW]) * MLA_EXP2_SCALE
        q_ref[:, c0:c0 + MLA_NOPE] = qh[:, :MLA_NOPE].astype(BF16)
        q_ref[:, c0 + MLA_NOPE:c0 + MLA_QW] = _rope_pair(qh[:, MLA_NOPE:], cs).astype(BF16)


def _mla_q_call(cq, cs, w, tm, tiles_per_seq):
    n = cq.shape[0]
    return pl.pallas_call(
        _mla_q_body,
        grid=(n // tm,),
        in_specs=[_row_spec(tm, MLA_Q_RANK),
                  pl.BlockSpec((tm, LANES), lambda i: (i % tiles_per_seq, 0)),
                  _const_spec(w.shape)],
        out_specs=_row_spec(tm, MLA_HEADS * MLA_QW),
        out_shape=jax.ShapeDtypeStruct((n, MLA_HEADS * MLA_QW), BF16),
        compiler_params=_params("parallel"),
        name="mla_q_proj",
    )(cq, cs, w)


def _mla_kv_up_body(ckv_ref, kpad_ref, wuk_ref, wuv_t_ref, k_ref, vt_ref):
    ckv = ckv_ref[...].astype(BF16)
    vt_ref[...] = _dot_nt(wuv_t_ref[...], ckv).astype(BF16)
    k_nope = _dot(ckv, wuk_ref[...]).astype(BF16)
    for h in range(MLA_HEADS):
        c0 = h * MLA_QW
        k_ref[:, c0:c0 + MLA_NOPE] = k_nope[:, h * MLA_NOPE:(h + 1) * MLA_NOPE]
        k_ref[:, c0 + MLA_NOPE:c0 + MLA_QW] = kpad_ref[...]


def _mla_kv_up_call(ckv, kpad, wuk, wuv_t, tm):
    n = ckv.shape[0]
    return pl.pallas_call(
        _mla_kv_up_body,
        grid=(n // tm,),
        in_specs=[_row_spec(tm, MLA_KV_RANK), _row_spec(tm, LANES),
                  _const_spec(wuk.shape), _const_spec(wuv_t.shape)],
        out_specs=[_row_spec(tm, MLA_HEADS * MLA_QW),
                   pl.BlockSpec((MLA_HEADS * MLA_V, tm), lambda i: (0, i))],
        out_shape=[jax.ShapeDtypeStruct((n, MLA_HEADS * MLA_QW), BF16),
                   jax.ShapeDtypeStruct((MLA_HEADS * MLA_V, n), BF16)],
        compiler_params=_params("parallel"),
        name="mla_kv_up_proj",
    )(ckv, kpad, wuk, wuv_t)


def _softmax_init(m_scr, l_scr, acc_scr):
    m_scr[...] = jnp.full(m_scr.shape, NEG_INF, F32)
    l_scr[...] = jnp.zeros(l_scr.shape, F32)
    acc_scr[...] = jnp.zeros(acc_scr.shape, F32)


LOG2E = math.log2(math.e)
MLA_EXP2_SCALE = MLA_SCALE * LOG2E


def _flash_update(s, v, m_ref, l_ref, acc_ref):
    m_prev = m_ref[...]
    m_new = jnp.maximum(m_prev, jnp.max(s, axis=-1, keepdims=True))
    corr = jnp.exp2(m_prev - m_new)
    p = jnp.exp2(s - m_new)
    l_ref[...] = l_ref[...] * corr + jnp.sum(p, axis=-1, keepdims=True)
    acc_ref[...] = acc_ref[...] * corr + _dot(p.astype(BF16), v)
    m_ref[...] = m_new


def _flash_update_t(s_t, v_t, m_ref, l_ref, acc_ref):
    m_prev = m_ref[...]
    m_new = jnp.maximum(m_prev, jnp.max(s_t, axis=0, keepdims=True))
    corr = jnp.exp2(m_prev - m_new)
    p = jnp.exp2(s_t - m_new)
    l_ref[...] = l_ref[...] * corr + jnp.sum(p, axis=0, keepdims=True)
    acc_ref[...] = acc_ref[...] * corr + _dot(v_t, p.astype(BF16))
    m_ref[...] = m_new


def _mla_flash_body(q_ref, k_ref, vt_ref, o_ref, s_scr, m_scr, l_scr, acc_scr, *, t, heads):
    qi = pl.program_id(2)

    def scores(hh, ki):
        k = k_ref[pl.ds(pl.multiple_of(ki * t, t), t), hh * MLA_QW:(hh + 1) * MLA_QW]
        return _dot_nt(k, q_ref[:, hh * MLA_QW:(hh + 1) * MLA_QW])

    def values(hh, ki):
        return vt_ref[hh * MLA_V:(hh + 1) * MLA_V, pl.ds(pl.multiple_of(ki * t, t), t)]

    def advance(ki, slot):
        for hh in range(heads):
            s_scr[hh, 1 - slot] = scores(hh, ki + 1)
        for hh in range(heads):
            _flash_update_t(s_scr[hh, slot], values(hh, ki), m_scr.at[hh], l_scr.at[hh],
                            acc_scr.at[hh])

    def finish(slot):
        key = lax.broadcasted_iota(jnp.int32, (t, t), 0)
        query = lax.broadcasted_iota(jnp.int32, (t, t), 1)
        for hh in range(heads):
            s = jnp.where(key <= query, s_scr[hh, slot], NEG_INF)
            _flash_update_t(s, values(hh, qi), m_scr.at[hh], l_scr.at[hh], acc_scr.at[hh])
            o_t = acc_scr[hh] / l_scr[hh]
            o_ref[:, hh * MLA_V:(hh + 1) * MLA_V] = o_t.T.astype(BF16)

    for hh in range(heads):
        _softmax_init(m_scr.at[hh], l_scr.at[hh], acc_scr.at[hh])
        s_scr[hh, 0] = scores(hh, 0)

    def pair_step(kk, carry):
        advance(2 * kk, 0)
        advance(2 * kk + 1, 1)
        return carry

    lax.fori_loop(0, qi // 2, pair_step, 0)

    @pl.when(qi % 2 == 0)
    def _():
        finish(0)

    @pl.when(qi % 2 == 1)
    def _():
        advance(qi - 1, 0)
        finish(1)


def _mla_flash_call(q, k, v_t, batch, seq):
    t, heads = FLASH_T, FLASH_HEADS
    nq = seq // t
    body = functools.partial(_mla_flash_body, t=t, heads=heads)
    return pl.pallas_call(
        body,
        grid=(batch, MLA_HEADS // heads, nq),
        in_specs=[pl.BlockSpec((t, heads * MLA_QW), lambda b, h, i: (b * nq + i, h)),
                  pl.BlockSpec((seq, heads * MLA_QW), lambda b, h, i: (b, h)),
                  pl.BlockSpec((heads * MLA_V, seq), lambda b, h, i: (h, b))],
        out_specs=pl.BlockSpec((t, heads * MLA_V), lambda b, h, i: (b * nq + i, h)),
        out_shape=jax.ShapeDtypeStruct((batch * seq, MLA_HEADS * MLA_V), BF16),
        scratch_shapes=[pltpu.VMEM((heads, 2, t, t), F32),
                        pltpu.VMEM((heads, 1, t), F32), pltpu.VMEM((heads, 1, t), F32),
                        pltpu.VMEM((heads, MLA_V, t), F32)],
        compiler_params=_params("parallel", "parallel", "arbitrary"),
        name="mla_prompt_attention",
    )(q, k, v_t)


MLA_LATQ = MLA_KV_RANK + 2 * MLA_ROPE


def _mla_absorb_q_body(q_ref, wuk_ref, out_ref):
    q = q_ref[...]
    out_ref[:, :MLA_KV_RANK] = _dot_nt(q[:, :MLA_NOPE], wuk_ref[...]).astype(BF16)
    out_ref[:, MLA_KV_RANK:] = q[:, MLA_NOPE:]


def _mla_absorb_q_call(q, wuk):
    n = q.shape[0]
    return pl.pallas_call(
        _mla_absorb_q_body,
        grid=(MLA_HEADS,),
        in_specs=[pl.BlockSpec((n, MLA_QW), lambda h: (0, h)),
                  pl.BlockSpec((MLA_KV_RANK, MLA_NOPE), lambda h: (0, h))],
        out_specs=pl.BlockSpec((n, MLA_LATQ), lambda h: (0, h)),
        out_shape=jax.ShapeDtypeStruct((n, MLA_HEADS * MLA_LATQ), BF16),
        compiler_params=_params("parallel"),
        name="mla_absorb_q",
    )(q, wuk)


def _mla_decode_body(pt_ref, q_ref, nck_ref, nkp_ref, ck_hbm, kp_hbm, o_ref, ck_buf, kp_buf, sem,
                     ck_scr, kp_scr, m_scr, l_scr, acc_scr, *, pages, chains, t_new, layer):
    b = pl.program_id(0)
    step = pl.program_id(1)
    n_steps = pl.num_programs(1)
    idx = b * n_steps + step
    slot = idx % 2
    per_chain = pages // chains
    span = per_chain * PAGE_SIZE

    def page_copies(page_of, sl):
        copies = []
        for g in range(pages):
            page = page_of(g)
            copies.append(pltpu.make_async_copy(ck_hbm.at[layer, page], ck_buf.at[sl, g],
                                                sem.at[0, sl]))
            copies.append(pltpu.make_async_copy(kp_hbm.at[layer, page], kp_buf.at[sl, g],
                                                sem.at[1, sl]))
        return copies

    @pl.when(idx == 0)
    def _():
        for cp in page_copies(lambda g: pt_ref[0, g], 0):
            cp.start()

    @pl.when(idx + 1 < pl.num_programs(0) * n_steps)
    def _():
        wrap = step == n_steps - 1
        nb = jnp.where(wrap, b + 1, b)
        ns = jnp.where(wrap, 0, step + 1)
        for cp in page_copies(lambda g: pt_ref[nb, ns * pages + g], 1 - slot):
            cp.start()

    for cp in page_copies(lambda g: 0, slot):
        cp.wait()

    @pl.when(step == 0)
    def _():
        _softmax_init(m_scr, l_scr, acc_scr)

    q_lat = q_ref[0, :, :MLA_KV_RANK]
    q_pe = q_ref[0, :, MLA_KV_RANK:MLA_KV_RANK + MLA_ROPE]
    scores = []
    for c in range(chains):
        for g in range(c * per_chain, (c + 1) * per_chain):
            ck_scr[g * PAGE_SIZE:(g + 1) * PAGE_SIZE, :] = ck_buf[slot, g].astype(BF16)
            kp_scr[:, g * PAGE_SIZE:(g + 1) * PAGE_SIZE] = kp_buf[slot, g].astype(BF16)
        scores.append(_dot_nt(q_lat, ck_scr[c * span:(c + 1) * span, :])
                      + _dot(q_pe, kp_scr[:, c * span:(c + 1) * span]))
    for c in range(chains):
        _flash_update(scores[c], ck_scr[c * span:(c + 1) * span, :], m_scr.at[c], l_scr.at[c],
                      acc_scr.at[c])

    @pl.when(step == pl.num_programs(1) - 1)
    def _():
        for c in range(1, chains):
            m = jnp.maximum(m_scr[0], m_scr[c])
            w0 = jnp.exp2(m_scr[0] - m)
            wc = jnp.exp2(m_scr[c] - m)
            l_scr[0] = l_scr[0] * w0 + l_scr[c] * wc
            acc_scr[0] = acc_scr[0] * w0 + acc_scr[c] * wc
            m_scr[0] = m
        pad = PAGE_SIZE - nck_ref.shape[1]
        nck = jnp.concatenate([nck_ref[0], jnp.zeros((pad, MLA_KV_RANK), F32)],
                              axis=0).astype(BF16)
        nkp = jnp.concatenate([nkp_ref[0], jnp.zeros((pad, MLA_ROPE), F32)],
                              axis=0).astype(BF16)
        s2 = _dot_nt(q_lat, nck) + _dot_nt(q_pe, nkp)
        tq = lax.broadcasted_iota(jnp.int32, s2.shape, 0) // MLA_HEADS
        tk = lax.broadcasted_iota(jnp.int32, s2.shape, 1)
        s2 = jnp.where((tk <= tq) & (tk < t_new), s2, NEG_INF)
        _flash_update(s2, nck, m_scr.at[0], l_scr.at[0], acc_scr.at[0])
        o_ref[0] = (acc_scr[0] / l_scr[0]).astype(BF16)


def _mla_decode_call(page_table, q, new_ckv, new_kpe, cache_ckv, cache_kpe_t, layer, t_new):
    nb, rows, _ = q.shape
    n_pages = page_table.shape[1]
    pages, chains = DECODE_PAGES, DECODE_CHAINS
    pad_rows = new_ckv.shape[1]

    body = functools.partial(_mla_decode_body, pages=pages, chains=chains, t_new=t_new,
                             layer=layer)
    grid_spec = pltpu.PrefetchScalarGridSpec(
        num_scalar_prefetch=1,
        grid=(nb, n_pages // pages),
        in_specs=[pl.BlockSpec((1, rows, MLA_LATQ), lambda b, s, pt: (b, 0, 0)),
                  pl.BlockSpec((1, pad_rows, MLA_KV_RANK), lambda b, s, pt: (b, 0, 0)),
                  pl.BlockSpec((1, pad_rows, MLA_ROPE), lambda b, s, pt: (b, 0, 0)),
                  pl.BlockSpec(memory_space=pl.ANY),
                  pl.BlockSpec(memory_space=pl.ANY)],
        out_specs=pl.BlockSpec((1, rows, MLA_KV_RANK), lambda b, s, pt: (b, 0, 0)),
        scratch_shapes=[pltpu.VMEM((2, pages, PAGE_SIZE, MLA_KV_RANK), F32),
                        pltpu.VMEM((2, pages, MLA_ROPE, PAGE_SIZE), F32),
                        pltpu.SemaphoreType.DMA((2, 2)),
                        pltpu.VMEM((pages * PAGE_SIZE, MLA_KV_RANK), BF16),
                        pltpu.VMEM((MLA_ROPE, pages * PAGE_SIZE), BF16),
                        pltpu.VMEM((chains, rows, 1), F32), pltpu.VMEM((chains, rows, 1), F32),
                        pltpu.VMEM((chains, rows, MLA_KV_RANK), F32)],
    )
    return pl.pallas_call(
        body,
        grid_spec=grid_spec,
        out_shape=jax.ShapeDtypeStruct((nb, rows, MLA_KV_RANK), BF16),
        compiler_params=_params("arbitrary", "arbitrary"),
        name="mla_paged_decode",
    )(page_table, q, new_ckv, new_kpe, cache_ckv, cache_kpe_t)


def _mla_absorb_o_body(o_ref, wuv_ref, out_ref):
    out_ref[...] = _dot(o_ref[0], wuv_ref[...]).astype(BF16)


def _mla_absorb_o_call(o_lat, wuv):
    _, n, _ = o_lat.shape
    return pl.pallas_call(
        _mla_absorb_o_body,
        grid=(MLA_HEADS,),
        in_specs=[pl.BlockSpec((1, n, MLA_KV_RANK), lambda h: (h, 0, 0)),
                  pl.BlockSpec((MLA_KV_RANK, MLA_V), lambda h: (0, h))],
        out_specs=pl.BlockSpec((n, MLA_V), lambda h: (0, h)),
        out_shape=jax.ShapeDtypeStruct((n, MLA_HEADS * MLA_V), BF16),
        compiler_params=_params("parallel"),
        name="mla_absorb_o",
    )(o_lat, wuv)


SWA_NQ = SWA_HEADS * SWA_HEAD_DIM
SWA_NKV = SWA_KV_HEADS * SWA_HEAD_DIM
SWA_PAIRS = SWA_KV_HEADS // 2


def _swa_regroup(a, axis):
    shape = a.shape
    split = shape[:axis] + (SWA_PAIRS, 2, SWA_GROUP, SWA_HEAD_DIM) + shape[axis + 1:]
    return jnp.swapaxes(a.reshape(split), axis + 1, axis + 2).reshape(shape)


def _swa_slope(head):
    return 2.0 ** (-8.0 * (head + 1) / SWA_HEADS)


def _swa_qkv_body(x_ref, w_ref, b_ref, q_ref, k_ref, v_ref):
    for rows in _row_chunks(x_ref):
        qkv = _dot(x_ref[rows, :].astype(BF16), w_ref[...]) + b_ref[...]
        q_ref[rows, :] = qkv[:, :SWA_NQ].astype(BF16)
        k_ref[rows, :] = qkv[:, SWA_NQ:SWA_NQ + SWA_NKV]
        v_ref[rows, :] = qkv[:, SWA_NQ + SWA_NKV:]


def _swa_qkv_call(x, w, b, tm):
    n = x.shape[0]
    return pl.pallas_call(
        _swa_qkv_body,
        grid=(n // tm,),
        in_specs=[_row_spec(tm, D_MODEL), _const_spec(w.shape), _const_spec(b.shape)],
        out_specs=[_row_spec(tm, SWA_NQ), _row_spec(tm, SWA_NKV), _row_spec(tm, SWA_NKV)],
        out_shape=[jax.ShapeDtypeStruct((n, SWA_NQ), BF16),
                   jax.ShapeDtypeStruct((n, SWA_NKV), F32),
                   jax.ShapeDtypeStruct((n, SWA_NKV), F32)],
        compiler_params=_params("parallel"),
        name="swa_qkv_proj",
    )(x, w, b)


def _swa_heads(q_blk, k_pair, v_pair, sink_ref, p, g, lower, dist, valid):
    lane_q = lax.broadcasted_iota(jnp.int32, q_blk.shape, 1)
    outs = []
    for half in range(2):
        head = (2 * p + half) * SWA_GROUP + g
        in_half = (lane_q >= SWA_HEAD_DIM) if half else (lane_q < SWA_HEAD_DIM)
        qm = jnp.where(in_half, q_blk, jnp.zeros_like(q_blk))
        raw = _dot_nt(qm, k_pair)
        s = jnp.where(lower, raw[:, WINDOW:], raw[:, :WINDOW])
        s = s * (SWA_SCALE * LOG2E) - (_swa_slope(head) * LOG2E) * dist
        s = jnp.where(valid, s, NEG_INF)
        sink = sink_ref[head] * LOG2E
        m = jnp.maximum(jnp.max(s, axis=-1, keepdims=True), sink)
        e = jnp.exp2(s - m)
        denom = jnp.sum(e, axis=-1, keepdims=True) + jnp.exp2(sink - m)
        zero = jnp.zeros_like(e)
        pe = jnp.concatenate([jnp.where(lower, zero, e), jnp.where(lower, e, zero)], axis=1)
        outs.append(_dot(pe.astype(BF16), v_pair) * (1.0 / denom))
    lane_o = lax.broadcasted_iota(jnp.int32, outs[0].shape, 1)
    return jnp.where(lane_o < SWA_HEAD_DIM, outs[0], outs[1])


def _swa_prompt_body(sink_ref, q_ref, kc_ref, kp_ref, vc_ref, vp_ref, o_ref):
    i = pl.program_id(1)
    k = jnp.concatenate([kp_ref[...].astype(BF16), kc_ref[...].astype(BF16)], axis=0)
    v = jnp.concatenate([vp_ref[...].astype(BF16), vc_ref[...].astype(BF16)], axis=0)
    row = lax.broadcasted_iota(jnp.int32, (WINDOW, WINDOW), 0)
    col = lax.broadcasted_iota(jnp.int32, (WINDOW, WINDOW), 1)
    lower = col <= row
    distf = jnp.where(lower, row - col, row - col + WINDOW).astype(F32)
    valid = lower | (i > 0)
    for p in range(SWA_PAIRS):
        kpair = k[:, p * LANES:(p + 1) * LANES]
        vpair = v[:, p * LANES:(p + 1) * LANES]
        for g in range(SWA_GROUP):
            c0 = (p * SWA_GROUP + g) * LANES
            o = _swa_heads(q_ref[:, c0:c0 + LANES], kpair, vpair, sink_ref, p, g, lower, distf,
                           valid)
            o_ref[:, c0:c0 + LANES] = o.astype(BF16)


def _swa_prompt_call(sinks, q, k, v, batch, seq):
    nb = seq // WINDOW
    cur = lambda b, i: (b * nb + i, 0)
    prev = lambda b, i: (b * nb + jnp.maximum(i - 1, 0), 0)
    return pl.pallas_call(
        _swa_prompt_body,
        grid=(batch, nb),
        in_specs=[pl.BlockSpec(memory_space=pltpu.SMEM),
                  pl.BlockSpec((WINDOW, SWA_NQ), cur),
                  pl.BlockSpec((WINDOW, SWA_NKV), cur), pl.BlockSpec((WINDOW, SWA_NKV), prev),
                  pl.BlockSpec((WINDOW, SWA_NKV), cur), pl.BlockSpec((WINDOW, SWA_NKV), prev)],
        out_specs=pl.BlockSpec((WINDOW, SWA_NQ), cur),
        out_shape=jax.ShapeDtypeStruct((batch * seq, SWA_NQ), BF16),
        compiler_params=_params("parallel", "arbitrary"),
        name="swa_prompt_attention",
    )(sinks, q, k, k, v, v)


def _swa_sample_body(sink_ref, q_ref, kc_ref, vc_ref, kn_ref, vn_ref, o_ref, ko_ref, vo_ref,
                     *, bt, t_pad, t_new):
    n_keys = 2 * WINDOW
    fill = jnp.zeros((WINDOW - t_pad, LANES), F32)
    rows = 2 * SWA_GROUP * t_pad
    ridx = lax.broadcasted_iota(jnp.int32, (rows, 1), 0) // t_pad
    tq = lax.broadcasted_iota(jnp.int32, (rows, n_keys), 0) % t_pad
    col = lax.broadcasted_iota(jnp.int32, (rows, n_keys), 1)
    dist = tq + WINDOW - col
    valid = (dist >= 0) & (dist < WINDOW)
    distf = dist.astype(F32)
    lane = lax.broadcasted_iota(jnp.int32, (t_pad, LANES), 1)
    bias, sink_cols = [], []
    for p in range(SWA_PAIRS):
        slope = jnp.zeros((rows, 1), F32)
        sink = jnp.zeros((rows, 1), F32)
        for half in range(2):
            for g in range(SWA_GROUP):
                head = (2 * p + half) * SWA_GROUP + g
                here = ridx == half * SWA_GROUP + g
                slope = jnp.where(here, _swa_slope(head), slope)
                sink = jnp.where(here, sink_ref[head], sink)
        bias.append(slope * distf)
        sink_cols.append(sink)

    def one_seq(bb, carry):
        scores = []
        for p in range(SWA_PAIRS):
            pair = slice(p * LANES, (p + 1) * LANES)
            k_old = kc_ref[bb, pair, :].astype(BF16)
            k_new = jnp.concatenate([kn_ref[bb, :, pair], fill], axis=0).astype(BF16)
            blocks = [q_ref[bb, :, (p * SWA_GROUP + g) * LANES:(p * SWA_GROUP + g + 1) * LANES]
                      .astype(F32) for g in range(SWA_GROUP)]
            stack = ([jnp.where(lane < SWA_HEAD_DIM, blk, 0.0) for blk in blocks]
                     + [jnp.where(lane >= SWA_HEAD_DIM, blk, 0.0) for blk in blocks])
            qs = jnp.concatenate(stack, axis=0).astype(BF16)
            scores.append(jnp.concatenate([_dot(qs, k_old), _dot_nt(qs, k_new)], axis=1))
        probs = []
        for p in range(SWA_PAIRS):
            s = scores[p] * SWA_SCALE - bias[p]
            s = jnp.where(valid, s, NEG_INF)
            m = jnp.maximum(jnp.max(s, axis=-1, keepdims=True), sink_cols[p])
            e = jnp.exp(s - m)
            denom = jnp.sum(e, axis=-1, keepdims=True) + jnp.exp(sink_cols[p] - m)
            probs.append((e / denom).astype(BF16))
        for p in range(SWA_PAIRS):
            pair = slice(p * LANES, (p + 1) * LANES)
            v_old = vc_ref[bb, pair, :].astype(BF16)
            v_new = jnp.concatenate([vn_ref[bb, :, pair], fill], axis=0).astype(BF16)
            pr = probs[p]
            o = _dot_nt(pr[:, :WINDOW], v_old) + _dot(pr[:, WINDOW:], v_new)
            for g in range(SWA_GROUP):
                lo = o[g * t_pad:(g + 1) * t_pad]
                hi = o[(SWA_GROUP + g) * t_pad:(SWA_GROUP + g + 1) * t_pad]
                c0 = (p * SWA_GROUP + g) * LANES
                o_ref[bb, :, c0:c0 + LANES] = jnp.where(lane < SWA_HEAD_DIM, lo, hi).astype(BF16)
        tcol = lax.broadcasted_iota(jnp.int32, (LANES, WINDOW), 1)
        for old_ref, new_ref, out_ref in ((kc_ref, kn_ref, ko_ref), (vc_ref, vn_ref, vo_ref)):
            for p in range(SWA_PAIRS):
                pair = slice(p * LANES, (p + 1) * LANES)
                new_t = jnp.concatenate([new_ref[bb, :, pair], fill], axis=0).T
                merged = jnp.where(tcol < t_new, new_t, old_ref[bb, pair, :])
                out_ref[bb, pair, :] = pltpu.roll(merged, WINDOW - t_new, 1)
        return carry

    lax.fori_loop(0, bt, one_seq, 0)


def _swa_sample_call(sinks, q, kc, vc, kn, vn, t_new):
    nb, t_pad, _ = q.shape
    bt = SWA_SEQ_TILE
    body = functools.partial(_swa_sample_body, bt=bt, t_pad=t_pad, t_new=t_new)
    seq3 = lambda rows, cols: pl.BlockSpec((bt, rows, cols), lambda i: (i, 0, 0))
    return pl.pallas_call(
        body,
        grid=(nb // bt,),
        in_specs=[pl.BlockSpec(memory_space=pltpu.SMEM),
                  seq3(t_pad, SWA_NQ), seq3(SWA_NKV, WINDOW), seq3(SWA_NKV, WINDOW),
                  seq3(t_pad, SWA_NKV), seq3(t_pad, SWA_NKV)],
        out_specs=[seq3(t_pad, SWA_NQ), seq3(SWA_NKV, WINDOW), seq3(SWA_NKV, WINDOW)],
        out_shape=[jax.ShapeDtypeStruct((nb, t_pad, SWA_NQ), BF16),
                   jax.ShapeDtypeStruct((nb, SWA_NKV, WINDOW), F32),
                   jax.ShapeDtypeStruct((nb, SWA_NKV, WINDOW), F32)],
        compiler_params=_params("parallel"),
        name="swa_sample_attention",
    )(sinks, q, kc, vc, kn, vn)


def _rot_cols(w):
    half = w.shape[-1] // 2
    return jnp.concatenate([-w[..., half:], w[..., :half]], axis=-1)


def _rope_table(pos):
    half = MLA_ROPE // 2
    inv = ROPE_THETA ** (-jnp.arange(half, dtype=F32) / half)
    ang = pos.astype(F32)[:, None] * inv[None, :]
    cos, sin = jnp.cos(ang), jnp.sin(ang)
    return jnp.concatenate([cos, cos, sin, sin], axis=-1)


def _time_major(a, nb, t):
    return jnp.swapaxes(a, 0, 1).reshape((t * nb,) + a.shape[2:])


def _batch_major(a, nb, t):
    return jnp.swapaxes(a.reshape((t, nb) + a.shape[1:]), 0, 1)


def kernel(x_prompt, x_sample, state_pool, cache_mla_ckv, cache_mla_kpe, page_table, cache_swa_k, cache_swa_v, state_ffn_conv, ln_g, ln_b, pool_w, pool_scale, mla_w_a, mla_g_q, mla_g_kv, mla_w_uq, mla_w_uk, mla_w_uv, mla_w_o, swa_w_qkv, swa_b_qkv, swa_sinks, swa_w_o, swa_b_o, ffn_w_in, ffn_conv_w, ffn_conv_b, ffn_w_out):
    B, S, D = x_prompt.shape
    NB, T, _ = x_sample.shape
    NS = NB * T
    tm = ROW_TILE
    tps = S // tm
    assert S % tm == 0 and NS == tm and S % FLASH_T == 0 and T <= SUBLANES

    xp = x_prompt.reshape(B * S, D)
    xs = _time_major(x_sample, NB, T)
    cs_p = _rope_table(jnp.arange(S))
    cs_s = jnp.repeat(_rope_table(PAST_LEN + jnp.arange(T)), NB, axis=0)
    zero_bias = jnp.zeros((1, D), F32)
    ffn_conv = (ffn_conv_w, ffn_conv_b.reshape(DEPTH, 1, D_FF))
    ffn_f32 = (ffn_w_in, ffn_w_out)
    ln_rows = (ln_g.reshape(2 * DEPTH, 1, D), ln_b.reshape(2 * DEPTH, 1, D))

    pool_p, pool_s, ckv_p, ckv_s, kpe_p, kpe_s = [], [], [], [], [], []
    swk_p, swk_s, swv_p, swv_s, conv_p, conv_s = [], [], [], [], [], []

    for i in range(DEPTH):
        kind, j = i % N_MIXERS, i // N_MIXERS
        g1, b1 = ln_g[i, 0][None, :], ln_b[i, 0][None, :]
        if kind == 0:
            w = pool_w[j].astype(BF16)
            sc = pool_scale[j][None, :]
            pool_p.append(xp.reshape(B, S, D)[:, S - POOL_BUF:])
            prev_tm = jnp.swapaxes(state_pool, 1, 2)
            xs, state = _pool_sample_call(prev_tm, j, xs.reshape(T, NB, D), w, sc, g1, b1, 32)
            xs = xs.reshape(NS, D)
            pool_s.append(jnp.swapaxes(state, 0, 1))
            xp = _pool_prompt_call(xp, w, sc, g1, b1, tm, tps)
        elif kind == 1:
            w_a = mla_w_a[j]
            n_lat = MLA_Q_RANK + MLA_KV_RANK
            w_a_ext = jnp.concatenate([w_a, _rot_cols(w_a[:, n_lat:])], axis=1).astype(BF16)
            w_uq = mla_w_uq[j]
            w_uq_ext = jnp.concatenate([w_uq, _rot_cols(w_uq[..., MLA_NOPE:])], axis=-1)
            w_uq_ext = w_uq_ext.reshape(MLA_Q_RANK, MLA_HEADS * MLA_QW).astype(BF16)
            w_uk = mla_w_uk[j].reshape(MLA_KV_RANK, MLA_HEADS * MLA_NOPE).astype(BF16)
            w_uv = mla_w_uv[j].reshape(MLA_KV_RANK, MLA_HEADS * MLA_V).astype(BF16)
            w_o = mla_w_o[j].reshape(MLA_HEADS * MLA_V, D).astype(BF16)
            gq, gkv = mla_g_q[j][None, :], mla_g_kv[j][None, :]

            cq, ckv, kpe, kpad = _mla_a_call(xp, cs_p, w_a_ext, gq, gkv, tm, tps)
            q = _mla_q_call(cq, cs_p, w_uq_ext, tm, tps)
            k_full, v_full = _mla_kv_up_call(ckv, kpad, w_uk, w_uv.T, tm)
            o = _mla_flash_call(q, k_full, v_full, B, S)
            ckv_p.append(ckv.reshape(B, S, MLA_KV_RANK))
            kpe_p.append(kpe.reshape(B, S, MLA_ROPE))
            xp = _proj_ln_call(o, xp, w_o, zero_bias, g1, b1, tm)

            cq, ckv, kpe, _ = _mla_a_call(xs, cs_s, w_a_ext, gq, gkv, tm, 1)
            q = _mla_q_call(cq, cs_s, w_uq_ext, tm, 1)
            q_abs = _mla_absorb_q_call(q, w_uk)
            q_abs = _batch_major(q_abs.reshape(NS, MLA_HEADS, MLA_LATQ), NB, T)
            q_abs = q_abs.reshape(NB, T * MLA_HEADS, MLA_LATQ)
            ckv_bm = _batch_major(ckv, NB, T)
            kpe_bm = _batch_major(kpe, NB, T)
            pad = ((0, 0), (0, SUBLANES - T), (0, 0))
            o_lat = _mla_decode_call(page_table, q_abs, jnp.pad(ckv_bm, pad), jnp.pad(kpe_bm, pad),
                                     cache_mla_ckv, jnp.swapaxes(cache_mla_kpe, 2, 3), j, T)
            o_lat = o_lat.reshape(NB, T, MLA_HEADS, MLA_KV_RANK).transpose(2, 1, 0, 3)
            o = _mla_absorb_o_call(o_lat.reshape(MLA_HEADS, NS, MLA_KV_RANK), w_uv)
            ckv_s.append(ckv_bm)
            kpe_s.append(kpe_bm)
            xs = _proj_ln_call(o, xs, w_o, zero_bias, g1, b1, tm)
        else:
            w_qkv, b_qkv = swa_w_qkv[j], swa_b_qkv[j]
            w_qkv = jnp.concatenate([_swa_regroup(w_qkv[:, :SWA_NQ], 1), w_qkv[:, SWA_NQ:]],
                                    axis=1)
            b_qkv = jnp.concatenate([_swa_regroup(b_qkv[:SWA_NQ], 0), b_qkv[SWA_NQ:]])[None, :]
            w_qkv = w_qkv.astype(BF16)
            w_o = _swa_regroup(swa_w_o[j], 0).astype(BF16)
            b_o = swa_b_o[j][None, :]
            sinks = swa_sinks[j]

            q, k, v = _swa_qkv_call(xp, w_qkv, b_qkv, tm // 2)
            o = _swa_prompt_call(sinks, q, k, v, B, S)
            kv_shape = (B, WINDOW, SWA_KV_HEADS, SWA_HEAD_DIM)
            swk_p.append(k.reshape(B, S, SWA_NKV)[:, S - WINDOW:].reshape(kv_shape))
            swv_p.append(v.reshape(B, S, SWA_NKV)[:, S - WINDOW:].reshape(kv_shape))
            xp = _proj_ln_call(o, xp, w_o, b_o, g1, b1, tm)

            q, k, v = _swa_qkv_call(xs, w_qkv, b_qkv, tm // 2)
            pad = ((0, 0), (0, SUBLANES - T), (0, 0))
            q_bm = jnp.pad(_batch_major(q, NB, T), pad)
            k_bm, v_bm = _batch_major(k, NB, T), _batch_major(v, NB, T)
            kc = jnp.transpose(cache_swa_k[j], (0, 2, 3, 1)).reshape(NB, SWA_NKV, WINDOW)
            vc = jnp.transpose(cache_swa_v[j], (0, 2, 3, 1)).reshape(NB, SWA_NKV, WINDOW)
            o, k_upd, v_upd = _swa_sample_call(sinks, q_bm, kc, vc, jnp.pad(k_bm, pad),
                                               jnp.pad(v_bm, pad), T)
            o = _time_major(o[:, :T], NB, T)
            kv_shape = (NB, SWA_KV_HEADS, SWA_HEAD_DIM, WINDOW)
            swk_s.append(jnp.transpose(k_upd.reshape(kv_shape), (0, 3, 1, 2)))
            swv_s.append(jnp.transpose(v_upd.reshape(kv_shape), (0, 3, 1, 2)))
            xs = _proj_ln_call(o, xs, w_o, b_o, g1, b1, tm)

        prev = _time_major(state_ffn_conv[i], NB, CONV_W - 1)
        if i == 0:
            xs, tails, w_bf16 = _ffn_call(xs, prev, ffn_f32, ffn_conv, ln_rows, i, tm=tm,
                                          tf=FFN_SAMPLE_TF, shift=NB, tiles_per_seq=1,
                                          own_f32=True)
        else:
            xs, tails, _ = _ffn_call(xs, prev, w_bf16, ffn_conv, ln_rows, i, tm=tm, tf=FFN_TF,
                                     shift=NB, tiles_per_seq=1)
        conv_s.append(_batch_major(tails, NB, CONV_W - 1))
        xp, tails, w_next = _ffn_call(xp, None, w_bf16, ffn_conv, ln_rows, i, tm=tm, tf=FFN_TF,
                                      shift=1, tiles_per_seq=tps,
                                      cast_next=ffn_f32 if i + 1 < DEPTH else None)
        w_bf16 = w_next
        conv_p.append(tails.reshape(B, tps, SUBLANES, D_FF)[:, -1, SUBLANES - (CONV_W - 1):])

    y_p = xp.reshape(B, S, D)
    y_s = _batch_major(xs, NB, T)
    return (y_p, y_s, jnp.stack(pool_p), jnp.stack(pool_s), jnp.stack(ckv_p), jnp.stack(ckv_s),
            jnp.stack(kpe_p), jnp.stack(kpe_s), jnp.stack(swk_p), jnp.stack(swk_s),
            jnp.stack(swv_p), jnp.stack(swv_s), jnp.stack(conv_p), jnp.stack(conv_s))
```

```python
import functools
import math

import jax
import jax.numpy as jnp
import numpy as np
from jax import lax
from jax.experimental import pallas as pl
from jax.experimental.pallas import tpu as pltpu

D_MODEL = 2048
DEPTH = 4
N_MIXERS = 3
PAST_LEN = 8192
PAGE_SIZE = 128

POOL_WINDOWS = (2, 4, 8, 16)
POOL_GROUPS = len(POOL_WINDOWS)
POOL_GC = D_MODEL // POOL_GROUPS
POOL_BUF = max(POOL_WINDOWS) - 1

MLA_HEADS = 16
MLA_Q_RANK = 512
MLA_KV_RANK = 512
MLA_NOPE = 128
MLA_ROPE = 64
MLA_V = 128
MLA_SCALE = 1.0 / math.sqrt(MLA_NOPE + MLA_ROPE)
ROPE_THETA = 10000.0

SWA_HEADS = 32
SWA_KV_HEADS = 8
SWA_GROUP = SWA_HEADS // SWA_KV_HEADS
SWA_HEAD_DIM = 64
SWA_SCALE = 1.0 / math.sqrt(SWA_HEAD_DIM)
WINDOW = 128

D_FF = 5632
CONV_W = 3

ALPHA = (2.0 * DEPTH) ** 0.25
LN_EPS = 1e-5
RMS_EPS = 1e-6
NEG_INF = -1e30

LANES = 128
SUBLANES = 8
BF16_ROWS = 16
VMEM_LIMIT = 56 * 1024 * 1024

ROW_TILE = 512
ROW_CHUNK = 128
FFN_TF = 512
FFN_SAMPLE_TF = 256
FLASH_T = 512
FLASH_HEADS = 2
DECODE_PAGES = 64
DECODE_CHAINS = 2
SWA_SEQ_TILE = 8

F32 = jnp.float32
BF16 = jnp.bfloat16


def _dot(a, b):
    return jnp.dot(a, b, preferred_element_type=F32)


def _dot_nt(a, b):
    return lax.dot_general(a, b, (((1,), (1,)), ((), ())), preferred_element_type=F32)


def _layer_norm(y, g, b):
    mu = jnp.mean(y, axis=-1, keepdims=True)
    d = y - mu
    var = jnp.mean(d * d, axis=-1, keepdims=True)
    return d * lax.rsqrt(var + LN_EPS) * g + b


def _rms_norm(a, g):
    return a * lax.rsqrt(jnp.mean(a * a, axis=-1, keepdims=True) + RMS_EPS) * g


def _params(*sem):
    return pltpu.CompilerParams(dimension_semantics=sem, vmem_limit_bytes=VMEM_LIMIT)


def _row_spec(tm, cols):
    return pl.BlockSpec((tm, cols), lambda i: (i, 0))


def _const_spec(shape):
    nd = len(shape)
    return pl.BlockSpec(shape, lambda i: (0,) * nd)


def _ffn_body(x_ref, halo_ref, wg_ref, wv_ref, wo_ref, cw_ref, cb_ref, lng_ref, lnb_ref,
              *rest, tm, halo, shift, halo_is_gate, tail_rows, tiles_per_seq, own_f32,
              cast_next):
    i = pl.program_id(0)
    j = pl.program_id(1)
    xoff = 0 if halo_is_gate else halo
    rest = list(rest)
    next_in = [rest.pop(0) for _ in range(3)] if cast_next else []
    out_ref, tail_ref = rest.pop(0), rest.pop(0)
    if own_f32:
        wg_out, wv_out, wo_out = rest.pop(0), rest.pop(0), rest.pop(0)
        wg_out[...] = wg_ref[...].astype(BF16)
        wv_out[...] = wv_ref[...].astype(BF16)
        wo_out[...] = wo_ref[...].astype(BF16)
        wg_ref, wv_ref, wo_ref = wg_out, wv_out, wo_out
    next_out = [rest.pop(0) for _ in next_in]
    xb_scr, g_scr, acc_scr = rest

    @pl.when(j == 0)
    def _():
        if not halo_is_gate:
            keep = (i % tiles_per_seq != 0).astype(F32)
            xb_scr[0:halo, :] = (halo_ref[...] * keep).astype(BF16)
        xb_scr[xoff:xoff + tm, :] = x_ref[...].astype(BF16)
        acc_scr[...] = jnp.zeros_like(acc_scr)

    if halo_is_gate:
        g_scr[0:halo, :] = halo_ref[...]
        g_scr[halo:halo + tm, :] = _dot(xb_scr[...], wg_ref[...])
    else:
        g_scr[...] = _dot(xb_scr[...], wg_ref[...])
    val = _dot(xb_scr[xoff:xoff + tm, :], wv_ref[...])
    conv = (cb_ref[...]
            + g_scr[halo - 2 * shift:halo - 2 * shift + tm, :] * cw_ref[0:1, :]
            + g_scr[halo - shift:halo - shift + tm, :] * cw_ref[1:2, :]
            + g_scr[halo:halo + tm, :] * cw_ref[2:3, :])
    h = 0.5 * conv * (1.0 + lax.erf(conv * math.sqrt(0.5))) * val
    for src, dst in zip(next_in, next_out):
        dst[...] = src[...].astype(BF16)
    acc_scr[...] += _dot(h.astype(BF16), wo_ref[...])
    tail_ref[...] = g_scr[halo + tm - tail_rows:halo + tm, :]

    @pl.when(j == pl.num_programs(1) - 1)
    def _():
        y = ALPHA * x_ref[...] + acc_scr[...]
        out_ref[...] = _layer_norm(y, lng_ref[...], lnb_ref[...])


def _prev_rows_spec(tm, rows, cols, grid_rank):
    per_tile = tm // rows
    if grid_rank == 1:
        return pl.BlockSpec((rows, cols), lambda i: (jnp.maximum(i * per_tile - 1, 0), 0))
    return pl.BlockSpec((rows, cols), lambda i, j: (jnp.maximum(i * per_tile - 1, 0), 0))


def _ffn_call(x, gate_prev, weights, conv, ln, layer, *, tm, tf, shift, tiles_per_seq,
              own_f32=False, cast_next=None):
    cw, cb = conv
    lng, lnb = ln
    n = x.shape[0]
    nt = n // tm
    nf = D_FF // tf
    halo_is_gate = gate_prev is not None
    bf16_shapes = [jax.ShapeDtypeStruct((D_MODEL, D_FF), BF16),
                   jax.ShapeDtypeStruct((D_MODEL, D_FF), BF16),
                   jax.ShapeDtypeStruct((D_FF, D_MODEL), BF16)]
    tile_specs = [pl.BlockSpec((D_MODEL, tf), lambda i, j: (0, j)),
                  pl.BlockSpec((D_MODEL, tf), lambda i, j: (0, j)),
                  pl.BlockSpec((tf, D_MODEL), lambda i, j: (j, 0))]
    if halo_is_gate:
        halo_rows = tail_rows = gate_prev.shape[0]
        halo, halo_spec = gate_prev, pl.BlockSpec((halo_rows, tf), lambda i, j: (0, j))
        x_rows = tm
        tail_shape = (tail_rows, D_FF)
        tail_spec = pl.BlockSpec((tail_rows, tf), lambda i, j: (0, j))
    else:
        halo_rows, tail_rows = BF16_ROWS, SUBLANES
        halo, halo_spec = x, _prev_rows_spec(tm, halo_rows, D_MODEL, 2)
        x_rows = tm + halo_rows
        tail_shape = (nt, tail_rows, D_FF)
        tail_spec = pl.BlockSpec((None, tail_rows, tf), lambda i, j: (i, 0, j))
    extra_in, extra_in_specs, extra_specs, extra_shapes = [], [], [], []
    if own_f32:
        w_in, w_out = weights
        w_args = (w_in, w_in, w_out)
        w_specs = [pl.BlockSpec((None, D_MODEL, tf), lambda i, j: (layer, 0, j)),
                   pl.BlockSpec((None, D_MODEL, tf), lambda i, j: (layer, 0, j + nf)),
                   pl.BlockSpec((None, tf, D_MODEL), lambda i, j: (layer, j, 0))]
        extra_specs += tile_specs
        extra_shapes += bf16_shapes
    else:
        w_args, w_specs = weights, tile_specs
    if cast_next is not None:
        w_in, w_out = cast_next
        dr, fr = D_MODEL // nt, tf // nt
        assert dr * nt == D_MODEL and fr * nt == tf and fr % BF16_ROWS == 0
        nxt = layer + 1
        extra_in += [w_in, w_in, w_out]
        extra_in_specs += [pl.BlockSpec((None, dr, tf), lambda i, j: (nxt, i, j)),
                           pl.BlockSpec((None, dr, tf), lambda i, j: (nxt, i, j + nf)),
                           pl.BlockSpec((None, fr, D_MODEL), lambda i, j: (nxt, j * nt + i, 0))]
        extra_specs += [pl.BlockSpec((dr, tf), lambda i, j: (i, j)),
                        pl.BlockSpec((dr, tf), lambda i, j: (i, j)),
                        pl.BlockSpec((fr, D_MODEL), lambda i, j: (j * nt + i, 0))]
        extra_shapes += bf16_shapes
    body = functools.partial(_ffn_body, tm=tm, halo=halo_rows, shift=shift,
                             halo_is_gate=halo_is_gate, tail_rows=tail_rows,
                             tiles_per_seq=tiles_per_seq, own_f32=own_f32,
                             cast_next=cast_next is not None)
    ln_row = 2 * layer + 1
    outs = pl.pallas_call(
        body,
        grid=(nt, nf),
        in_specs=[
            pl.BlockSpec((tm, D_MODEL), lambda i, j: (i, 0)),
            halo_spec,
            *w_specs,
            pl.BlockSpec((None, CONV_W, tf), lambda i, j: (layer, 0, j)),
            pl.BlockSpec((None, 1, tf), lambda i, j: (layer, 0, j)),
            pl.BlockSpec((None, 1, D_MODEL), lambda i, j: (ln_row, 0, 0)),
            pl.BlockSpec((None, 1, D_MODEL), lambda i, j: (ln_row, 0, 0)),
            *extra_in_specs,
        ],
        out_specs=[pl.BlockSpec((tm, D_MODEL), lambda i, j: (i, 0)), tail_spec, *extra_specs],
        out_shape=[jax.ShapeDtypeStruct((n, D_MODEL), F32),
                   jax.ShapeDtypeStruct(tail_shape, F32), *extra_shapes],
        scratch_shapes=[pltpu.VMEM((x_rows, D_MODEL), BF16),
                        pltpu.VMEM((tm + halo_rows, tf), F32),
                        pltpu.VMEM((tm, D_MODEL), F32)],
        compiler_params=_params("parallel", "arbitrary"),
        name="conv_ffn_ln",
    )(x, halo, *w_args, cw, cb, lng, lnb, *extra_in)
    return outs[0], outs[1], tuple(outs[2:])


def _row_chunks(ref):
    rows = ref.shape[0]
    step = min(rows, ROW_CHUNK)
    return [slice(r, r + step) for r in range(0, rows, step)]


def _proj_ln_body(a_ref, x_ref, w_ref, bias_ref, lng_ref, lnb_ref, out_ref):
    for rows in _row_chunks(a_ref):
        h = _dot(a_ref[rows, :], w_ref[...]) + bias_ref[...]
        out_ref[rows, :] = _layer_norm(ALPHA * x_ref[rows, :] + h, lng_ref[...], lnb_ref[...])


def _proj_ln_call(a, x, w, bias, lng, lnb, tm):
    n, k = a.shape
    return pl.pallas_call(
        _proj_ln_body,
        grid=(n // tm,),
        in_specs=[_row_spec(tm, k), _row_spec(tm, D_MODEL), _const_spec(w.shape),
                  _const_spec((1, D_MODEL)), _const_spec((1, D_MODEL)), _const_spec((1, D_MODEL))],
        out_specs=_row_spec(tm, D_MODEL),
        out_shape=jax.ShapeDtypeStruct((n, D_MODEL), F32),
        compiler_params=_params("parallel"),
        name="proj_res_ln",
    )(a, x, w, bias, lng, lnb)


POOL_HALO = 4 * SUBLANES


def _pool_prompt_body(x_ref, halo_ref, w_ref, scale_ref, lng_ref, lnb_ref, out_ref, ext_scr,
                      tmp_scr, *, tm, tiles_per_seq):
    i = pl.program_id(0)
    hb = POOL_HALO
    end = hb + tm
    ext_scr[0:hb, :] = halo_ref[...] * (i % tiles_per_seq != 0).astype(F32)
    ext_scr[hb:end, :] = x_ref[...]
    pos = (i % tiles_per_seq) * tm + lax.broadcasted_iota(jnp.int32, (tm, 1), 0)
    for g, wnd in enumerate(POOL_WINDOWS):
        c0 = g * POOL_GC
        stages = wnd.bit_length() - 1
        src, cols = ext_scr, slice(c0, c0 + POOL_GC)
        for k in range(1, stages + 1):
            w = 2 ** (k - 1)
            start = hb if k == stages else SUBLANES * k
            win = src[start:end, cols] + src[start - w:end - w, cols]
            if k < stages:
                dst = tmp_scr.at[k % 2]
                dst[start:end, :] = win
                src, cols = dst, slice(0, POOL_GC)
        xg = x_ref[:, c0:c0 + POOL_GC]
        inv_cnt = 1.0 / jnp.minimum(pos + 1, wnd).astype(F32)
        pooled = win * inv_cnt - xg
        y = _dot(pooled.astype(BF16), w_ref[g]) * scale_ref[:, c0:c0 + POOL_GC]
        ext_scr[hb:end, c0:c0 + POOL_GC] = ALPHA * xg + y
    out_ref[...] = _layer_norm(ext_scr[hb:end, :], lng_ref[...], lnb_ref[...])


def _pool_prompt_call(x, w, scale, lng, lnb, tm, tiles_per_seq):
    n = x.shape[0]
    hb = POOL_HALO
    body = functools.partial(_pool_prompt_body, tm=tm, tiles_per_seq=tiles_per_seq)
    return pl.pallas_call(
        body,
        grid=(n // tm,),
        in_specs=[_row_spec(tm, D_MODEL),
                  _prev_rows_spec(tm, hb, D_MODEL, 1),
                  _const_spec(w.shape), _const_spec((1, D_MODEL)),
                  _const_spec((1, D_MODEL)), _const_spec((1, D_MODEL))],
        out_specs=_row_spec(tm, D_MODEL),
        out_shape=jax.ShapeDtypeStruct((n, D_MODEL), F32),
        scratch_shapes=[pltpu.VMEM((tm + hb, D_MODEL), F32),
                        pltpu.VMEM((2, tm + hb, POOL_GC), F32)],
        compiler_params=_params("parallel"),
        name="pool_mix_ln_prompt",
    )(x, x, w, scale, lng, lnb)


def _pool_sample_body(prev_ref, x_ref, w_ref, scale_ref, lng_ref, lnb_ref, out_ref, state_ref,
                      y_scr, *, t_new, bt):
    def row(r, cols):
        return prev_ref[r, :, cols] if r < POOL_BUF else x_ref[r - POOL_BUF, :, cols]

    for r in range(POOL_BUF):
        state_ref[r] = row(r + t_new, slice(None))

    for g, wnd in enumerate(POOL_WINDOWS):
        cols = slice(g * POOL_GC, (g + 1) * POOL_GC)
        pooled = []
        for t in range(t_new):
            win = x_ref[t, :, cols]
            for k in range(1, wnd):
                win = win + row(POOL_BUF + t - k, cols)
            pooled.append(win / float(wnd) - x_ref[t, :, cols])
        pooled = jnp.concatenate(pooled, axis=0)
        y = _dot(pooled.astype(BF16), w_ref[g]) * scale_ref[:, cols]
        for t in range(t_new):
            y_scr[t, :, cols] = ALPHA * x_ref[t, :, cols] + y[t * bt:(t + 1) * bt]
    for t in range(t_new):
        out_ref[t] = _layer_norm(y_scr[t], lng_ref[...], lnb_ref[...])


def _pool_sample_call(prev_all, layer, x, w, scale, lng, lnb, bt):
    t_new, nb, _ = x.shape
    body = functools.partial(_pool_sample_body, t_new=t_new, bt=bt)
    return pl.pallas_call(
        body,
        grid=(nb // bt,),
        in_specs=[pl.BlockSpec((None, POOL_BUF, bt, D_MODEL), lambda i: (layer, 0, i, 0)),
                  pl.BlockSpec((t_new, bt, D_MODEL), lambda i: (0, i, 0)),
                  _const_spec(w.shape), _const_spec((1, D_MODEL)),
                  _const_spec((1, D_MODEL)), _const_spec((1, D_MODEL))],
        out_specs=[pl.BlockSpec((t_new, bt, D_MODEL), lambda i: (0, i, 0)),
                   pl.BlockSpec((POOL_BUF, bt, D_MODEL), lambda i: (0, i, 0))],
        out_shape=[jax.ShapeDtypeStruct((t_new, nb, D_MODEL), F32),
                   jax.ShapeDtypeStruct((POOL_BUF, nb, D_MODEL), F32)],
        scratch_shapes=[pltpu.VMEM((t_new, bt, D_MODEL), F32)],
        compiler_params=_params("parallel"),
        name="pool_mix_ln_sample",
    )(prev_all, x, w, scale, lng, lnb)


def _rope_pair(pair, cs):
    t = pair * cs
    return t + pltpu.roll(t, MLA_ROPE, 1)


def _mla_a_body(x_ref, cs_ref, w_ref, gq_ref, gkv_ref, cq_ref, ckv_ref, kpe_ref, kpad_ref):
    for rows in _row_chunks(x_ref):
        a = _dot(x_ref[rows, :].astype(BF16), w_ref[...])
        cq_ref[rows, :] = _rms_norm(a[:, :MLA_Q_RANK], gq_ref[...]).astype(BF16)
        ckv_ref[rows, :] = _rms_norm(a[:, MLA_Q_RANK:MLA_Q_RANK + MLA_KV_RANK], gkv_ref[...])
        r = _rope_pair(a[:, MLA_Q_RANK + MLA_KV_RANK:], cs_ref[rows, :])
        kpe_ref[rows, :] = r[:, :MLA_ROPE]
        lane = lax.broadcasted_iota(jnp.int32, r.shape, 1)
        kpad_ref[rows, :] = jnp.where(lane < MLA_ROPE, r, 0.0).astype(BF16)


def _mla_a_call(x, cs, w, gq, gkv, tm, tiles_per_seq):
    n = x.shape[0]
    return pl.pallas_call(
        _mla_a_body,
        grid=(n // tm,),
        in_specs=[_row_spec(tm, D_MODEL),
                  pl.BlockSpec((tm, LANES), lambda i: (i % tiles_per_seq, 0)),
                  _const_spec(w.shape), _const_spec((1, MLA_Q_RANK)),
                  _const_spec((1, MLA_KV_RANK))],
        out_specs=[_row_spec(tm, MLA_Q_RANK), _row_spec(tm, MLA_KV_RANK),
                   _row_spec(tm, MLA_ROPE), _row_spec(tm, LANES)],
        out_shape=[jax.ShapeDtypeStruct((n, MLA_Q_RANK), BF16),
                   jax.ShapeDtypeStruct((n, MLA_KV_RANK), F32),
                   jax.ShapeDtypeStruct((n, MLA_ROPE), F32),
                   jax.ShapeDtypeStruct((n, LANES), BF16)],
        compiler_params=_params("parallel"),
        name="mla_down_proj",
    )(x, cs, w, gq, gkv)


MLA_QW = MLA_NOPE + 2 * MLA_ROPE


def _mla_q_body(cq_ref, cs_ref, w_ref, q_ref):
    cs = cs_ref[...]
    for h in range(MLA_HEADS):
        c0 = h * MLA_QW
        qh = _dot(cq_ref[...], w_ref[:, c0:c0 + MLA_QW]) * MLA_EXP2_SCALE
        q_ref[:, c0:c0 + MLA_NOPE] = qh[:, :MLA_NOPE].astype(BF16)
        q_ref[:, c0 + MLA_NOPE:c0 + MLA_QW] = _rope_pair(qh[:, MLA_NOPE:], cs).astype(BF16)


def _mla_q_call(cq, cs, w, tm, tiles_per_seq):
    n = cq.shape[0]
    return pl.pallas_call(
        _mla_q_body,
        grid=(n // tm,),
        in_specs=[_row_spec(tm, MLA_Q_RANK),
                  pl.BlockSpec((tm, LANES), lambda i: (i % tiles_per_seq, 0)),
                  _const_spec(w.shape)],
        out_specs=_row_spec(tm, MLA_HEADS * MLA_QW),
        out_shape=jax.ShapeDtypeStruct((n, MLA_HEADS * MLA_QW), BF16),
        compiler_params=_params("parallel"),
        name="mla_q_proj",
    )(cq, cs, w)


def _mla_kv_up_body(ckv_ref, kpad_ref, wuk_ref, wuv_t_ref, k_ref, vt_ref):
    ckv = ckv_ref[...].astype(BF16)
    vt_ref[...] = _dot_nt(wuv_t_ref[...], ckv).astype(BF16)
    k_nope = _dot(ckv, wuk_ref[...]).astype(BF16)
    for h in range(MLA_HEADS):
        c0 = h * MLA_QW
        k_ref[:, c0:c0 + MLA_NOPE] = k_nope[:, h * MLA_NOPE:(h + 1) * MLA_NOPE]
        k_ref[:, c0 + MLA_NOPE:c0 + MLA_QW] = kpad_ref[...]


def _mla_kv_up_call(ckv, kpad, wuk, wuv_t, tm):
    n = ckv.shape[0]
    return pl.pallas_call(
        _mla_kv_up_body,
        grid=(n // tm,),
        in_specs=[_row_spec(tm, MLA_KV_RANK), _row_spec(tm, LANES),
                  _const_spec(wuk.shape), _const_spec(wuv_t.shape)],
        out_specs=[_row_spec(tm, MLA_HEADS * MLA_QW),
                   pl.BlockSpec((MLA_HEADS * MLA_V, tm), lambda i: (0, i))],
        out_shape=[jax.ShapeDtypeStruct((n, MLA_HEADS * MLA_QW), BF16),
                   jax.ShapeDtypeStruct((MLA_HEADS * MLA_V, n), BF16)],
        compiler_params=_params("parallel"),
        name="mla_kv_up_proj",
    )(ckv, kpad, wuk, wuv_t)


def _softmax_init(m_scr, l_scr, acc_scr):
    m_scr[...] = jnp.full(m_scr.shape, NEG_INF, F32)
    l_scr[...] = jnp.zeros(l_scr.shape, F32)
    acc_scr[...] = jnp.zeros(acc_scr.shape, F32)


LOG2E = math.log2(math.e)
MLA_EXP2_SCALE = MLA_SCALE * LOG2E


def _flash_update(s, v, m_ref, l_ref, acc_ref):
    m_prev = m_ref[...]
    m_new = jnp.maximum(m_prev, jnp.max(s, axis=-1, keepdims=True))
    corr = jnp.exp2(m_prev - m_new)
    p = jnp.exp2(s - m_new)
    l_ref[...] = l_ref[...] * corr + jnp.sum(p, axis=-1, keepdims=True)
    acc_ref[...] = acc_ref[...] * corr + _dot(p.astype(BF16), v)
    m_ref[...] = m_new


def _flash_update_t(s_t, v_t, m_ref, l_ref, acc_ref):
    m_prev = m_ref[...]
    m_new = jnp.maximum(m_prev, jnp.max(s_t, axis=0, keepdims=True))
    corr = jnp.exp2(m_prev - m_new)
    p = jnp.exp2(s_t - m_new)
    l_ref[...] = l_ref[...] * corr + jnp.sum(p, axis=0, keepdims=True)
    acc_ref[...] = acc_ref[...] * corr + _dot(v_t, p.astype(BF16))
    m_ref[...] = m_new


def _mla_flash_body(q_ref, k_ref, vt_ref, o_ref, s_scr, m_scr, l_scr, acc_scr, *, t, heads):
    qi = pl.program_id(2)

    def scores(hh, ki):
        k = k_ref[pl.ds(pl.multiple_of(ki * t, t), t), hh * MLA_QW:(hh + 1) * MLA_QW]
        return _dot_nt(k, q_ref[:, hh * MLA_QW:(hh + 1) * MLA_QW])

    def values(hh, ki):
        return vt_ref[hh * MLA_V:(hh + 1) * MLA_V, pl.ds(pl.multiple_of(ki * t, t), t)]

    def advance(ki, slot):
        for hh in range(heads):
            s_scr[hh, 1 - slot] = scores(hh, ki + 1)
        for hh in range(heads):
            _flash_update_t(s_scr[hh, slot], values(hh, ki), m_scr.at[hh], l_scr.at[hh],
                            acc_scr.at[hh])

    def finish(slot):
        key = lax.broadcasted_iota(jnp.int32, (t, t), 0)
        query = lax.broadcasted_iota(jnp.int32, (t, t), 1)
        for hh in range(heads):
            s = jnp.where(key <= query, s_scr[hh, slot], NEG_INF)
            _flash_update_t(s, values(hh, qi), m_scr.at[hh], l_scr.at[hh], acc_scr.at[hh])
            o_t = acc_scr[hh] / l_scr[hh]
            o_ref[:, hh * MLA_V:(hh + 1) * MLA_V] = o_t.T.astype(BF16)

    for hh in range(heads):
        _softmax_init(m_scr.at[hh], l_scr.at[hh], acc_scr.at[hh])
        s_scr[hh, 0] = scores(hh, 0)

    def pair_step(kk, carry):
        advance(2 * kk, 0)
        advance(2 * kk + 1, 1)
        return carry

    lax.fori_loop(0, qi // 2, pair_step, 0)

    @pl.when(qi % 2 == 0)
    def _():
        finish(0)

    @pl.when(qi % 2 == 1)
    def _():
        advance(qi - 1, 0)
        finish(1)


def _mla_flash_call(q, k, v_t, batch, seq):
    t, heads = FLASH_T, FLASH_HEADS
    nq = seq // t
    body = functools.partial(_mla_flash_body, t=t, heads=heads)
    return pl.pallas_call(
        body,
        grid=(batch, MLA_HEADS // heads, nq),
        in_specs=[pl.BlockSpec((t, heads * MLA_QW), lambda b, h, i: (b * nq + i, h)),
                  pl.BlockSpec((seq, heads * MLA_QW), lambda b, h, i: (b, h)),
                  pl.BlockSpec((heads * MLA_V, seq), lambda b, h, i: (h, b))],
        out_specs=pl.BlockSpec((t, heads * MLA_V), lambda b, h, i: (b * nq + i, h)),
        out_shape=jax.ShapeDtypeStruct((batch * seq, MLA_HEADS * MLA_V), BF16),
        scratch_shapes=[pltpu.VMEM((heads, 2, t, t), F32),
                        pltpu.VMEM((heads, 1, t), F32), pltpu.VMEM((heads, 1, t), F32),
                        pltpu.VMEM((heads, MLA_V, t), F32)],
        compiler_params=_params("parallel", "parallel", "arbitrary"),
        name="mla_prompt_attention",
    )(q, k, v_t)


MLA_LATQ = MLA_KV_RANK + 2 * MLA_ROPE


def _mla_absorb_q_body(q_ref, wuk_ref, out_ref):
    q = q_ref[...]
    out_ref[:, :MLA_KV_RANK] = _dot_nt(q[:, :MLA_NOPE], wuk_ref[...]).astype(BF16)
    out_ref[:, MLA_KV_RANK:] = q[:, MLA_NOPE:]


def _mla_absorb_q_call(q, wuk):
    n = q.shape[0]
    return pl.pallas_call(
        _mla_absorb_q_body,
        grid=(MLA_HEADS,),
        in_specs=[pl.BlockSpec((n, MLA_QW), lambda h: (0, h)),
                  pl.BlockSpec((MLA_KV_RANK, MLA_NOPE), lambda h: (0, h))],
        out_specs=pl.BlockSpec((n, MLA_LATQ), lambda h: (0, h)),
        out_shape=jax.ShapeDtypeStruct((n, MLA_HEADS * MLA_LATQ), BF16),
        compiler_params=_params("parallel"),
        name="mla_absorb_q",
    )(q, wuk)


def _mla_decode_body(pt_ref, q_ref, nck_ref, nkp_ref, ck_hbm, kp_hbm, o_ref, ck_buf, kp_buf, sem,
                     ck_scr, kp_scr, m_scr, l_scr, acc_scr, *, pages, chains, t_new, layer):
    b = pl.program_id(0)
    step = pl.program_id(1)
    n_steps = pl.num_programs(1)
    idx = b * n_steps + step
    slot = idx % 2
    per_chain = pages // chains
    span = per_chain * PAGE_SIZE

    def page_copies(page_of, sl):
        copies = []
        for g in range(pages):
            page = page_of(g)
            copies.append(pltpu.make_async_copy(ck_hbm.at[layer, page], ck_buf.at[sl, g],
                                                sem.at[0, sl]))
            copies.append(pltpu.make_async_copy(kp_hbm.at[layer, page], kp_buf.at[sl, g],
                                                sem.at[1, sl]))
        return copies

    @pl.when(idx == 0)
    def _():
        for n, cp in enumerate(page_copies(lambda g: pt_ref[0, g], 0)):
            cp.start(priority=n % 2)

    @pl.when(idx + 1 < pl.num_programs(0) * n_steps)
    def _():
        wrap = step == n_steps - 1
        nb = jnp.where(wrap, b + 1, b)
        ns = jnp.where(wrap, 0, step + 1)
        for n, cp in enumerate(page_copies(lambda g: pt_ref[nb, ns * pages + g], 1 - slot)):
            cp.start(priority=n % 2)

    for cp in page_copies(lambda g: 0, slot):
        cp.wait()

    @pl.when(step == 0)
    def _():
        _softmax_init(m_scr, l_scr, acc_scr)

    q_lat = q_ref[0, :, :MLA_KV_RANK]
    q_pe = q_ref[0, :, MLA_KV_RANK:MLA_KV_RANK + MLA_ROPE]
    scores = []
    for c in range(chains):
        for g in range(c * per_chain, (c + 1) * per_chain):
            ck_scr[g * PAGE_SIZE:(g + 1) * PAGE_SIZE, :] = ck_buf[slot, g].astype(BF16)
            kp_scr[:, g * PAGE_SIZE:(g + 1) * PAGE_SIZE] = kp_buf[slot, g].astype(BF16)
        scores.append(_dot_nt(q_lat, ck_scr[c * span:(c + 1) * span, :])
                      + _dot(q_pe, kp_scr[:, c * span:(c + 1) * span]))
    for c in range(chains):
        _flash_update(scores[c], ck_scr[c * span:(c + 1) * span, :], m_scr.at[c], l_scr.at[c],
                      acc_scr.at[c])

    @pl.when(step == pl.num_programs(1) - 1)
    def _():
        for c in range(1, chains):
            m = jnp.maximum(m_scr[0], m_scr[c])
            w0 = jnp.exp2(m_scr[0] - m)
            wc = jnp.exp2(m_scr[c] - m)
            l_scr[0] = l_scr[0] * w0 + l_scr[c] * wc
            acc_scr[0] = acc_scr[0] * w0 + acc_scr[c] * wc
            m_scr[0] = m
        pad = PAGE_SIZE - nck_ref.shape[1]
        nck = jnp.concatenate([nck_ref[0], jnp.zeros((pad, MLA_KV_RANK), F32)],
                              axis=0).astype(BF16)
        nkp = jnp.concatenate([nkp_ref[0], jnp.zeros((pad, MLA_ROPE), F32)],
                              axis=0).astype(BF16)
        s2 = _dot_nt(q_lat, nck) + _dot_nt(q_pe, nkp)
        tq = lax.broadcasted_iota(jnp.int32, s2.shape, 0) // MLA_HEADS
        tk = lax.broadcasted_iota(jnp.int32, s2.shape, 1)
        s2 = jnp.where((tk <= tq) & (tk < t_new), s2, NEG_INF)
        _flash_update(s2, nck, m_scr.at[0], l_scr.at[0], acc_scr.at[0])
        o_ref[0] = (acc_scr[0] / l_scr[0]).astype(BF16)


def _mla_decode_call(page_table, q, new_ckv, new_kpe, cache_ckv, cache_kpe_t, layer, t_new):
    nb, rows, _ = q.shape
    n_pages = page_table.shape[1]
    pages, chains = DECODE_PAGES, DECODE_CHAINS
    pad_rows = new_ckv.shape[1]

    body = functools.partial(_mla_decode_body, pages=pages, chains=chains, t_new=t_new,
                             layer=layer)
    grid_spec = pltpu.PrefetchScalarGridSpec(
        num_scalar_prefetch=1,
        grid=(nb, n_pages // pages),
        in_specs=[pl.BlockSpec((1, rows, MLA_LATQ), lambda b, s, pt: (b, 0, 0)),
                  pl.BlockSpec((1, pad_rows, MLA_KV_RANK), lambda b, s, pt: (b, 0, 0)),
                  pl.BlockSpec((1, pad_rows, MLA_ROPE), lambda b, s, pt: (b, 0, 0)),
                  pl.BlockSpec(memory_space=pl.ANY),
                  pl.BlockSpec(memory_space=pl.ANY)],
        out_specs=pl.BlockSpec((1, rows, MLA_KV_RANK), lambda b, s, pt: (b, 0, 0)),
        scratch_shapes=[pltpu.VMEM((2, pages, PAGE_SIZE, MLA_KV_RANK), F32),
                        pltpu.VMEM((2, pages, MLA_ROPE, PAGE_SIZE), F32),
                        pltpu.SemaphoreType.DMA((2, 2)),
                        pltpu.VMEM((pages * PAGE_SIZE, MLA_KV_RANK), BF16),
                        pltpu.VMEM((MLA_ROPE, pages * PAGE_SIZE), BF16),
                        pltpu.VMEM((chains, rows, 1), F32), pltpu.VMEM((chains, rows, 1), F32),
                        pltpu.VMEM((chains, rows, MLA_KV_RANK), F32)],
    )
    return pl.pallas_call(
        body,
        grid_spec=grid_spec,
        out_shape=jax.ShapeDtypeStruct((nb, rows, MLA_KV_RANK), BF16),
        compiler_params=_params("arbitrary", "arbitrary"),
        name="mla_paged_decode",
    )(page_table, q, new_ckv, new_kpe, cache_ckv, cache_kpe_t)


def _mla_absorb_o_body(o_ref, wuv_ref, out_ref):
    out_ref[...] = _dot(o_ref[0], wuv_ref[...]).astype(BF16)


def _mla_absorb_o_call(o_lat, wuv):
    _, n, _ = o_lat.shape
    return pl.pallas_call(
        _mla_absorb_o_body,
        grid=(MLA_HEADS,),
        in_specs=[pl.BlockSpec((1, n, MLA_KV_RANK), lambda h: (h, 0, 0)),
                  pl.BlockSpec((MLA_KV_RANK, MLA_V), lambda h: (0, h))],
        out_specs=pl.BlockSpec((n, MLA_V), lambda h: (0, h)),
        out_shape=jax.ShapeDtypeStruct((n, MLA_HEADS * MLA_V), BF16),
        compiler_params=_params("parallel"),
        name="mla_absorb_o",
    )(o_lat, wuv)


SWA_NQ = SWA_HEADS * SWA_HEAD_DIM
SWA_NKV = SWA_KV_HEADS * SWA_HEAD_DIM
SWA_PAIRS = SWA_KV_HEADS // 2


def _swa_regroup(a, axis):
    shape = a.shape
    split = shape[:axis] + (SWA_PAIRS, 2, SWA_GROUP, SWA_HEAD_DIM) + shape[axis + 1:]
    return jnp.swapaxes(a.reshape(split), axis + 1, axis + 2).reshape(shape)


def _swa_slope(head):
    return 2.0 ** (-8.0 * (head + 1) / SWA_HEADS)


def _swa_qkv_body(x_ref, w_ref, b_ref, q_ref, k_ref, v_ref):
    for rows in _row_chunks(x_ref):
        qkv = _dot(x_ref[rows, :].astype(BF16), w_ref[...]) + b_ref[...]
        q_ref[rows, :] = qkv[:, :SWA_NQ].astype(BF16)
        k_ref[rows, :] = qkv[:, SWA_NQ:SWA_NQ + SWA_NKV]
        v_ref[rows, :] = qkv[:, SWA_NQ + SWA_NKV:]


def _swa_qkv_call(x, w, b, tm):
    n = x.shape[0]
    return pl.pallas_call(
        _swa_qkv_body,
        grid=(n // tm,),
        in_specs=[_row_spec(tm, D_MODEL), _const_spec(w.shape), _const_spec(b.shape)],
        out_specs=[_row_spec(tm, SWA_NQ), _row_spec(tm, SWA_NKV), _row_spec(tm, SWA_NKV)],
        out_shape=[jax.ShapeDtypeStruct((n, SWA_NQ), BF16),
                   jax.ShapeDtypeStruct((n, SWA_NKV), F32),
                   jax.ShapeDtypeStruct((n, SWA_NKV), F32)],
        compiler_params=_params("parallel"),
        name="swa_qkv_proj",
    )(x, w, b)


def _swa_heads(q_blk, k_pair, v_pair, sink_ref, p, g, lower, dist, valid):
    lane_q = lax.broadcasted_iota(jnp.int32, q_blk.shape, 1)
    outs = []
    for half in range(2):
        head = (2 * p + half) * SWA_GROUP + g
        in_half = (lane_q >= SWA_HEAD_DIM) if half else (lane_q < SWA_HEAD_DIM)
        qm = jnp.where(in_half, q_blk, jnp.zeros_like(q_blk))
        raw = _dot_nt(qm, k_pair)
        s = jnp.where(lower, raw[:, WINDOW:], raw[:, :WINDOW])
        s = s * (SWA_SCALE * LOG2E) - (_swa_slope(head) * LOG2E) * dist
        s = jnp.where(valid, s, NEG_INF)
        sink = sink_ref[head] * LOG2E
        m = jnp.maximum(jnp.max(s, axis=-1, keepdims=True), sink)
        e = jnp.exp2(s - m)
        denom = jnp.sum(e, axis=-1, keepdims=True) + jnp.exp2(sink - m)
        zero = jnp.zeros_like(e)
        pe = jnp.concatenate([jnp.where(lower, zero, e), jnp.where(lower, e, zero)], axis=1)
        outs.append(_dot(pe.astype(BF16), v_pair) * (1.0 / denom))
    lane_o = lax.broadcasted_iota(jnp.int32, outs[0].shape, 1)
    return jnp.where(lane_o < SWA_HEAD_DIM, outs[0], outs[1])


def _swa_prompt_body(sink_ref, q_ref, kc_ref, kp_ref, vc_ref, vp_ref, o_ref):
    i = pl.program_id(1)
    k = jnp.concatenate([kp_ref[...].astype(BF16), kc_ref[...].astype(BF16)], axis=0)
    v = jnp.concatenate([vp_ref[...].astype(BF16), vc_ref[...].astype(BF16)], axis=0)
    row = lax.broadcasted_iota(jnp.int32, (WINDOW, WINDOW), 0)
    col = lax.broadcasted_iota(jnp.int32, (WINDOW, WINDOW), 1)
    lower = col <= row
    distf = jnp.where(lower, row - col, row - col + WINDOW).astype(F32)
    valid = lower | (i > 0)
    for p in range(SWA_PAIRS):
        kpair = k[:, p * LANES:(p + 1) * LANES]
        vpair = v[:, p * LANES:(p + 1) * LANES]
        for g in range(SWA_GROUP):
            c0 = (p * SWA_GROUP + g) * LANES
            o = _swa_heads(q_ref[:, c0:c0 + LANES], kpair, vpair, sink_ref, p, g, lower, distf,
                           valid)
            o_ref[:, c0:c0 + LANES] = o.astype(BF16)


def _swa_prompt_call(sinks, q, k, v, batch, seq):
    nb = seq // WINDOW
    cur = lambda b, i: (b * nb + i, 0)
    prev = lambda b, i: (b * nb + jnp.maximum(i - 1, 0), 0)
    return pl.pallas_call(
        _swa_prompt_body,
        grid=(batch, nb),
        in_specs=[pl.BlockSpec(memory_space=pltpu.SMEM),
                  pl.BlockSpec((WINDOW, SWA_NQ), cur),
                  pl.BlockSpec((WINDOW, SWA_NKV), cur), pl.BlockSpec((WINDOW, SWA_NKV), prev),
                  pl.BlockSpec((WINDOW, SWA_NKV), cur), pl.BlockSpec((WINDOW, SWA_NKV), prev)],
        out_specs=pl.BlockSpec((WINDOW, SWA_NQ), cur),
        out_shape=jax.ShapeDtypeStruct((batch * seq, SWA_NQ), BF16),
        compiler_params=_params("parallel", "arbitrary"),
        name="swa_prompt_attention",
    )(sinks, q, k, k, v, v)


def _swa_sample_body(sink_ref, q_ref, kc_ref, vc_ref, kn_ref, vn_ref, o_ref, ko_ref, vo_ref,
                     *, bt, t_pad, t_new):
    n_keys = 2 * WINDOW
    fill = jnp.zeros((WINDOW - t_pad, LANES), F32)
    rows = 2 * SWA_GROUP * t_pad
    ridx = lax.broadcasted_iota(jnp.int32, (rows, 1), 0) // t_pad
    tq = lax.broadcasted_iota(jnp.int32, (rows, n_keys), 0) % t_pad
    col = lax.broadcasted_iota(jnp.int32, (rows, n_keys), 1)
    dist = tq + WINDOW - col
    valid = (dist >= 0) & (dist < WINDOW)
    distf = dist.astype(F32)
    lane = lax.broadcasted_iota(jnp.int32, (t_pad, LANES), 1)
    bias, sink_cols = [], []
    for p in range(SWA_PAIRS):
        slope = jnp.zeros((rows, 1), F32)
        sink = jnp.zeros((rows, 1), F32)
        for half in range(2):
            for g in range(SWA_GROUP):
                head = (2 * p + half) * SWA_GROUP + g
                here = ridx == half * SWA_GROUP + g
                slope = jnp.where(here, _swa_slope(head), slope)
                sink = jnp.where(here, sink_ref[head], sink)
        bias.append(slope * distf)
        sink_cols.append(sink)

    def one_seq(bb, carry):
        scores = []
        for p in range(SWA_PAIRS):
            pair = slice(p * LANES, (p + 1) * LANES)
            k_old = kc_ref[bb, pair, :].astype(BF16)
            k_new = jnp.concatenate([kn_ref[bb, :, pair], fill], axis=0).astype(BF16)
            blocks = [q_ref[bb, :, (p * SWA_GROUP + g) * LANES:(p * SWA_GROUP + g + 1) * LANES]
                      .astype(F32) for g in range(SWA_GROUP)]
            stack = ([jnp.where(lane < SWA_HEAD_DIM, blk, 0.0) for blk in blocks]
                     + [jnp.where(lane >= SWA_HEAD_DIM, blk, 0.0) for blk in blocks])
            qs = jnp.concatenate(stack, axis=0).astype(BF16)
            scores.append(jnp.concatenate([_dot(qs, k_old), _dot_nt(qs, k_new)], axis=1))
        probs = []
        for p in range(SWA_PAIRS):
            s = scores[p] * SWA_SCALE - bias[p]
            s = jnp.where(valid, s, NEG_INF)
            m = jnp.maximum(jnp.max(s, axis=-1, keepdims=True), sink_cols[p])
            e = jnp.exp(s - m)
            denom = jnp.sum(e, axis=-1, keepdims=True) + jnp.exp(sink_cols[p] - m)
            probs.append((e / denom).astype(BF16))
        for p in range(SWA_PAIRS):
            pair = slice(p * LANES, (p + 1) * LANES)
            v_old = vc_ref[bb, pair, :].astype(BF16)
            v_new = jnp.concatenate([vn_ref[bb, :, pair], fill], axis=0).astype(BF16)
            pr = probs[p]
            o = _dot_nt(pr[:, :WINDOW], v_old) + _dot(pr[:, WINDOW:], v_new)
            for g in range(SWA_GROUP):
                lo = o[g * t_pad:(g + 1) * t_pad]
                hi = o[(SWA_GROUP + g) * t_pad:(SWA_GROUP + g + 1) * t_pad]
                c0 = (p * SWA_GROUP + g) * LANES
                o_ref[bb, :, c0:c0 + LANES] = jnp.where(lane < SWA_HEAD_DIM, lo, hi).astype(BF16)
        tcol = lax.broadcasted_iota(jnp.int32, (LANES, WINDOW), 1)
        for old_ref, new_ref, out_ref in ((kc_ref, kn_ref, ko_ref), (vc_ref, vn_ref, vo_ref)):
            for p in range(SWA_PAIRS):
                pair = slice(p * LANES, (p + 1) * LANES)
                new_t = jnp.concatenate([new_ref[bb, :, pair], fill], axis=0).T
                merged = jnp.where(tcol < t_new, new_t, old_ref[bb, pair, :])
                out_ref[bb, pair, :] = pltpu.roll(merged, WINDOW - t_new, 1)
        return carry

    lax.fori_loop(0, bt, one_seq, 0)


def _swa_sample_call(sinks, q, kc, vc, kn, vn, t_new):
    nb, t_pad, _ = q.shape
    bt = SWA_SEQ_TILE
    body = functools.partial(_swa_sample_body, bt=bt, t_pad=t_pad, t_new=t_new)
    seq3 = lambda rows, cols: pl.BlockSpec((bt, rows, cols), lambda i: (i, 0, 0))
    return pl.pallas_call(
        body,
        grid=(nb // bt,),
        in_specs=[pl.BlockSpec(memory_space=pltpu.SMEM),
                  seq3(t_pad, SWA_NQ), seq3(SWA_NKV, WINDOW), seq3(SWA_NKV, WINDOW),
                  seq3(t_pad, SWA_NKV), seq3(t_pad, SWA_NKV)],
        out_specs=[seq3(t_pad, SWA_NQ), seq3(SWA_NKV, WINDOW), seq3(SWA_NKV, WINDOW)],
        out_shape=[jax.ShapeDtypeStruct((nb, t_pad, SWA_NQ), BF16),
                   jax.ShapeDtypeStruct((nb, SWA_NKV, WINDOW), F32),
                   jax.ShapeDtypeStruct((nb, SWA_NKV, WINDOW), F32)],
        compiler_params=_params("parallel"),
        name="swa_sample_attention",
    )(sinks, q, kc, vc, kn, vn)


def _rot_cols(w):
    half = w.shape[-1] // 2
    return jnp.concatenate([-w[..., half:], w[..., :half]], axis=-1)


def _rope_table(pos):
    half = MLA_ROPE // 2
    inv = ROPE_THETA ** (-jnp.arange(half, dtype=F32) / half)
    ang = pos.astype(F32)[:, None] * inv[None, :]
    cos, sin = jnp.cos(ang), jnp.sin(ang)
    return jnp.concatenate([cos, cos, sin, sin], axis=-1)


def _time_major(a, nb, t):
    return jnp.swapaxes(a, 0, 1).reshape((t * nb,) + a.shape[2:])


def _batch_major(a, nb, t):
    return jnp.swapaxes(a.reshape((t, nb) + a.shape[1:]), 0, 1)


def kernel(x_prompt, x_sample, state_pool, cache_mla_ckv, cache_mla_kpe, page_table, cache_swa_k, cache_swa_v, state_ffn_conv, ln_g, ln_b, pool_w, pool_scale, mla_w_a, mla_g_q, mla_g_kv, mla_w_uq, mla_w_uk, mla_w_uv, mla_w_o, swa_w_qkv, swa_b_qkv, swa_sinks, swa_w_o, swa_b_o, ffn_w_in, ffn_conv_w, ffn_conv_b, ffn_w_out):
    B, S, D = x_prompt.shape
    NB, T, _ = x_sample.shape
    NS = NB * T
    tm = ROW_TILE
    tps = S // tm
    assert S % tm == 0 and NS == tm and S % FLASH_T == 0 and T <= SUBLANES

    xp = x_prompt.reshape(B * S, D)
    xs = _time_major(x_sample, NB, T)
    cs_p = _rope_table(jnp.arange(S))
    cs_s = jnp.repeat(_rope_table(PAST_LEN + jnp.arange(T)), NB, axis=0)
    zero_bias = jnp.zeros((1, D), F32)
    ffn_conv = (ffn_conv_w, ffn_conv_b.reshape(DEPTH, 1, D_FF))
    ffn_f32 = (ffn_w_in, ffn_w_out)
    ln_rows = (ln_g.reshape(2 * DEPTH, 1, D), ln_b.reshape(2 * DEPTH, 1, D))

    pool_p, pool_s, ckv_p, ckv_s, kpe_p, kpe_s = [], [], [], [], [], []
    swk_p, swk_s, swv_p, swv_s, conv_p, conv_s = [], [], [], [], [], []

    for i in range(DEPTH):
        kind, j = i % N_MIXERS, i // N_MIXERS
        g1, b1 = ln_g[i, 0][None, :], ln_b[i, 0][None, :]
        if kind == 0:
            w = pool_w[j].astype(BF16)
            sc = pool_scale[j][None, :]
            pool_p.append(xp.reshape(B, S, D)[:, S - POOL_BUF:])
            prev_tm = jnp.swapaxes(state_pool, 1, 2)
            xs, state = _pool_sample_call(prev_tm, j, xs.reshape(T, NB, D), w, sc, g1, b1, 32)
            xs = xs.reshape(NS, D)
            pool_s.append(jnp.swapaxes(state, 0, 1))
            xp = _pool_prompt_call(xp, w, sc, g1, b1, tm, tps)
        elif kind == 1:
            w_a = mla_w_a[j]
            n_lat = MLA_Q_RANK + MLA_KV_RANK
            w_a_ext = jnp.concatenate([w_a, _rot_cols(w_a[:, n_lat:])], axis=1).astype(BF16)
            w_uq = mla_w_uq[j]
            w_uq_ext = jnp.concatenate([w_uq, _rot_cols(w_uq[..., MLA_NOPE:])], axis=-1)
            w_uq_ext = w_uq_ext.reshape(MLA_Q_RANK, MLA_HEADS * MLA_QW).astype(BF16)
            w_uk = mla_w_uk[j].reshape(MLA_KV_RANK, MLA_HEADS * MLA_NOPE).astype(BF16)
            w_uv = mla_w_uv[j].reshape(MLA_KV_RANK, MLA_HEADS * MLA_V).astype(BF16)
            w_o = mla_w_o[j].reshape(MLA_HEADS * MLA_V, D).astype(BF16)
            gq, gkv = mla_g_q[j][None, :], mla_g_kv[j][None, :]

            cq, ckv, kpe, kpad = _mla_a_call(xp, cs_p, w_a_ext, gq, gkv, tm, tps)
            q = _mla_q_call(cq, cs_p, w_uq_ext, tm, tps)
            k_full, v_full = _mla_kv_up_call(ckv, kpad, w_uk, w_uv.T, tm)
            o = _mla_flash_call(q, k_full, v_full, B, S)
            ckv_p.append(ckv.reshape(B, S, MLA_KV_RANK))
            kpe_p.append(kpe.reshape(B, S, MLA_ROPE))
            xp = _proj_ln_call(o, xp, w_o, zero_bias, g1, b1, tm)

            cq, ckv, kpe, _ = _mla_a_call(xs, cs_s, w_a_ext, gq, gkv, tm, 1)
            q = _mla_q_call(cq, cs_s, w_uq_ext, tm, 1)
            q_abs = _mla_absorb_q_call(q, w_uk)
            q_abs = _batch_major(q_abs.reshape(NS, MLA_HEADS, MLA_LATQ), NB, T)
            q_abs = q_abs.reshape(NB, T * MLA_HEADS, MLA_LATQ)
            ckv_bm = _batch_major(ckv, NB, T)
            kpe_bm = _batch_major(kpe, NB, T)
            pad = ((0, 0), (0, SUBLANES - T), (0, 0))
            o_lat = _mla_decode_call(page_table, q_abs, jnp.pad(ckv_bm, pad), jnp.pad(kpe_bm, pad),
                                     cache_mla_ckv, jnp.swapaxes(cache_mla_kpe, 2, 3), j, T)
            o_lat = o_lat.reshape(NB, T, MLA_HEADS, MLA_KV_RANK).transpose(2, 1, 0, 3)
            o = _mla_absorb_o_call(o_lat.reshape(MLA_HEADS, NS, MLA_KV_RANK), w_uv)
            ckv_s.append(ckv_bm)
            kpe_s.append(kpe_bm)
            xs = _proj_ln_call(o, xs, w_o, zero_bias, g1, b1, tm)
        else:
            w_qkv, b_qkv = swa_w_qkv[j], swa_b_qkv[j]
            w_qkv = jnp.concatenate([_swa_regroup(w_qkv[:, :SWA_NQ], 1), w_qkv[:, SWA_NQ:]],
                                    axis=1)
            b_qkv = jnp.concatenate([_swa_regroup(b_qkv[:SWA_NQ], 0), b_qkv[SWA_NQ:]])[None, :]
            w_qkv = w_qkv.astype(BF16)
            w_o = _swa_regroup(swa_w_o[j], 0).astype(BF16)
            b_o = swa_b_o[j][None, :]
            sinks = swa_sinks[j]

            q, k, v = _swa_qkv_call(xp, w_qkv, b_qkv, tm // 2)
            o = _swa_prompt_call(sinks, q, k, v, B, S)
            kv_shape = (B, WINDOW, SWA_KV_HEADS, SWA_HEAD_DIM)
            swk_p.append(k.reshape(B, S, SWA_NKV)[:, S - WINDOW:].reshape(kv_shape))
            swv_p.append(v.reshape(B, S, SWA_NKV)[:, S - WINDOW:].reshape(kv_shape))
            xp = _proj_ln_call(o, xp, w_o, b_o, g1, b1, tm)

            q, k, v = _swa_qkv_call(xs, w_qkv, b_qkv, tm // 2)
            pad = ((0, 0), (0, SUBLANES - T), (0, 0))
            q_bm = jnp.pad(_batch_major(q, NB, T), pad)
            k_bm, v_bm = _batch_major(k, NB, T), _batch_major(v, NB, T)
            kc = jnp.transpose(cache_swa_k[j], (0, 2, 3, 1)).reshape(NB, SWA_NKV, WINDOW)
            vc = jnp.transpose(cache_swa_v[j], (0, 2, 3, 1)).reshape(NB, SWA_NKV, WINDOW)
            o, k_upd, v_upd = _swa_sample_call(sinks, q_bm, kc, vc, jnp.pad(k_bm, pad),
                                               jnp.pad(v_bm, pad), T)
            o = _time_major(o[:, :T], NB, T)
            kv_shape = (NB, SWA_KV_HEADS, SWA_HEAD_DIM, WINDOW)
            swk_s.append(jnp.transpose(k_upd.reshape(kv_shape), (0, 3, 1, 2)))
            swv_s.append(jnp.transpose(v_upd.reshape(kv_shape), (0, 3, 1, 2)))
            xs = _proj_ln_call(o, xs, w_o, b_o, g1, b1, tm)

        prev = _time_major(state_ffn_conv[i], NB, CONV_W - 1)
        if i == 0:
            xs, tails, w_bf16 = _ffn_call(xs, prev, ffn_f32, ffn_conv, ln_rows, i, tm=tm,
                                          tf=FFN_SAMPLE_TF, shift=NB, tiles_per_seq=1,
                                          own_f32=True)
        else:
            xs, tails, _ = _ffn_call(xs, prev, w_bf16, ffn_conv, ln_rows, i, tm=tm, tf=FFN_TF,
                                     shift=NB, tiles_per_seq=1)
        conv_s.append(_batch_major(tails, NB, CONV_W - 1))
        xp, tails, w_next = _ffn_call(xp, None, w_bf16, ffn_conv, ln_rows, i, tm=tm, tf=FFN_TF,
                                      shift=1, tiles_per_seq=tps,
                                      cast_next=ffn_f32 if i + 1 < DEPTH else None)
        w_bf16 = w_next
        conv_p.append(tails.reshape(B, tps, SUBLANES, D_FF)[:, -1, SUBLANES - (CONV_W - 1):])

    y_p = xp.reshape(B, S, D)
    y_s = _batch_major(xs, NB, T)
    return (y_p, y_s, jnp.stack(pool_p), jnp.stack(pool_s), jnp.stack(ckv_p), jnp.stack(ckv_s),
            jnp.stack(kpe_p), jnp.stack(kpe_s), jnp.stack(swk_p), jnp.stack(swk_s),
            jnp.stack(swv_p), jnp.stack(swv_s), jnp.stack(conv_p), jnp.stack(conv_s))
```
